```python
import jax, jax.numpy as jnp
from jax import lax
import numpy as np

D_MODEL = 2048
BATCH = 8
SEQ = 2048
DEPTH = 1
DEC_BATCH = 4
DEC_SEQ = 2048
PAST_LEN = 128

HEAD_DIM = 128
N_HEADS_A = D_MODEL // 256
N_KV_A = N_HEADS_A // 4
GQA_GROUP = N_HEADS_A // N_KV_A
N_HEADS_B = D_MODEL // 256
WIDTH_A = N_HEADS_A * HEAD_DIM
KV_WIDTH_A = N_KV_A * HEAD_DIM
WIDTH_B = N_HEADS_B * HEAD_DIM
IN_WIDTH = WIDTH_A + 2 * KV_WIDTH_A + 3 * WIDTH_B + 2 * D_MODEL
D_FF = 256 * ((8 * D_MODEL // 3 + 255) // 256)
D_PLE = 256
GRID_W = 64
WIN_ROWS = 8
WIN_COLS = 16
Q_BLOCK = 128
ROPE_THETA = 10000.0
EPS = 1e-6

kernel_name = "hybrid_gqa_natten_macaron_encoder"


def _rms_norm(x, g):
    xf = x.astype(jnp.float32)
    y = xf * lax.rsqrt(jnp.mean(xf * xf, axis=-1, keepdims=True) + EPS)
    return (y * g.astype(jnp.float32)).astype(x.dtype)


def _swiglu(x, w_gate, w_up, w_down):
    return (jax.nn.silu(x @ w_gate) * (x @ w_up)) @ w_down


def _axial_rope_tables(s):
    n_freq = HEAD_DIM // 4
    inv_freq = ROPE_THETA ** (-jnp.arange(n_freq, dtype=jnp.float32) / n_freq)
    t = jnp.arange(s)
    row = (t // GRID_W).astype(jnp.float32)
    col = (t % GRID_W).astype(jnp.float32)
    ang = jnp.concatenate([row[:, None] * inv_freq[None], col[:, None] * inv_freq[None]], axis=-1)
    return jnp.cos(ang), jnp.sin(ang)


def _apply_rope(x, cos, sin):
    xf = x.astype(jnp.float32).reshape(x.shape[:-1] + (HEAD_DIM // 2, 2))
    x0, x1 = xf[..., 0], xf[..., 1]
    c = cos[None, :, None, :]
    s_ = sin[None, :, None, :]
    out = jnp.stack([x0 * c - x1 * s_, x0 * s_ + x1 * c], axis=-1)
    return out.reshape(x.shape).astype(x.dtype)


def _gqa_attention(q, k, v):
    b, s, _, d = q.shape
    nb = s // Q_BLOCK
    scale = d ** -0.5
    qb = q.reshape(b, nb, Q_BLOCK, N_KV_A, GQA_GROUP, d).transpose(1, 0, 2, 3, 4, 5)

    def block(qblk):
        sc = jnp.einsum('bqkgd,bskd->bkgqs', qblk, k).astype(jnp.float32) * scale
        pr = jax.nn.softmax(sc, axis=-1).astype(v.dtype)
        return jnp.einsum('bkgqs,bskd->bqkgd', pr, v)

    o = lax.map(block, qb)
    return o.transpose(1, 0, 2, 3, 4, 5).reshape(b, s, N_KV_A * GQA_GROUP * d)


def _neighbourhood_tables(s):
    rows = s // GRID_W
    wr = min(WIN_ROWS, rows)
    wc = WIN_COLS
    t = jnp.arange(s)
    r = t // GRID_W
    c = t % GRID_W
    rs = jnp.clip(r - wr // 2, 0, rows - wr)
    cs = jnp.clip(c - wc // 2, 0, GRID_W - wc)
    kr = jnp.broadcast_to(rs[:, None, None] + jnp.arange(wr)[None, :, None], (s, wr, wc))
    kc = jnp.broadcast_to(cs[:, None, None] + jnp.arange(wc)[None, None, :], (s, wr, wc))
    idx = (kr * GRID_W + kc).reshape(s, wr * wc)
    dr = kr - r[:, None, None] + (WIN_ROWS - 1)
    dc = kc - c[:, None, None] + (WIN_COLS - 1)
    return idx, dr, dc, rows, wr * wc


def _neighbourhood_attention(q, k, v, rpb, idx, dr, dc, rows, nk):
    b, s, h, d = q.shape
    scale = d ** -0.5
    bias = rpb[:, dr, dc].reshape(h, s, nk)
    qb = q.reshape(b, rows, GRID_W, h, d).transpose(1, 0, 2, 3, 4)
    idx_b = idx.reshape(rows, GRID_W, nk)
    bias_b = bias.reshape(h, rows, GRID_W, nk).transpose(1, 0, 2, 3)

    def block(args):
        qblk, iblk, bblk = args
        kg = jnp.take(k, iblk, axis=1)
        vg = jnp.take(v, iblk, axis=1)
        sc = jnp.einsum('bqhd,bqnhd->bhqn', qblk, kg).astype(jnp.float32) * scale
        sc = sc + bblk.astype(jnp.float32)[None]
        pr = jax.nn.softmax(sc, axis=-1).astype(v.dtype)
        return jnp.einsum('bhqn,bqnhd->bqhd', pr, vg)

    o = lax.map(block, (qb, idx_b, bias_b))
    return o.transpose(1, 0, 2, 3, 4).reshape(b, s, h * d)


def _encoder(x, p, ffn1_norm, ffn1_w_gate, ffn1_w_up, ffn1_w_down, mix_norm, w_in, q_norm, k_norm,
             nat_rpb, w_branch_a, w_branch_b, w_out, ffn2_norm, ffn2_w_gate, ffn2_w_up, ffn2_w_down,
             ple_norm, w_ple_gate, w_ple_proj, final_norm):
    b, s, _ = x.shape
    cos, sin = _axial_rope_tables(s)
    idx, dr, dc, rows, nk = _neighbourhood_tables(s)
    splits = np.cumsum([WIDTH_A, KV_WIDTH_A, KV_WIDTH_A, WIDTH_B, WIDTH_B, WIDTH_B, D_MODEL]).tolist()
    h = x
    for i in range(DEPTH):
        h = h + 0.5 * _swiglu(_rms_norm(h, ffn1_norm[i]), ffn1_w_gate[i], ffn1_w_up[i], ffn1_w_down[i])
        u = _rms_norm(h, mix_norm[i])
        z = u @ w_in[i]
        q_a, k_a, v_a, q_b, k_b, v_b, g_a, g_b = jnp.split(z, splits, axis=-1)
        q_a = _rms_norm(q_a.reshape(b, s, N_HEADS_A, HEAD_DIM), q_norm[i])
        k_a = _rms_norm(k_a.reshape(b, s, N_KV_A, HEAD_DIM), k_norm[i])
        q_a = _apply_rope(q_a, cos, sin)
        k_a = _apply_rope(k_a, cos, sin)
        v_a = v_a.reshape(b, s, N_KV_A, HEAD_DIM)
        out_a = _gqa_attention(q_a, k_a, v_a)
        out_b = _neighbourhood_attention(
            q_b.reshape(b, s, N_HEADS_B, HEAD_DIM), k_b.reshape(b, s, N_HEADS_B, HEAD_DIM),
            v_b.reshape(b, s, N_HEADS_B, HEAD_DIM), nat_rpb[i], idx, dr, dc, rows, nk)
        merged = jax.nn.sigmoid(g_a) * (out_a @ w_branch_a[i]) + jax.nn.sigmoid(g_b) * (out_b @ w_branch_b[i])
        h = h + merged @ w_out[i]
        h = h + 0.5 * _swiglu(_rms_norm(h, ffn2_norm[i]), ffn2_w_gate[i], ffn2_w_up[i], ffn2_w_down[i])
        gate = jax.nn.sigmoid(_rms_norm(h, ple_norm[i]) @ w_ple_gate[i])
        h = h + gate * (p[i] @ w_ple_proj[i])
    return _rms_norm(h, final_norm)


def setup_inputs(seed: int = 0) -> dict:
    key = jax.random.key(seed)
    ks = jax.random.split(key, 24)
    f32 = jnp.float32

    def w(k, shape, fan_in):
        return jax.random.normal(k, shape, f32) * (fan_in ** -0.5)

    def gain(k, shape):
        return 1.0 + 0.02 * jax.random.normal(k, shape, f32)

    return {
        "x_prompt": jax.random.normal(ks[0], (BATCH, SEQ, D_MODEL), f32),
        "x_sample": jax.random.normal(ks[1], (DEC_BATCH, DEC_SEQ, D_MODEL), f32),
        "p_prompt": jax.random.normal(ks[2], (DEPTH, BATCH, SEQ, D_PLE), f32),
        "p_sample": jax.random.normal(ks[3], (DEPTH, DEC_BATCH, DEC_SEQ, D_PLE), f32),
        "ffn1_norm": gain(ks[4], (DEPTH, D_MODEL)),
        "ffn1_w_gate": w(ks[5], (DEPTH, D_MODEL, D_FF), D_MODEL),
        "ffn1_w_up": w(ks[6], (DEPTH, D_MODEL, D_FF), D_MODEL),
        "ffn1_w_down": w(ks[7], (DEPTH, D_FF, D_MODEL), D_FF),
        "mix_norm": gain(ks[8], (DEPTH, D_MODEL)),
        "w_in": w(ks[9], (DEPTH, D_MODEL, IN_WIDTH), D_MODEL),
        "q_norm": gain(ks[10], (DEPTH, HEAD_DIM)),
        "k_norm": gain(ks[11], (DEPTH, HEAD_DIM)),
        "nat_rpb": 0.1 * jax.random.normal(ks[12], (DEPTH, N_HEADS_B, 2 * WIN_ROWS - 1, 2 * WIN_COLS - 1), f32),
        "w_branch_a": w(ks[13], (DEPTH, WIDTH_A, D_MODEL), WIDTH_A),
        "w_branch_b": w(ks[14], (DEPTH, WIDTH_B, D_MODEL), WIDTH_B),
        "w_out": w(ks[15], (DEPTH, D_MODEL, D_MODEL), D_MODEL),
        "ffn2_norm": gain(ks[16], (DEPTH, D_MODEL)),
        "ffn2_w_gate": w(ks[17], (DEPTH, D_MODEL, D_FF), D_MODEL),
        "ffn2_w_up": w(ks[18], (DEPTH, D_MODEL, D_FF), D_MODEL),
        "ffn2_w_down": w(ks[19], (DEPTH, D_FF, D_MODEL), D_FF),
        "ple_norm": gain(ks[20], (DEPTH, D_MODEL)),
        "w_ple_gate": w(ks[21], (DEPTH, D_MODEL, D_MODEL), D_MODEL),
        "w_ple_proj": w(ks[22], (DEPTH, D_PLE, D_MODEL), D_PLE),
        "final_norm": gain(ks[23], (D_MODEL,)),
    }


def reference(x_prompt, x_sample, p_prompt, p_sample, ffn1_norm, ffn1_w_gate, ffn1_w_up, ffn1_w_down,
              mix_norm, w_in, q_norm, k_norm, nat_rpb, w_branch_a, w_branch_b, w_out, ffn2_norm,
              ffn2_w_gate, ffn2_w_up, ffn2_w_down, ple_norm, w_ple_gate, w_ple_proj, final_norm):
    y_prompt = _encoder(x_prompt, p_prompt, ffn1_norm, ffn1_w_gate, ffn1_w_up, ffn1_w_down, mix_norm, w_in,
                        q_norm, k_norm, nat_rpb, w_branch_a, w_branch_b, w_out, ffn2_norm, ffn2_w_gate,
                        ffn2_w_up, ffn2_w_down, ple_norm, w_ple_gate, w_ple_proj, final_norm)
    y_sample = _encoder(x_sample, p_sample, ffn1_norm, ffn1_w_gate, ffn1_w_up, ffn1_w_down, mix_norm, w_in,
                        q_norm, k_norm, nat_rpb, w_branch_a, w_branch_b, w_out, ffn2_norm, ffn2_w_gate,
                        ffn2_w_up, ffn2_w_down, ple_norm, w_ple_gate, w_ple_proj, final_norm)
    return (y_prompt, y_sample)
```

```python
import functools

import numpy as np
import jax
import jax.numpy as jnp
from jax import lax
from jax.experimental import pallas as pl
from jax.experimental.pallas import tpu as pltpu

F32 = jnp.float32
BF16 = jnp.bfloat16

D_MODEL = 2048
HEAD_DIM = 128
N_HEADS_A = 8
N_KV_A = 2
GQA_GROUP = N_HEADS_A // N_KV_A
N_HEADS_B = 8
WIDTH_A = N_HEADS_A * HEAD_DIM
KV_WIDTH_A = N_KV_A * HEAD_DIM
WIDTH_B = N_HEADS_B * HEAD_DIM
QKV_A_WIDTH = WIDTH_A + 2 * KV_WIDTH_A
D_FF = 5632
D_PLE = 256
GRID_W = 64
WIN_ROWS = 8
WIN_COLS = 16
ROPE_THETA = 10000.0
EPS = 1e-6
ATTN_SCALE = HEAD_DIM ** -0.5
MASKED = -1e30

NBR_Q_ROWS = 4
NBR_KEY_ROWS = 12
NBR_Q = NBR_Q_ROWS * GRID_W
NBR_KEYS = NBR_KEY_ROWS * GRID_W

V7X_VMEM_BYTES = 64 * 1024 * 1024
MIB = 1024 * 1024


def _params(semantics, vmem_mib):
    assert vmem_mib * MIB < V7X_VMEM_BYTES
    return pltpu.CompilerParams(dimension_semantics=semantics, vmem_limit_bytes=vmem_mib * MIB)


def _rms(x):
    return x * lax.rsqrt(jnp.mean(x * x, axis=-1, keepdims=True) + EPS)


def _ffn_kernel(x_ref, g1_ref, wg_ref, wu_ref, wd_ref, g2_ref, h_ref, n_ref, xn_ref, acc_ref):
    j = pl.program_id(1)

    @pl.when(j == 0)
    def _():
        xn_ref[...] = (_rms(x_ref[...]) * g1_ref[...]).astype(BF16)
        acc_ref[...] = jnp.zeros_like(acc_ref)

    xn = xn_ref[...]
    g = jnp.dot(xn, wg_ref[...], preferred_element_type=F32)
    u = jnp.dot(xn, wu_ref[...], preferred_element_type=F32)
    a = (g * jax.nn.sigmoid(g) * u).astype(BF16)
    acc_ref[...] += jnp.dot(a, wd_ref[...], preferred_element_type=F32)

    @pl.when(j == pl.num_programs(1) - 1)
    def _():
        h = x_ref[...] + 0.5 * acc_ref[...]
        h_ref[...] = h
        n_ref[...] = (_rms(h) * g2_ref[...]).astype(BF16)


def _ffn(x, g1, wg, wu, wd, g2, *, tm=512, tf=512):
    t, d = x.shape
    dff = wg.shape[1]
    assert t % tm == 0 and dff % tf == 0
    return pl.pallas_call(
        _ffn_kernel,
        out_shape=(jax.ShapeDtypeStruct((t, d), F32), jax.ShapeDtypeStruct((t, d), BF16)),
        grid=(t // tm, dff // tf),
        in_specs=[
            pl.BlockSpec((tm, d), lambda i, j: (i, 0)),
            pl.BlockSpec((1, d), lambda i, j: (0, 0)),
            pl.BlockSpec((d, tf), lambda i, j: (0, j)),
            pl.BlockSpec((d, tf), lambda i, j: (0, j)),
            pl.BlockSpec((tf, d), lambda i, j: (j, 0)),
            pl.BlockSpec((1, d), lambda i, j: (0, 0)),
        ],
        out_specs=(
            pl.BlockSpec((tm, d), lambda i, j: (i, 0)),
            pl.BlockSpec((tm, d), lambda i, j: (i, 0)),
        ),
        scratch_shapes=[pltpu.VMEM((tm, d), BF16), pltpu.VMEM((tm, d), F32)],
        compiler_params=_params(("parallel", "arbitrary"), 48),
        name="ffn",
    )(x, g1, wg, wu, wd, g2)


def _qkv_a_kernel(u_ref, w_ref, gain_ref, cos_ref, sin_ref, q_ref, k_ref, v_ref):
    z = jnp.dot(u_ref[...], w_ref[...], preferred_element_type=F32)
    c = cos_ref[...]
    s = sin_ref[...]
    for hh in range(N_HEADS_A + N_KV_A):
        zh = z[:, hh * HEAD_DIM:(hh + 1) * HEAD_DIM]
        zn = _rms(zh) * gain_ref[hh:hh + 1, :]
        out = (zn * c + pltpu.roll(zn, HEAD_DIM // 2, 1) * s).astype(BF16)
        if hh < N_HEADS_A:
            q_ref[:, hh * HEAD_DIM:(hh + 1) * HEAD_DIM] = out
        else:
            kk = hh - N_HEADS_A
            k_ref[:, kk * HEAD_DIM:(kk + 1) * HEAD_DIM] = out
    v_ref[...] = z[:, WIDTH_A + KV_WIDTH_A:].astype(BF16)


def _qkv_a(u, w, gain, cos, sin, seq, *, tm=512):
    t, d = u.shape
    assert t % tm == 0 and seq % tm == 0
    seq_blocks = seq // tm
    return pl.pallas_call(
        _qkv_a_kernel,
        out_shape=(
            jax.ShapeDtypeStruct((t, WIDTH_A), BF16),
            jax.ShapeDtypeStruct((t, KV_WIDTH_A), BF16),
            jax.ShapeDtypeStruct((t, KV_WIDTH_A), BF16),
        ),
        grid=(t // tm,),
        in_specs=[
            pl.BlockSpec((tm, d), lambda i: (i, 0)),
            pl.BlockSpec((d, QKV_A_WIDTH), lambda i: (0, 0)),
            pl.BlockSpec((N_HEADS_A + N_KV_A, HEAD_DIM), lambda i: (0, 0)),
            pl.BlockSpec((tm, HEAD_DIM), lambda i: (i % seq_blocks, 0)),
            pl.BlockSpec((tm, HEAD_DIM), lambda i: (i % seq_blocks, 0)),
        ],
        out_specs=(
            pl.BlockSpec((tm, WIDTH_A), lambda i: (i, 0)),
            pl.BlockSpec((tm, KV_WIDTH_A), lambda i: (i, 0)),
            pl.BlockSpec((tm, KV_WIDTH_A), lambda i: (i, 0)),
        ),
        compiler_params=_params(("parallel",), 40),
        name="qkv_a",
    )(u, w, gain, cos, sin)


def _proj_kernel(u_ref, w_ref, scale_ref, z_ref):
    z = jnp.dot(u_ref[...], w_ref[...], preferred_element_type=F32)
    z_ref[...] = (z * scale_ref[...]).astype(BF16)


def _proj(u, w, col_scale, *, tm=1024, tn=512):
    t, d = u.shape
    n = w.shape[1]
    assert t % tm == 0 and n % tn == 0
    return pl.pallas_call(
        _proj_kernel,
        out_shape=jax.ShapeDtypeStruct((t, n), BF16),
        grid=(t // tm, n // tn),
        in_specs=[
            pl.BlockSpec((tm, d), lambda i, j: (i, 0)),
            pl.BlockSpec((d, tn), lambda i, j: (0, j)),
            pl.BlockSpec((1, tn), lambda i, j: (0, j)),
        ],
        out_specs=pl.BlockSpec((tm, tn), lambda i, j: (i, j)),
        compiler_params=_params(("parallel", "arbitrary"), 32),
        name="proj",
    )(u, w, col_scale)


def _softmax_pv(s, v):
    m = jnp.max(s, axis=-1, keepdims=True)
    p = jnp.exp(s - m)
    l = jnp.sum(p, axis=-1, keepdims=True)
    return jnp.dot(p.astype(BF16), v, preferred_element_type=F32) / l


_NT = (((1,), (1,)), ((), ()))


def _gqa_kernel(q_ref, k_ref, v_ref, o_ref):
    k = k_ref[0]
    v = v_ref[0]
    for g in range(GQA_GROUP):
        q = q_ref[0, :, g * HEAD_DIM:(g + 1) * HEAD_DIM]
        s = lax.dot_general(q, k, _NT, preferred_element_type=F32)
        o_ref[0, :, g * HEAD_DIM:(g + 1) * HEAD_DIM] = _softmax_pv(s, v).astype(BF16)


def _gqa(q, k, v, *, tq=512):
    b, s, _ = q.shape
    assert s % tq == 0
    gw = GQA_GROUP * HEAD_DIM
    return pl.pallas_call(
        _gqa_kernel,
        out_shape=jax.ShapeDtypeStruct((b, s, WIDTH_A), BF16),
        grid=(b, N_KV_A, s // tq),
        in_specs=[
            pl.BlockSpec((1, tq, gw), lambda bi, kh, qi: (bi, qi, kh)),
            pl.BlockSpec((1, s, HEAD_DIM), lambda bi, kh, qi: (bi, 0, kh)),
            pl.BlockSpec((1, s, HEAD_DIM), lambda bi, kh, qi: (bi, 0, kh)),
        ],
        out_specs=pl.BlockSpec((1, tq, gw), lambda bi, kh, qi: (bi, qi, kh)),
        compiler_params=_params(("parallel", "parallel", "arbitrary"), 40),
        name="gqa",
    )(q, k, v)


def _nbr_window_start(j, rows):
    return jnp.clip(NBR_Q_ROWS * j - WIN_ROWS // 2, 0, rows - NBR_KEY_ROWS)


def _nbr_kernel(q_ref, k_ref, v_ref, bias_ref, o_ref, *, rows):
    j = pl.program_id(1)
    start = pl.multiple_of(_nbr_window_start(j, rows) * GRID_W, NBR_Q)
    for h in range(N_HEADS_B):
        cols = slice(h * HEAD_DIM, (h + 1) * HEAD_DIM)
        q = q_ref[0, :, cols]
        k = k_ref[0, pl.ds(start, NBR_KEYS), cols]
        v = v_ref[0, pl.ds(start, NBR_KEYS), cols]
        s = lax.dot_general(q, k, _NT, preferred_element_type=F32) + bias_ref[0, h]
        o_ref[0, :, cols] = _softmax_pv(s, v).astype(BF16)


def _nbr_block_type(j, n_blocks):
    return jnp.minimum(j, 2) + j // (n_blocks - 1)


def _nbr(z, bias, q_col, k_col, v_col):
    b, s, _ = z.shape
    rows = s // GRID_W
    n_blocks = rows // NBR_Q_ROWS
    return pl.pallas_call(
        functools.partial(_nbr_kernel, rows=rows),
        out_shape=jax.ShapeDtypeStruct((b, s, WIDTH_B), BF16),
        grid=(b, n_blocks),
        in_specs=[
            pl.BlockSpec((1, NBR_Q, WIDTH_B), lambda bi, j: (bi, j, q_col)),
            pl.BlockSpec((1, s, WIDTH_B), lambda bi, j: (bi, 0, k_col)),
            pl.BlockSpec((1, s, WIDTH_B), lambda bi, j: (bi, 0, v_col)),
            pl.BlockSpec((1, N_HEADS_B, NBR_Q, NBR_KEYS),
                         lambda bi, j: (_nbr_block_type(j, n_blocks), 0, 0, 0)),
        ],
        out_specs=pl.BlockSpec((1, NBR_Q, WIDTH_B), lambda bi, j: (bi, j, 0)),
        compiler_params=_params(("parallel", "arbitrary"), 48),
        name="nbr",
    )(z, z, z, bias)


def _nbr_bias_table(rpb, seq):
    rows = seq // GRID_W
    n_blocks = rows // NBR_Q_ROWS
    assert rows >= NBR_KEY_ROWS and n_blocks >= 4
    tables = []
    for j in (0, 1, 2, n_blocks - 1):
        start = int(np.clip(NBR_Q_ROWS * j - WIN_ROWS // 2, 0, rows - NBR_KEY_ROWS))
        r = (NBR_Q_ROWS * j + np.arange(NBR_Q_ROWS))[:, None, None, None]
        c = np.arange(GRID_W)[None, :, None, None]
        kr = (start + np.arange(NBR_KEY_ROWS))[None, None, :, None]
        kc = np.arange(GRID_W)[None, None, None, :]
        rs = np.clip(r - WIN_ROWS // 2, 0, rows - WIN_ROWS)
        cs = np.clip(c - WIN_COLS // 2, 0, GRID_W - WIN_COLS)
        valid = (kr >= rs) & (kr < rs + WIN_ROWS) & (kc >= cs) & (kc < cs + WIN_COLS)
        dr = np.clip(kr - r + (WIN_ROWS - 1), 0, 2 * WIN_ROWS - 2)
        dc = np.clip(kc - c + (WIN_COLS - 1), 0, 2 * WIN_COLS - 2)
        shape = (NBR_Q_ROWS, GRID_W, NBR_KEY_ROWS, GRID_W)
        dr, dc, valid = (np.broadcast_to(a, shape).reshape(NBR_Q, NBR_KEYS) for a in (dr, dc, valid))
        tables.append(jnp.where(valid[None], rpb[:, dr, dc], MASKED))
    return jnp.stack(tables)


def _merge_kernel(oa_ref, ob_ref, ga_ref, gb_ref, h_ref, wa_ref, wb_ref, wo_ref, out_ref):
    a = jnp.dot(oa_ref[...], wa_ref[...], preferred_element_type=F32)
    b = jnp.dot(ob_ref[...], wb_ref[...], preferred_element_type=F32)
    merged = jax.nn.sigmoid(ga_ref[...].astype(F32)) * a + jax.nn.sigmoid(gb_ref[...].astype(F32)) * b
    out_ref[...] = h_ref[...] + jnp.dot(merged.astype(BF16), wo_ref[...], preferred_element_type=F32)


def _merge(oa, ob, z, ga_col, gb_col, h, wa, wb, wo, *, tm=256):
    t, d = h.shape
    assert t % tm == 0
    return pl.pallas_call(
        _merge_kernel,
        out_shape=jax.ShapeDtypeStruct((t, d), F32),
        grid=(t // tm,),
        in_specs=[
            pl.BlockSpec((tm, WIDTH_A), lambda i: (i, 0)),
            pl.BlockSpec((tm, WIDTH_B), lambda i: (i, 0)),
            pl.BlockSpec((tm, d), lambda i: (i, ga_col)),
            pl.BlockSpec((tm, d), lambda i: (i, gb_col)),
            pl.BlockSpec((tm, d), lambda i: (i, 0)),
            pl.BlockSpec((WIDTH_A, d), lambda i: (0, 0)),
            pl.BlockSpec((WIDTH_B, d), lambda i: (0, 0)),
            pl.BlockSpec((d, d), lambda i: (0, 0)),
        ],
        out_specs=pl.BlockSpec((tm, d), lambda i: (i, 0)),
        compiler_params=_params(("parallel",), 56),
        name="merge",
    )(oa, ob, z, z, h, wa, wb, wo)


def _ple_kernel(h_ref, n_ref, p_ref, wg_ref, wp_ref, gf_ref, y_ref):
    gate = jax.nn.sigmoid(jnp.dot(n_ref[...], wg_ref[...], preferred_element_type=F32))
    emb = jnp.dot(p_ref[...].astype(BF16), wp_ref[...], preferred_element_type=F32)
    y_ref[...] = _rms(h_ref[...] + gate * emb) * gf_ref[...]


def _ple(h, n, p, wg, wp, gf, *, tm=512):
    t, d = h.shape
    assert t % tm == 0
    return pl.pallas_call(
        _ple_kernel,
        out_shape=jax.ShapeDtypeStruct((t, d), F32),
        grid=(t // tm,),
        in_specs=[
            pl.BlockSpec((tm, d), lambda i: (i, 0)),
            pl.BlockSpec((tm, d), lambda i: (i, 0)),
            pl.BlockSpec((tm, D_PLE), lambda i: (i, 0)),
            pl.BlockSpec((d, d), lambda i: (0, 0)),
            pl.BlockSpec((D_PLE, d), lambda i: (0, 0)),
            pl.BlockSpec((1, d), lambda i: (0, 0)),
        ],
        out_specs=pl.BlockSpec((tm, d), lambda i: (i, 0)),
        compiler_params=_params(("parallel",), 48),
        name="ple",
    )(h, n, p, wg, wp, gf)


def _rope_tables(seq):
    n_freq = HEAD_DIM // 4
    inv_freq = ROPE_THETA ** (-jnp.arange(n_freq, dtype=F32) / n_freq)
    t = jnp.arange(seq)
    row = (t // GRID_W).astype(F32)
    col = (t % GRID_W).astype(F32)
    ang = jnp.concatenate([row[:, None] * inv_freq[None], col[:, None] * inv_freq[None]], axis=-1)
    cos, sin = jnp.cos(ang), jnp.sin(ang)
    return jnp.concatenate([cos, cos], axis=-1), jnp.concatenate([-sin, sin], axis=-1)


def _prepare_weights(ffn1_norm, ffn1_w_gate, ffn1_w_up, ffn1_w_down, mix_norm, w_in, q_norm, k_norm,
                     w_branch_a, w_branch_b, w_out, ffn2_norm, ffn2_w_gate, ffn2_w_up, ffn2_w_down,
                     ple_norm, w_ple_gate, w_ple_proj, final_norm, layer):
    i = layer
    half = np.concatenate([np.arange(0, HEAD_DIM, 2), np.arange(1, HEAD_DIM, 2)])
    n_rot = N_HEADS_A + N_KV_A
    rot_cols = (np.arange(n_rot)[:, None] * HEAD_DIM + half[None]).reshape(-1)
    w = w_in[i]
    w_qkv_a = jnp.concatenate([w[:, rot_cols], w[:, n_rot * HEAD_DIM:QKV_A_WIDTH]], axis=1).astype(BF16)
    o = QKV_A_WIDTH
    q_b, k_b, v_b = (w[:, o + n * WIDTH_B:o + (n + 1) * WIDTH_B] for n in range(3))
    o += 3 * WIDTH_B
    g_a, g_b = w[:, o:o + D_MODEL], w[:, o + D_MODEL:]
    w_rest = jnp.concatenate([g_a, g_b, q_b, k_b, v_b], axis=1).astype(BF16)
    rest_scale = jnp.concatenate([jnp.ones((2 * D_MODEL,), F32), jnp.full((WIDTH_B,), ATTN_SCALE, F32),
                                  jnp.ones((2 * WIDTH_B,), F32)])[None]
    gain = jnp.concatenate([jnp.tile((q_norm[i] * ATTN_SCALE)[half][None], (N_HEADS_A, 1)),
                            jnp.tile(k_norm[i][half][None], (N_KV_A, 1))])
    return dict(
        ffn1=(ffn1_norm[i][None], ffn1_w_gate[i].astype(BF16), ffn1_w_up[i].astype(BF16),
              ffn1_w_down[i].astype(BF16), mix_norm[i][None]),
        w_qkv_a=w_qkv_a, gain=gain, w_rest=w_rest, rest_scale=rest_scale,
        merge=(w_branch_a[i].astype(BF16), w_branch_b[i].astype(BF16), w_out[i].astype(BF16)),
        ffn2=(ffn2_norm[i][None], ffn2_w_gate[i].astype(BF16), ffn2_w_up[i].astype(BF16),
              ffn2_w_down[i].astype(BF16), ple_norm[i][None]),
        ple=(w_ple_gate[i].astype(BF16), w_ple_proj[i].astype(BF16), final_norm[None]),
    )


_GA_COL, _GB_COL = 0, 1
_QB_COL, _KB_COL, _VB_COL = 4, 5, 6
assert 2 * D_MODEL == _QB_COL * WIDTH_B


def _encoder(x, p, wts, cos, sin, bias):
    b, s, d = x.shape
    t = b * s
    h1, u = _ffn(x.reshape(t, d), *wts["ffn1"])
    q_a, k_a, v_a = _qkv_a(u, wts["w_qkv_a"], wts["gain"], cos, sin, s)
    z = _proj(u, wts["w_rest"], wts["rest_scale"])
    o_a = _gqa(q_a.reshape(b, s, -1), k_a.reshape(b, s, -1), v_a.reshape(b, s, -1))
    o_b = _nbr(z.reshape(b, s, -1), bias, _QB_COL, _KB_COL, _VB_COL)
    h2 = _merge(o_a.reshape(t, -1), o_b.reshape(t, -1), z, _GA_COL, _GB_COL, h1, *wts["merge"])
    h3, n = _ffn(h2, *wts["ffn2"])
    y = _ple(h3, n, p.reshape(t, -1), *wts["ple"])
    return y.reshape(b, s, d)


def kernel(x_prompt, x_sample, p_prompt, p_sample, ffn1_norm, ffn1_w_gate, ffn1_w_up, ffn1_w_down, mix_norm, w_in, q_norm, k_norm, nat_rpb, w_branch_a, w_branch_b, w_out, ffn2_norm, ffn2_w_gate, ffn2_w_up, ffn2_w_down, ple_norm, w_ple_gate, w_ple_proj, final_norm):
    assert ffn1_norm.shape[0] == 1, "single-layer encoder"
    wts = _prepare_weights(ffn1_norm, ffn1_w_gate, ffn1_w_up, ffn1_w_down, mix_norm, w_in, q_norm, k_norm,
                           w_branch_a, w_branch_b, w_out, ffn2_norm, ffn2_w_gate, ffn2_w_up, ffn2_w_down,
                           ple_norm, w_ple_gate, w_ple_proj, final_norm, 0)
    outs = []
    for x, p in ((x_prompt, p_prompt), (x_sample, p_sample)):
        seq = x.shape[1]
        cos, sin = _rope_tables(seq)
        bias = _nbr_bias_table(nat_rpb[0], seq)
        outs.append(_encoder(x, p[0], wts, cos, sin, bias))
    return tuple(outs)
```

```python
import functools

import numpy as np
import jax
import jax.numpy as jnp
from jax import lax
from jax.experimental import pallas as pl
from jax.experimental.pallas import tpu as pltpu

F32 = jnp.float32
BF16 = jnp.bfloat16

D_MODEL = 2048
HEAD_DIM = 128
N_HEADS_A = 8
N_KV_A = 2
GQA_GROUP = N_HEADS_A // N_KV_A
N_HEADS_B = 8
WIDTH_A = N_HEADS_A * HEAD_DIM
KV_WIDTH_A = N_KV_A * HEAD_DIM
WIDTH_B = N_HEADS_B * HEAD_DIM
QKV_A_WIDTH = WIDTH_A + 2 * KV_WIDTH_A
D_FF = 5632
D_PLE = 256
GRID_W = 64
WIN_ROWS = 8
WIN_COLS = 16
ROPE_THETA = 10000.0
EPS = 1e-6
ATTN_SCALE = HEAD_DIM ** -0.5
MASKED = -1e30

NBR_Q_ROWS = 4
NBR_KEY_ROWS = 12
NBR_Q = NBR_Q_ROWS * GRID_W
NBR_KEYS = NBR_KEY_ROWS * GRID_W
NBR_KEY_PAIRS = NBR_KEY_ROWS // 2
NBR_DR_PAD = (NBR_KEY_ROWS - NBR_Q_ROWS) - (WIN_ROWS - 1) + (NBR_Q_ROWS - 1)
NBR_DR_SLOTS = NBR_DR_PAD + (WIN_ROWS - 1) + (NBR_KEY_ROWS - 2) + 1

V7X_VMEM_BYTES = 64 * 1024 * 1024
MIB = 1024 * 1024


def _params(semantics, vmem_mib):
    assert vmem_mib * MIB < V7X_VMEM_BYTES
    return pltpu.CompilerParams(dimension_semantics=semantics, vmem_limit_bytes=vmem_mib * MIB)


def _resident(shape):
    zeros = (0,) * len(shape)
    return pl.BlockSpec(shape, lambda *_: zeros, pipeline_mode=pl.Buffered(1))


def _rms(x):
    return x * lax.rsqrt(jnp.mean(x * x, axis=-1, keepdims=True) + EPS)


def _ffn_kernel(x_ref, g1_ref, wg_ref, wu_ref, wd_ref, g2_ref, h_ref, n_ref, xn_ref, acc_ref):
    j = pl.program_id(1)

    @pl.when(j == 0)
    def _():
        xn_ref[...] = (_rms(x_ref[...]) * g1_ref[...]).astype(BF16)
        acc_ref[...] = jnp.zeros_like(acc_ref)

    xn = xn_ref[...]
    g = jnp.dot(xn, wg_ref[...], preferred_element_type=F32)
    u = jnp.dot(xn, wu_ref[...], preferred_element_type=F32)
    a = (g * jax.nn.sigmoid(g) * u).astype(BF16)
    acc_ref[...] += jnp.dot(a, wd_ref[...], preferred_element_type=F32)

    @pl.when(j == pl.num_programs(1) - 1)
    def _():
        h = x_ref[...] + 0.5 * acc_ref[...]
        h_ref[...] = h
        n_ref[...] = (_rms(h) * g2_ref[...]).astype(BF16)


def _ffn(x, g1, wg, wu, wd, g2, *, tm=512, tf=512):
    t, d = x.shape
    dff = wg.shape[1]
    assert t % tm == 0 and dff % tf == 0
    return pl.pallas_call(
        _ffn_kernel,
        out_shape=(jax.ShapeDtypeStruct((t, d), F32), jax.ShapeDtypeStruct((t, d), BF16)),
        grid=(t // tm, dff // tf),
        in_specs=[
            pl.BlockSpec((tm, d), lambda i, j: (i, 0)),
            pl.BlockSpec((1, d), lambda i, j: (0, 0)),
            pl.BlockSpec((d, tf), lambda i, j: (0, j)),
            pl.BlockSpec((d, tf), lambda i, j: (0, j)),
            pl.BlockSpec((tf, d), lambda i, j: (j, 0)),
            pl.BlockSpec((1, d), lambda i, j: (0, 0)),
        ],
        out_specs=(
            pl.BlockSpec((tm, d), lambda i, j: (i, 0)),
            pl.BlockSpec((tm, d), lambda i, j: (i, 0)),
        ),
        scratch_shapes=[pltpu.VMEM((tm, d), BF16), pltpu.VMEM((tm, d), F32)],
        compiler_params=_params(("parallel", "arbitrary"), 48),
        name="ffn",
    )(x, g1, wg, wu, wd, g2)


def _qkv_a_kernel(u_ref, w_ref, gain_ref, cos_ref, sin_ref, q_ref, k_ref, v_ref):
    z = jnp.dot(u_ref[...], w_ref[...], preferred_element_type=F32)
    c = cos_ref[...]
    s = sin_ref[...]
    for hh in range(N_HEADS_A + N_KV_A):
        zh = z[:, hh * HEAD_DIM:(hh + 1) * HEAD_DIM]
        zn = _rms(zh) * gain_ref[hh:hh + 1, :]
        out = (zn * c + pltpu.roll(zn, HEAD_DIM // 2, 1) * s).astype(BF16)
        if hh < N_HEADS_A:
            q_ref[:, hh * HEAD_DIM:(hh + 1) * HEAD_DIM] = out
        else:
            kk = hh - N_HEADS_A
            k_ref[:, kk * HEAD_DIM:(kk + 1) * HEAD_DIM] = out
    v_ref[...] = z[:, WIDTH_A + KV_WIDTH_A:].astype(BF16)


def _qkv_a(u, w, gain, cos, sin, seq, *, tm=512):
    t, d = u.shape
    assert t % tm == 0 and seq % tm == 0
    seq_blocks = seq // tm
    return pl.pallas_call(
        _qkv_a_kernel,
        out_shape=(
            jax.ShapeDtypeStruct((t, WIDTH_A), BF16),
            jax.ShapeDtypeStruct((t, KV_WIDTH_A), BF16),
            jax.ShapeDtypeStruct((t, KV_WIDTH_A), BF16),
        ),
        grid=(t // tm,),
        in_specs=[
            pl.BlockSpec((tm, d), lambda i: (i, 0)),
            _resident((d, QKV_A_WIDTH)),
            _resident((N_HEADS_A + N_KV_A, HEAD_DIM)),
            pl.BlockSpec((tm, HEAD_DIM), lambda i: (i % seq_blocks, 0)),
            pl.BlockSpec((tm, HEAD_DIM), lambda i: (i % seq_blocks, 0)),
        ],
        out_specs=(
            pl.BlockSpec((tm, WIDTH_A), lambda i: (i, 0)),
            pl.BlockSpec((tm, KV_WIDTH_A), lambda i: (i, 0)),
            pl.BlockSpec((tm, KV_WIDTH_A), lambda i: (i, 0)),
        ),
        compiler_params=_params(("parallel",), 40),
        name="qkv_a",
    )(u, w, gain, cos, sin)


def _proj_kernel(u_ref, w_ref, scale_ref, z_ref):
    z = jnp.dot(u_ref[...], w_ref[...], preferred_element_type=F32)
    z_ref[...] = (z * scale_ref[...]).astype(BF16)


def _proj(u, w, col_scale, *, tm=1024, tn=512):
    t, d = u.shape
    n = w.shape[1]
    assert t % tm == 0 and n % tn == 0
    return pl.pallas_call(
        _proj_kernel,
        out_shape=jax.ShapeDtypeStruct((t, n), BF16),
        grid=(t // tm, n // tn),
        in_specs=[
            pl.BlockSpec((tm, d), lambda i, j: (i, 0)),
            pl.BlockSpec((d, tn), lambda i, j: (0, j)),
            pl.BlockSpec((1, tn), lambda i, j: (0, j)),
        ],
        out_specs=pl.BlockSpec((tm, tn), lambda i, j: (i, j)),
        compiler_params=_params(("parallel", "arbitrary"), 32),
        name="proj",
    )(u, w, col_scale)


def _softmax_pv(s, v):
    m = jnp.max(s, axis=-1, keepdims=True)
    p = jnp.exp(s - m)
    l = jnp.sum(p, axis=-1, keepdims=True)
    return jnp.dot(p.astype(BF16), v, preferred_element_type=F32) / l


_NT = (((1,), (1,)), ((), ()))


def _gqa_kernel(q_ref, k_ref, v_ref, o_ref):
    k = k_ref[0]
    v = v_ref[0]
    for g in range(GQA_GROUP):
        q = q_ref[0, :, g * HEAD_DIM:(g + 1) * HEAD_DIM]
        s = lax.dot_general(q, k, _NT, preferred_element_type=F32)
        o_ref[0, :, g * HEAD_DIM:(g + 1) * HEAD_DIM] = _softmax_pv(s, v).astype(BF16)


def _gqa(q, k, v, *, tq=512):
    b, s, _ = q.shape
    assert s % tq == 0
    gw = GQA_GROUP * HEAD_DIM
    return pl.pallas_call(
        _gqa_kernel,
        out_shape=jax.ShapeDtypeStruct((b, s, WIDTH_A), BF16),
        grid=(b, N_KV_A, s // tq),
        in_specs=[
            pl.BlockSpec((1, tq, gw), lambda bi, kh, qi: (bi, qi, kh)),
            pl.BlockSpec((1, s, HEAD_DIM), lambda bi, kh, qi: (bi, 0, kh)),
            pl.BlockSpec((1, s, HEAD_DIM), lambda bi, kh, qi: (bi, 0, kh)),
        ],
        out_specs=pl.BlockSpec((1, tq, gw), lambda bi, kh, qi: (bi, qi, kh)),
        compiler_params=_params(("parallel", "parallel", "arbitrary"), 40),
        name="gqa",
    )(q, k, v)


def _nbr_window_start(j, rows):
    return jnp.clip(NBR_Q_ROWS * j - WIN_ROWS // 2, 0, rows - NBR_KEY_ROWS)


def _nbr_kernel(q_ref, k_ref, v_ref, cb_ref, rm_ref, o_ref, *, rows):
    j = pl.program_id(1)
    ws = _nbr_window_start(j, rows)
    start = pl.multiple_of(ws * GRID_W, NBR_Q)
    d0 = ws - NBR_Q_ROWS * j + (WIN_ROWS - 1) + NBR_DR_PAD
    for h in range(N_HEADS_B):
        cols = slice(h * HEAD_DIM, (h + 1) * HEAD_DIM)
        q = q_ref[0, :, cols]
        k = k_ref[0, pl.ds(start, NBR_KEYS), cols]
        v = v_ref[0, pl.ds(start, NBR_KEYS), cols]
        s = lax.dot_general(q, k, _NT, preferred_element_type=F32)
        s = jnp.concatenate([
            jnp.concatenate([
                s[qr * GRID_W:(qr + 1) * GRID_W, a * 2 * GRID_W:(a + 1) * 2 * GRID_W]
                + cb_ref[h, d0 + 2 * a - qr]
                + rm_ref[0, qr * NBR_KEY_PAIRS + a:qr * NBR_KEY_PAIRS + a + 1, :]
                for a in range(NBR_KEY_PAIRS)], axis=1)
            for qr in range(NBR_Q_ROWS)], axis=0)
        o_ref[0, :, cols] = _softmax_pv(s, v).astype(BF16)


def _nbr_block_type(j, n_blocks):
    return jnp.minimum(j, 2) + j // (n_blocks - 1)


def _nbr(z, col_bias, row_mask, q_col, k_col, v_col):
    b, s, _ = z.shape
    rows = s // GRID_W
    n_blocks = rows // NBR_Q_ROWS
    return pl.pallas_call(
        functools.partial(_nbr_kernel, rows=rows),
        out_shape=jax.ShapeDtypeStruct((b, s, WIDTH_B), BF16),
        grid=(b, n_blocks),
        in_specs=[
            pl.BlockSpec((1, NBR_Q, WIDTH_B), lambda bi, j: (bi, j, q_col)),
            pl.BlockSpec((1, s, WIDTH_B), lambda bi, j: (bi, 0, k_col)),
            pl.BlockSpec((1, s, WIDTH_B), lambda bi, j: (bi, 0, v_col)),
            _resident(col_bias.shape),
            pl.BlockSpec((1,) + row_mask.shape[1:], lambda bi, j: (_nbr_block_type(j, n_blocks), 0, 0)),
        ],
        out_specs=pl.BlockSpec((1, NBR_Q, WIDTH_B), lambda bi, j: (bi, j, 0)),
        compiler_params=_params(("parallel", "arbitrary"), 48),
        name="nbr",
    )(z, z, z, col_bias, row_mask)


def _nbr_col_bias(rpb):
    n_dr, n_dc = 2 * WIN_ROWS - 1, 2 * WIN_COLS - 1
    c = np.arange(GRID_W)[:, None]
    kc = np.arange(GRID_W)[None, :]
    cs = np.clip(c - WIN_COLS // 2, 0, GRID_W - WIN_COLS)
    col_valid = (kc >= cs) & (kc < cs + WIN_COLS)
    onehot = ((kc - c + (WIN_COLS - 1))[None] == np.arange(n_dc)[:, None, None]) & col_valid[None]
    onehot = jnp.asarray(onehot.reshape(n_dc, GRID_W * GRID_W), F32)
    tm = jnp.einsum("hrd,dn->hrn", rpb, onehot, precision=lax.Precision.HIGHEST)
    tm = jnp.where(col_valid.reshape(-1), tm, MASKED).reshape(-1, n_dr, GRID_W, GRID_W)
    hi_pad = NBR_DR_SLOTS + 1 - NBR_DR_PAD - n_dr
    tm = jnp.pad(tm, ((0, 0), (NBR_DR_PAD, hi_pad), (0, 0), (0, 0)), constant_values=MASKED)
    return jnp.concatenate([tm[:, :-1], tm[:, 1:]], axis=-1)


def _nbr_row_mask(seq):
    rows = seq // GRID_W
    n_blocks = rows // NBR_Q_ROWS
    assert rows >= NBR_KEY_ROWS and n_blocks >= 4
    out = np.zeros((4, NBR_Q_ROWS, NBR_KEY_ROWS, GRID_W), np.float32)
    for t, j in enumerate((0, 1, 2, n_blocks - 1)):
        start = int(np.clip(NBR_Q_ROWS * j - WIN_ROWS // 2, 0, rows - NBR_KEY_ROWS))
        r = (NBR_Q_ROWS * j + np.arange(NBR_Q_ROWS))[:, None]
        kr = (start + np.arange(NBR_KEY_ROWS))[None, :]
        rs = np.clip(r - WIN_ROWS // 2, 0, rows - WIN_ROWS)
        valid = (kr >= rs) & (kr < rs + WIN_ROWS)
        out[t] = np.where(valid, 0.0, MASKED)[:, :, None]
    return jnp.asarray(out.reshape(4, NBR_Q_ROWS * NBR_KEY_PAIRS, 2 * GRID_W))


def _merge_kernel(oa_ref, ob_ref, ga_ref, gb_ref, h_ref, wa_ref, wb_ref, wo_ref, out_ref):
    a = jnp.dot(oa_ref[...], wa_ref[...], preferred_element_type=F32)
    b = jnp.dot(ob_ref[...], wb_ref[...], preferred_element_type=F32)
    merged = jax.nn.sigmoid(ga_ref[...].astype(F32)) * a + jax.nn.sigmoid(gb_ref[...].astype(F32)) * b
    out_ref[...] = h_ref[...] + jnp.dot(merged.astype(BF16), wo_ref[...], preferred_element_type=F32)


def _merge(oa, ob, z, ga_col, gb_col, h, wa, wb, wo, *, tm=256):
    t, d = h.shape
    assert t % tm == 0
    return pl.pallas_call(
        _merge_kernel,
        out_shape=jax.ShapeDtypeStruct((t, d), F32),
        grid=(t // tm,),
        in_specs=[
            pl.BlockSpec((tm, WIDTH_A), lambda i: (i, 0)),
            pl.BlockSpec((tm, WIDTH_B), lambda i: (i, 0)),
            pl.BlockSpec((tm, d), lambda i: (i, ga_col)),
            pl.BlockSpec((tm, d), lambda i: (i, gb_col)),
            pl.BlockSpec((tm, d), lambda i: (i, 0)),
            _resident((WIDTH_A, d)),
            _resident((WIDTH_B, d)),
            _resident((d, d)),
        ],
        out_specs=pl.BlockSpec((tm, d), lambda i: (i, 0)),
        compiler_params=_params(("parallel",), 56),
        name="merge",
    )(oa, ob, z, z, h, wa, wb, wo)


def _ple_kernel(h_ref, n_ref, p_ref, wg_ref, wp_ref, gf_ref, y_ref):
    gate = jax.nn.sigmoid(jnp.dot(n_ref[...], wg_ref[...], preferred_element_type=F32))
    emb = jnp.dot(p_ref[...].astype(BF16), wp_ref[...], preferred_element_type=F32)
    y_ref[...] = _rms(h_ref[...] + gate * emb) * gf_ref[...]


def _ple(h, n, p, wg, wp, gf, *, tm=512):
    t, d = h.shape
    assert t % tm == 0
    return pl.pallas_call(
        _ple_kernel,
        out_shape=jax.ShapeDtypeStruct((t, d), F32),
        grid=(t // tm,),
        in_specs=[
            pl.BlockSpec((tm, d), lambda i: (i, 0)),
            pl.BlockSpec((tm, d), lambda i: (i, 0)),
            pl.BlockSpec((tm, D_PLE), lambda i: (i, 0)),
            _resident((d, d)),
            _resident((D_PLE, d)),
            _resident((1, d)),
        ],
        out_specs=pl.BlockSpec((tm, d), lambda i: (i, 0)),
        compiler_params=_params(("parallel",), 48),
        name="ple",
    )(h, n, p, wg, wp, gf)


def _rope_tables(seq):
    n_freq = HEAD_DIM // 4
    inv_freq = ROPE_THETA ** (-jnp.arange(n_freq, dtype=F32) / n_freq)
    t = jnp.arange(seq)
    row = (t // GRID_W).astype(F32)
    col = (t % GRID_W).astype(F32)
    ang = jnp.concatenate([row[:, None] * inv_freq[None], col[:, None] * inv_freq[None]], axis=-1)
    cos, sin = jnp.cos(ang), jnp.sin(ang)
    return jnp.concatenate([cos, cos], axis=-1), jnp.concatenate([-sin, sin], axis=-1)


def _prepare_weights(ffn1_norm, ffn1_w_gate, ffn1_w_up, ffn1_w_down, mix_norm, w_in, q_norm, k_norm,
                     w_branch_a, w_branch_b, w_out, ffn2_norm, ffn2_w_gate, ffn2_w_up, ffn2_w_down,
                     ple_norm, w_ple_gate, w_ple_proj, final_norm, layer):
    i = layer
    half = np.concatenate([np.arange(0, HEAD_DIM, 2), np.arange(1, HEAD_DIM, 2)])
    n_rot = N_HEADS_A + N_KV_A
    rot_cols = (np.arange(n_rot)[:, None] * HEAD_DIM + half[None]).reshape(-1)
    w = w_in[i]
    w_qkv_a = jnp.concatenate([w[:, rot_cols], w[:, n_rot * HEAD_DIM:QKV_A_WIDTH]], axis=1).astype(BF16)
    o = QKV_A_WIDTH
    q_b, k_b, v_b = (w[:, o + n * WIDTH_B:o + (n + 1) * WIDTH_B] for n in range(3))
    o += 3 * WIDTH_B
    g_a, g_b = w[:, o:o + D_MODEL], w[:, o + D_MODEL:]
    w_rest = jnp.concatenate([g_a, g_b, q_b, k_b, v_b], axis=1).astype(BF16)
    rest_scale = jnp.concatenate([jnp.ones((2 * D_MODEL,), F32), jnp.full((WIDTH_B,), ATTN_SCALE, F32),
                                  jnp.ones((2 * WIDTH_B,), F32)])[None]
    gain = jnp.concatenate([jnp.tile((q_norm[i] * ATTN_SCALE)[half][None], (N_HEADS_A, 1)),
                            jnp.tile(k_norm[i][half][None], (N_KV_A, 1))])
    return dict(
        ffn1=(ffn1_norm[i][None], ffn1_w_gate[i].astype(BF16), ffn1_w_up[i].astype(BF16),
              ffn1_w_down[i].astype(BF16), mix_norm[i][None]),
        w_qkv_a=w_qkv_a, gain=gain, w_rest=w_rest, rest_scale=rest_scale,
        merge=(w_branch_a[i].astype(BF16), w_branch_b[i].astype(BF16), w_out[i].astype(BF16)),
        ffn2=(ffn2_norm[i][None], ffn2_w_gate[i].astype(BF16), ffn2_w_up[i].astype(BF16),
              ffn2_w_down[i].astype(BF16), ple_norm[i][None]),
        ple=(w_ple_gate[i].astype(BF16), w_ple_proj[i].astype(BF16), final_norm[None]),
    )


_GA_COL, _GB_COL = 0, 1
_QB_COL, _KB_COL, _VB_COL = 4, 5, 6
assert 2 * D_MODEL == _QB_COL * WIDTH_B


def _encoder(x, p, wts, cos, sin, col_bias, row_mask):
    b, s, d = x.shape
    t = b * s
    h1, u = _ffn(x.reshape(t, d), *wts["ffn1"])
    q_a, k_a, v_a = _qkv_a(u, wts["w_qkv_a"], wts["gain"], cos, sin, s)
    z = _proj(u, wts["w_rest"], wts["rest_scale"])
    o_a = _gqa(q_a.reshape(b, s, -1), k_a.reshape(b, s, -1), v_a.reshape(b, s, -1))
    o_b = _nbr(z.reshape(b, s, -1), col_bias, row_mask, _QB_COL, _KB_COL, _VB_COL)
    h2 = _merge(o_a.reshape(t, -1), o_b.reshape(t, -1), z, _GA_COL, _GB_COL, h1, *wts["merge"])
    h3, n = _ffn(h2, *wts["ffn2"])
    y = _ple(h3, n, p.reshape(t, -1), *wts["ple"])
    return y.reshape(b, s, d)


def kernel(x_prompt, x_sample, p_prompt, p_sample, ffn1_norm, ffn1_w_gate, ffn1_w_up, ffn1_w_down, mix_norm, w_in, q_norm, k_norm, nat_rpb, w_branch_a, w_branch_b, w_out, ffn2_norm, ffn2_w_gate, ffn2_w_up, ffn2_w_down, ple_norm, w_ple_gate, w_ple_proj, final_norm):
    assert ffn1_norm.shape[0] == 1, "single-layer encoder"
    wts = _prepare_weights(ffn1_norm, ffn1_w_gate, ffn1_w_up, ffn1_w_down, mix_norm, w_in, q_norm, k_norm,
                           w_branch_a, w_branch_b, w_out, ffn2_norm, ffn2_w_gate, ffn2_w_up, ffn2_w_down,
                           ple_norm, w_ple_gate, w_ple_proj, final_norm, 0)
    col_bias = _nbr_col_bias(nat_rpb[0])
    tables = {}
    outs = []
    for x, p in ((x_prompt, p_prompt), (x_sample, p_sample)):
        seq = x.shape[1]
        if seq not in tables:
            tables[seq] = _rope_tables(seq) + (_nbr_row_mask(seq),)
        cos, sin, row_mask = tables[seq]
        outs.append(_encoder(x, p[0], wts, cos, sin, col_bias, row_mask))
    return tuple(outs)
```

```python
import functools
import math

import numpy as np
import jax
import jax.numpy as jnp
from jax import lax
from jax.experimental import pallas as pl
from jax.experimental.pallas import tpu as pltpu

F32 = jnp.float32
BF16 = jnp.bfloat16

D_MODEL = 2048
HEAD_DIM = 128
N_HEADS_A = 8
N_KV_A = 2
GQA_GROUP = N_HEADS_A // N_KV_A
N_HEADS_B = 8
WIDTH_A = N_HEADS_A * HEAD_DIM
KV_WIDTH_A = N_KV_A * HEAD_DIM
WIDTH_B = N_HEADS_B * HEAD_DIM
ROT_WIDTH = WIDTH_A + KV_WIDTH_A
QKV_A_WIDTH = ROT_WIDTH + KV_WIDTH_A
D_FF = 5632
D_PLE = 256
GRID_W = 64
WIN_ROWS = 8
WIN_COLS = 16
ROPE_THETA = 10000.0
EPS = 1e-6
ATTN_SCALE = HEAD_DIM ** -0.5
LOG2_E = math.log2(math.e)
MASKED = -1e30

NBR_Q_ROWS = 4
NBR_KEY_ROWS = 12
NBR_Q = NBR_Q_ROWS * GRID_W
NBR_KEYS = NBR_KEY_ROWS * GRID_W
NBR_KEY_PAIRS = NBR_KEY_ROWS // 2
NBR_DR_PAD = (NBR_KEY_ROWS - NBR_Q_ROWS) - (WIN_ROWS - 1) + (NBR_Q_ROWS - 1)
NBR_DR_SLOTS = NBR_DR_PAD + (WIN_ROWS - 1) + (NBR_KEY_ROWS - 2) + 1

PROJ_TN = 512

V7X_VMEM_BYTES = 64 * 1024 * 1024
MIB = 1024 * 1024


def _params(semantics, vmem_mib):
    assert vmem_mib * MIB < V7X_VMEM_BYTES
    return pltpu.CompilerParams(dimension_semantics=semantics, vmem_limit_bytes=vmem_mib * MIB)


def _resident(shape):
    zeros = (0,) * len(shape)
    return pl.BlockSpec(shape, lambda *_: zeros, pipeline_mode=pl.Buffered(1))


def _rms(x):
    return x * lax.rsqrt(jnp.mean(x * x, axis=-1, keepdims=True) + EPS)


ROW_CHUNK = 16
ROW_UNROLL = 8

def _ffn_kernel(x_hbm, g1_ref, wg_ref, wu_ref, wd_ref, g2_ref, h_ref, n_ref, x_buf, xn_ref, x_sem):
    i = pl.program_id(0)
    j = pl.program_id(1)
    tm = x_buf.shape[0]

    def x_copy(tile):
        return pltpu.make_async_copy(x_hbm.at[pl.ds(tile * tm, tm), :], x_buf, x_sem)

    @pl.when(j == 0)
    def _():
        @pl.when(i == 0)
        def _():
            x_copy(0).start()

        x_copy(i).wait()

        def first(r, carry):
            rows = pl.ds(pl.multiple_of(r * ROW_CHUNK, ROW_CHUNK), ROW_CHUNK)
            x = x_buf[rows, :]
            xn_ref[rows, :] = (_rms(x) * g1_ref[...]).astype(BF16)
            h_ref[rows, :] = 2.0 * x
            return carry

        lax.fori_loop(0, tm // ROW_CHUNK, first, 0, unroll=ROW_UNROLL)

        @pl.when(i + 1 < pl.num_programs(0))
        def _():
            x_copy(i + 1).start()

    xn = xn_ref[...]
    g = jnp.dot(xn, wg_ref[...], preferred_element_type=F32)
    u = jnp.dot(xn, wu_ref[...], preferred_element_type=F32)
    a = (g * jax.nn.sigmoid(g) * u).astype(BF16)
    h_ref[...] += jnp.dot(a, wd_ref[...], preferred_element_type=F32)

    @pl.when(j == pl.num_programs(1) - 1)
    def _():
        def last(r, carry):
            rows = pl.ds(pl.multiple_of(r * ROW_CHUNK, ROW_CHUNK), ROW_CHUNK)
            h = 0.5 * h_ref[rows, :]
            h_ref[rows, :] = h
            n_ref[rows, :] = (_rms(h) * g2_ref[...]).astype(BF16)
            return carry

        lax.fori_loop(0, tm // ROW_CHUNK, last, 0, unroll=ROW_UNROLL)


def _ffn(x, g1, wg, wu, wd, g2, *, tm=1024, tf=512):
    t, d = x.shape
    dff = wg.shape[1]
    assert t % tm == 0 and dff % tf == 0
    return pl.pallas_call(
        _ffn_kernel,
        out_shape=(jax.ShapeDtypeStruct((t, d), F32), jax.ShapeDtypeStruct((t, d), BF16)),
        grid=(t // tm, dff // tf),
        in_specs=[
            pl.BlockSpec(memory_space=pl.ANY),
            pl.BlockSpec((1, d), lambda i, j: (0, 0)),
            pl.BlockSpec((d, tf), lambda i, j: (0, j)),
            pl.BlockSpec((d, tf), lambda i, j: (0, j)),
            pl.BlockSpec((tf, d), lambda i, j: (j, 0)),
            pl.BlockSpec((1, d), lambda i, j: (0, 0)),
        ],
        out_specs=(
            pl.BlockSpec((tm, d), lambda i, j: (i, 0)),
            pl.BlockSpec((tm, d), lambda i, j: (i, 0)),
        ),
        scratch_shapes=[pltpu.VMEM((tm, d), F32), pltpu.VMEM((tm, d), BF16), pltpu.SemaphoreType.DMA(())],
        compiler_params=_params(("arbitrary", "arbitrary"), 60),
        name="ffn",
    )(x, g1, wg, wu, wd, g2)


def _qkv_a_kernel(u_ref, wr_ref, wv_ref, gain_ref, cos_ref, sin_ref, q_ref, k_ref, v_ref):
    u = u_ref[...]
    c = cos_ref[...]
    s = sin_ref[...]
    z = jnp.dot(u, wr_ref[...], preferred_element_type=F32)
    for hh in range(N_HEADS_A + N_KV_A):
        zh = z[:, hh * HEAD_DIM:(hh + 1) * HEAD_DIM]
        zn = _rms(zh) * gain_ref[hh:hh + 1, :]
        out = (zn * c + pltpu.roll(zn, HEAD_DIM // 2, 1) * s).astype(BF16)
        if hh < N_HEADS_A:
            q_ref[:, hh * HEAD_DIM:(hh + 1) * HEAD_DIM] = out
        else:
            kk = hh - N_HEADS_A
            k_ref[:, kk * HEAD_DIM:(kk + 1) * HEAD_DIM] = out
    v_ref[...] = jnp.dot(u, wv_ref[...], preferred_element_type=F32).astype(BF16)


def _qkv_a(u, w_rot, w_in, gain, cos, sin, seq, *, tm=512):
    t, d = u.shape
    assert t % tm == 0 and seq % tm == 0 and ROT_WIDTH % KV_WIDTH_A == 0
    seq_blocks = seq // tm
    return pl.pallas_call(
        _qkv_a_kernel,
        out_shape=(
            jax.ShapeDtypeStruct((t, WIDTH_A), BF16),
            jax.ShapeDtypeStruct((t, KV_WIDTH_A), BF16),
            jax.ShapeDtypeStruct((t, KV_WIDTH_A), BF16),
        ),
        grid=(t // tm,),
        in_specs=[
            pl.BlockSpec((tm, d), lambda i: (i, 0)),
            _resident((d, ROT_WIDTH)),
            pl.BlockSpec((d, KV_WIDTH_A), lambda i: (0, ROT_WIDTH // KV_WIDTH_A), pipeline_mode=pl.Buffered(1)),
            _resident((N_HEADS_A + N_KV_A, HEAD_DIM)),
            pl.BlockSpec((tm, HEAD_DIM), lambda i: (i % seq_blocks, 0)),
            pl.BlockSpec((tm, HEAD_DIM), lambda i: (i % seq_blocks, 0)),
        ],
        out_specs=(
            pl.BlockSpec((tm, WIDTH_A), lambda i: (i, 0)),
            pl.BlockSpec((tm, KV_WIDTH_A), lambda i: (i, 0)),
            pl.BlockSpec((tm, KV_WIDTH_A), lambda i: (i, 0)),
        ),
        compiler_params=_params(("parallel",), 40),
        name="qkv_a",
    )(u, w_rot, w_in, gain, cos, sin)


def _proj_kernel(u_ref, w_ref, scale_ref, z_ref):
    z = jnp.dot(u_ref[...], w_ref[...], preferred_element_type=F32)
    z_ref[...] = (z * scale_ref[...]).astype(BF16)


def _proj(u, w, src_cols, col_scale, *, tm=2048, tn=PROJ_TN):
    t, d = u.shape
    n = len(src_cols) * tn
    assert t % tm == 0 and col_scale.shape == (1, n)

    def w_block(i, j):
        src = src_cols[0]
        for k in range(1, len(src_cols)):
            src = jnp.where(j >= k, src_cols[k], src)
        return (0, src)

    return pl.pallas_call(
        _proj_kernel,
        out_shape=jax.ShapeDtypeStruct((t, n), BF16),
        grid=(t // tm, len(src_cols)),
        in_specs=[
            pl.BlockSpec((tm, d), lambda i, j: (i, 0)),
            pl.BlockSpec((d, tn), w_block),
            pl.BlockSpec((1, tn), lambda i, j: (0, j)),
        ],
        out_specs=pl.BlockSpec((tm, tn), lambda i, j: (i, j)),
        compiler_params=_params(("parallel", "arbitrary"), 40),
        name="proj",
    )(u, w, col_scale)


def _softmax_pv(s, v):
    m = jnp.max(s, axis=-1, keepdims=True)
    p = jnp.exp2(s - m)
    l = jnp.sum(p, axis=-1, keepdims=True)
    return jnp.dot(p.astype(BF16), v, preferred_element_type=F32) / l


_NT = (((1,), (1,)), ((), ()))


def _gqa_kernel(q_ref, k_ref, v_ref, o_ref):
    k = k_ref[0]
    v = v_ref[0]
    for g in range(GQA_GROUP):
        q = q_ref[0, :, g * HEAD_DIM:(g + 1) * HEAD_DIM]
        s = lax.dot_general(q, k, _NT, preferred_element_type=F32)
        o_ref[0, :, g * HEAD_DIM:(g + 1) * HEAD_DIM] = _softmax_pv(s, v).astype(BF16)


def _gqa(q, k, v, *, tq=512):
    b, s, _ = q.shape
    assert s % tq == 0
    gw = GQA_GROUP * HEAD_DIM
    return pl.pallas_call(
        _gqa_kernel,
        out_shape=jax.ShapeDtypeStruct((b, s, WIDTH_A), BF16),
        grid=(b, N_KV_A, s // tq),
        in_specs=[
            pl.BlockSpec((1, tq, gw), lambda bi, kh, qi: (bi, qi, kh)),
            pl.BlockSpec((1, s, HEAD_DIM), lambda bi, kh, qi: (bi, 0, kh)),
            pl.BlockSpec((1, s, HEAD_DIM), lambda bi, kh, qi: (bi, 0, kh)),
        ],
        out_specs=pl.BlockSpec((1, tq, gw), lambda bi, kh, qi: (bi, qi, kh)),
        compiler_params=_params(("parallel", "parallel", "arbitrary"), 40),
        name="gqa",
    )(q, k, v)


def _nbr_window_start(j, rows):
    return jnp.clip(NBR_Q_ROWS * j - WIN_ROWS // 2, 0, rows - NBR_KEY_ROWS)


def _nbr_kernel(q_ref, k_ref, v_ref, cb_ref, rm_ref, o_ref, *, rows):
    j = pl.program_id(1)
    ws = _nbr_window_start(j, rows)
    start = pl.multiple_of(ws * GRID_W, NBR_Q)
    d0 = ws - NBR_Q_ROWS * j + (WIN_ROWS - 1) + NBR_DR_PAD
    for h in range(N_HEADS_B):
        cols = slice(h * HEAD_DIM, (h + 1) * HEAD_DIM)
        q = q_ref[0, :, cols]
        k = k_ref[0, pl.ds(start, NBR_KEYS), cols]
        v = v_ref[0, pl.ds(start, NBR_KEYS), cols]
        s = lax.dot_general(q, k, _NT, preferred_element_type=F32)
        s = jnp.concatenate([
            jnp.concatenate([
                s[qr * GRID_W:(qr + 1) * GRID_W, a * 2 * GRID_W:(a + 1) * 2 * GRID_W]
                + cb_ref[h, d0 + 2 * a - qr]
                + rm_ref[0, qr * NBR_KEY_PAIRS + a:qr * NBR_KEY_PAIRS + a + 1, :]
                for a in range(NBR_KEY_PAIRS)], axis=1)
            for qr in range(NBR_Q_ROWS)], axis=0)
        o_ref[0, :, cols] = _softmax_pv(s, v).astype(BF16)


def _nbr_block_type(j, n_blocks):
    return jnp.minimum(j, 2) + j // (n_blocks - 1)


def _nbr(z, col_bias, row_mask, q_col, k_col, v_col):
    b, s, _ = z.shape
    rows = s // GRID_W
    n_blocks = rows // NBR_Q_ROWS
    return pl.pallas_call(
        functools.partial(_nbr_kernel, rows=rows),
        out_shape=jax.ShapeDtypeStruct((b, s, WIDTH_B), BF16),
        grid=(b, n_blocks),
        in_specs=[
            pl.BlockSpec((1, NBR_Q, WIDTH_B), lambda bi, j: (bi, j, q_col)),
            pl.BlockSpec((1, s, WIDTH_B), lambda bi, j: (bi, 0, k_col)),
            pl.BlockSpec((1, s, WIDTH_B), lambda bi, j: (bi, 0, v_col)),
            _resident(col_bias.shape),
            pl.BlockSpec((1,) + row_mask.shape[1:], lambda bi, j: (_nbr_block_type(j, n_blocks), 0, 0)),
        ],
        out_specs=pl.BlockSpec((1, NBR_Q, WIDTH_B), lambda bi, j: (bi, j, 0)),
        compiler_params=_params(("parallel", "arbitrary"), 48),
        name="nbr",
    )(z, z, z, col_bias, row_mask)


def _nbr_col_bias(rpb):
    n_dr, n_dc = 2 * WIN_ROWS - 1, 2 * WIN_COLS - 1
    c = np.arange(GRID_W)[:, None]
    kc = np.arange(GRID_W)[None, :]
    cs = np.clip(c - WIN_COLS // 2, 0, GRID_W - WIN_COLS)
    col_valid = (kc >= cs) & (kc < cs + WIN_COLS)
    onehot = ((kc - c + (WIN_COLS - 1))[None] == np.arange(n_dc)[:, None, None]) & col_valid[None]
    onehot = jnp.asarray(onehot.reshape(n_dc, GRID_W * GRID_W), F32)
    tm = jnp.einsum("hrd,dn->hrn", rpb * LOG2_E, onehot, precision=lax.Precision.HIGHEST)
    tm = jnp.where(col_valid.reshape(-1), tm, MASKED).reshape(-1, n_dr, GRID_W, GRID_W)
    hi_pad = NBR_DR_SLOTS + 1 - NBR_DR_PAD - n_dr
    tm = jnp.pad(tm, ((0, 0), (NBR_DR_PAD, hi_pad), (0, 0), (0, 0)), constant_values=MASKED)
    return jnp.concatenate([tm[:, :-1], tm[:, 1:]], axis=-1)


def _nbr_row_mask(seq):
    rows = seq // GRID_W
    n_blocks = rows // NBR_Q_ROWS
    assert rows >= NBR_KEY_ROWS and n_blocks >= 4
    out = np.zeros((4, NBR_Q_ROWS, NBR_KEY_ROWS, GRID_W), np.float32)
    for t, j in enumerate((0, 1, 2, n_blocks - 1)):
        start = int(np.clip(NBR_Q_ROWS * j - WIN_ROWS // 2, 0, rows - NBR_KEY_ROWS))
        r = (NBR_Q_ROWS * j + np.arange(NBR_Q_ROWS))[:, None]
        kr = (start + np.arange(NBR_KEY_ROWS))[None, :]
        rs = np.clip(r - WIN_ROWS // 2, 0, rows - WIN_ROWS)
        valid = (kr >= rs) & (kr < rs + WIN_ROWS)
        out[t] = np.where(valid, 0.0, MASKED)[:, :, None]
    return jnp.asarray(out.reshape(4, NBR_Q_ROWS * NBR_KEY_PAIRS, 2 * GRID_W))


MERGE_TN = 512


def _merge_kernel(oa_ref, ob_ref, ga_ref, gb_ref, h_ref, wa_ref, wb_ref, wo_ref, out_ref, m_ref):
    oa = oa_ref[...]
    ob = ob_ref[...]
    for c in range(D_MODEL // MERGE_TN):
        cols = slice(c * MERGE_TN, (c + 1) * MERGE_TN)
        a = jnp.dot(oa, wa_ref[:, cols], preferred_element_type=F32)
        b = jnp.dot(ob, wb_ref[:, cols], preferred_element_type=F32)
        merged = (jax.nn.sigmoid(ga_ref[:, cols].astype(F32)) * a
                  + jax.nn.sigmoid(gb_ref[:, cols].astype(F32)) * b)
        m_ref[:, cols] = merged.astype(BF16)
    out_ref[...] = h_ref[...] + jnp.dot(m_ref[...], wo_ref[...], preferred_element_type=F32)


def _merge(oa, ob, z, ga_col, gb_col, h, wa, wb, wo, *, tm=512):
    t, d = h.shape
    assert t % tm == 0 and d == D_MODEL
    return pl.pallas_call(
        _merge_kernel,
        out_shape=jax.ShapeDtypeStruct((t, d), F32),
        grid=(t // tm,),
        in_specs=[
            pl.BlockSpec((tm, WIDTH_A), lambda i: (i, 0)),
            pl.BlockSpec((tm, WIDTH_B), lambda i: (i, 0)),
            pl.BlockSpec((tm, d), lambda i: (i, ga_col)),
            pl.BlockSpec((tm, d), lambda i: (i, gb_col)),
            pl.BlockSpec((tm, d), lambda i: (i, 0)),
            _resident((WIDTH_A, d)),
            _resident((WIDTH_B, d)),
            _resident((d, d)),
        ],
        out_specs=pl.BlockSpec((tm, d), lambda i: (i, 0)),
        scratch_shapes=[pltpu.VMEM((tm, d), BF16)],
        compiler_params=_params(("parallel",), 56),
        name="merge",
    )(oa, ob, z, z, h, wa, wb, wo)


def _ple_kernel(h_ref, n_ref, p_ref, wg_ref, wp_ref, gf_ref, y_ref):
    gate = jax.nn.sigmoid(jnp.dot(n_ref[...], wg_ref[...], preferred_element_type=F32))
    emb = jnp.dot(p_ref[...].astype(BF16), wp_ref[...], preferred_element_type=F32)
    y_ref[...] = _rms(h_ref[...] + gate * emb) * gf_ref[...]


def _ple(h, n, p, wg, wp, gf, *, tm=512):
    t, d = h.shape
    assert t % tm == 0
    return pl.pallas_call(
        _ple_kernel,
        out_shape=jax.ShapeDtypeStruct((t, d), F32),
        grid=(t // tm,),
        in_specs=[
            pl.BlockSpec((tm, d), lambda i: (i, 0)),
            pl.BlockSpec((tm, d), lambda i: (i, 0)),
            pl.BlockSpec((tm, D_PLE), lambda i: (i, 0)),
            _resident((d, d)),
            _resident((D_PLE, d)),
            _resident((1, d)),
        ],
        out_specs=pl.BlockSpec((tm, d), lambda i: (i, 0)),
        compiler_params=_params(("parallel",), 48),
        name="ple",
    )(h, n, p, wg, wp, gf)


def _rope_tables(seq):
    n_freq = HEAD_DIM // 4
    inv_freq = ROPE_THETA ** (-jnp.arange(n_freq, dtype=F32) / n_freq)
    t = jnp.arange(seq)
    row = (t // GRID_W).astype(F32)
    col = (t % GRID_W).astype(F32)
    ang = jnp.concatenate([row[:, None] * inv_freq[None], col[:, None] * inv_freq[None]], axis=-1)
    cos, sin = jnp.cos(ang), jnp.sin(ang)
    return jnp.concatenate([cos, cos], axis=-1), jnp.concatenate([-sin, sin], axis=-1)


_W_IN_SEGMENTS = dict(q_b=QKV_A_WIDTH, k_b=QKV_A_WIDTH + WIDTH_B, v_b=QKV_A_WIDTH + 2 * WIDTH_B,
                      g_a=QKV_A_WIDTH + 3 * WIDTH_B, g_b=QKV_A_WIDTH + 3 * WIDTH_B + D_MODEL)
_PROJ_LAYOUT = (("g_a", D_MODEL), ("g_b", D_MODEL), ("q_b", WIDTH_B), ("k_b", WIDTH_B), ("v_b", WIDTH_B))
_PROJ_SRC_COLS = tuple((_W_IN_SEGMENTS[name] + off) // PROJ_TN
                       for name, width in _PROJ_LAYOUT for off in range(0, width, PROJ_TN))
assert all(start % PROJ_TN == 0 for start in _W_IN_SEGMENTS.values())
_GA_COL, _GB_COL = 0, 1
_QB_COL, _KB_COL, _VB_COL = 4, 5, 6
assert 2 * D_MODEL == _QB_COL * WIDTH_B


def _prepare_weights(ffn1_norm, ffn1_w_gate, ffn1_w_up, ffn1_w_down, mix_norm, w_in, q_norm, k_norm,
                     w_branch_a, w_branch_b, w_out, ffn2_norm, ffn2_w_gate, ffn2_w_up, ffn2_w_down,
                     ple_norm, w_ple_gate, w_ple_proj, final_norm, layer):
    i = layer
    half = np.concatenate([np.arange(0, HEAD_DIM, 2), np.arange(1, HEAD_DIM, 2)])
    n_rot = N_HEADS_A + N_KV_A
    w = w_in[i]
    w_rot = w[:, :ROT_WIDTH].reshape(D_MODEL, n_rot, HEAD_DIM // 2, 2)
    w_rot = jnp.swapaxes(w_rot, 2, 3).reshape(D_MODEL, ROT_WIDTH).astype(BF16)
    q_scale = ATTN_SCALE * LOG2_E
    proj_scale = jnp.concatenate([jnp.full((width,), q_scale if name == "q_b" else 1.0, F32)
                                  for name, width in _PROJ_LAYOUT])[None]
    gain = jnp.concatenate([jnp.tile((q_norm[i] * q_scale)[half][None], (N_HEADS_A, 1)),
                            jnp.tile(k_norm[i][half][None], (N_KV_A, 1))])
    return dict(
        ffn1=(ffn1_norm[i][None], ffn1_w_gate[i].astype(BF16), ffn1_w_up[i].astype(BF16),
              ffn1_w_down[i].astype(BF16), mix_norm[i][None]),
        w_rot=w_rot, gain=gain, w_in=w.astype(BF16), proj_scale=proj_scale,
        merge=(w_branch_a[i].astype(BF16), w_branch_b[i].astype(BF16), w_out[i].astype(BF16)),
        ffn2=(ffn2_norm[i][None], ffn2_w_gate[i].astype(BF16), ffn2_w_up[i].astype(BF16),
              ffn2_w_down[i].astype(BF16), ple_norm[i][None]),
        ple=(w_ple_gate[i].astype(BF16), w_ple_proj[i].astype(BF16), final_norm[None]),
    )


def _encoder(x, p, wts, cos, sin, col_bias, row_mask):
    b, s, d = x.shape
    t = b * s
    h1, u = _ffn(x.reshape(t, d), *wts["ffn1"])
    q_a, k_a, v_a = _qkv_a(u, wts["w_rot"], wts["w_in"], wts["gain"], cos, sin, s)
    z = _proj(u, wts["w_in"], _PROJ_SRC_COLS, wts["proj_scale"])
    o_a = _gqa(q_a.reshape(b, s, -1), k_a.reshape(b, s, -1), v_a.reshape(b, s, -1))
    o_b = _nbr(z.reshape(b, s, -1), col_bias, row_mask, _QB_COL, _KB_COL, _VB_COL)
    h2 = _merge(o_a.reshape(t, -1), o_b.reshape(t, -1), z, _GA_COL, _GB_COL, h1, *wts["merge"])
    h3, n = _ffn(h2, *wts["ffn2"])
    y = _ple(h3, n, p.reshape(t, -1), *wts["ple"])
    return y.reshape(b, s, d)


def kernel(x_prompt, x_sample, p_prompt, p_sample, ffn1_norm, ffn1_w_gate, ffn1_w_up, ffn1_w_down, mix_norm, w_in, q_norm, k_norm, nat_rpb, w_branch_a, w_branch_b, w_out, ffn2_norm, ffn2_w_gate, ffn2_w_up, ffn2_w_down, ple_norm, w_ple_gate, w_ple_proj, final_norm):
    assert ffn1_norm.shape[0] == 1, "single-layer encoder"
    wts = _prepare_weights(ffn1_norm, ffn1_w_gate, ffn1_w_up, ffn1_w_down, mix_norm, w_in, q_norm, k_norm,
                           w_branch_a, w_branch_b, w_out, ffn2_norm, ffn2_w_gate, ffn2_w_up, ffn2_w_down,
                           ple_norm, w_ple_gate, w_ple_proj, final_norm, 0)
    col_bias = _nbr_col_bias(nat_rpb[0])
    tables = {}
    outs = []
    for x, p in ((x_prompt, p_prompt), (x_sample, p_sample)):
        seq = x.shape[1]
        if seq not in tables:
            tables[seq] = _rope_tables(seq) + (_nbr_row_mask(seq),)
        cos, sin, row_mask = tables[seq]
        outs.append(_encoder(x, p[0], wts, cos, sin, col_bias, row_mask))
    return tuple(outs)
```

```python
import functools
import math

import numpy as np
import jax
import jax.numpy as jnp
from jax import lax
from jax.experimental import pallas as pl
from jax.experimental.pallas import tpu as pltpu

F32 = jnp.float32
BF16 = jnp.bfloat16

D_MODEL = 2048
HEAD_DIM = 128
N_HEADS_A = 8
N_KV_A = 2
GQA_GROUP = N_HEADS_A // N_KV_A
N_HEADS_B = 8
WIDTH_A = N_HEADS_A * HEAD_DIM
KV_WIDTH_A = N_KV_A * HEAD_DIM
WIDTH_B = N_HEADS_B * HEAD_DIM
ROT_WIDTH = WIDTH_A + KV_WIDTH_A
QKV_A_WIDTH = ROT_WIDTH + KV_WIDTH_A
D_FF = 5632
D_PLE = 256
GRID_W = 64
WIN_ROWS = 8
WIN_COLS = 16
ROPE_THETA = 10000.0
EPS = 1e-6
ATTN_SCALE = HEAD_DIM ** -0.5
LOG2_E = math.log2(math.e)
MASKED = -1e30
GQA_SAFE_LOG2_SCORE = 48.0
BF16_NORM_MARGIN = 1.02

NBR_Q_ROWS = 4
NBR_KEY_ROWS = 12
NBR_Q = NBR_Q_ROWS * GRID_W
NBR_KEYS = NBR_KEY_ROWS * GRID_W
NBR_KEY_PAIRS = NBR_KEY_ROWS // 2
NBR_DR_PAD = (NBR_KEY_ROWS - NBR_Q_ROWS) - (WIN_ROWS - 1) + (NBR_Q_ROWS - 1)
NBR_DR_SLOTS = NBR_DR_PAD + (WIN_ROWS - 1) + (NBR_KEY_ROWS - 2) + 1

PROJ_TN = 512

V7X_VMEM_BYTES = 64 * 1024 * 1024
MIB = 1024 * 1024


def _params(semantics, vmem_mib):
    assert vmem_mib * MIB < V7X_VMEM_BYTES
    return pltpu.CompilerParams(dimension_semantics=semantics, vmem_limit_bytes=vmem_mib * MIB)


def _resident(shape):
    zeros = (0,) * len(shape)
    return pl.BlockSpec(shape, lambda *_: zeros, pipeline_mode=pl.Buffered(1))


def _rms(x):
    return x * lax.rsqrt(jnp.mean(x * x, axis=-1, keepdims=True) + EPS)


ROW_CHUNK = 16
ROW_UNROLL = 8

def _ffn_kernel(x_hbm, g1_ref, wg_ref, wu_ref, wd_ref, g2_ref, h_ref, n_ref, x_buf, xn_ref, x_sem):
    i = pl.program_id(0)
    j = pl.program_id(1)
    tm = x_buf.shape[0]

    def x_copy(tile):
        return pltpu.make_async_copy(x_hbm.at[pl.ds(tile * tm, tm), :], x_buf, x_sem)

    @pl.when(j == 0)
    def _():
        @pl.when(i == 0)
        def _():
            x_copy(0).start()

        x_copy(i).wait()

        def first(r, carry):
            rows = pl.ds(pl.multiple_of(r * ROW_CHUNK, ROW_CHUNK), ROW_CHUNK)
            x = x_buf[rows, :]
            xn_ref[rows, :] = (_rms(x) * g1_ref[...]).astype(BF16)
            h_ref[rows, :] = 2.0 * x
            return carry

        lax.fori_loop(0, tm // ROW_CHUNK, first, 0, unroll=ROW_UNROLL)

        @pl.when(i + 1 < pl.num_programs(0))
        def _():
            x_copy(i + 1).start()

    xn = xn_ref[...]
    g = jnp.dot(xn, wg_ref[...], preferred_element_type=F32)
    u = jnp.dot(xn, wu_ref[...], preferred_element_type=F32)
    a = (g * jax.nn.sigmoid(g) * u).astype(BF16)
    h_ref[...] += jnp.dot(a, wd_ref[...], preferred_element_type=F32)

    @pl.when(j == pl.num_programs(1) - 1)
    def _():
        def last(r, carry):
            rows = pl.ds(pl.multiple_of(r * ROW_CHUNK, ROW_CHUNK), ROW_CHUNK)
            h = 0.5 * h_ref[rows, :]
            h_ref[rows, :] = h
            n_ref[rows, :] = (_rms(h) * g2_ref[...]).astype(BF16)
            return carry

        lax.fori_loop(0, tm // ROW_CHUNK, last, 0, unroll=ROW_UNROLL)


def _ffn(x, g1, wg, wu, wd, g2, *, tm=1024, tf=512):
    t, d = x.shape
    dff = wg.shape[1]
    assert t % tm == 0 and dff % tf == 0
    return pl.pallas_call(
        _ffn_kernel,
        out_shape=(jax.ShapeDtypeStruct((t, d), F32), jax.ShapeDtypeStruct((t, d), BF16)),
        grid=(t // tm, dff // tf),
        in_specs=[
            pl.BlockSpec(memory_space=pl.ANY),
            pl.BlockSpec((1, d), lambda i, j: (0, 0)),
            pl.BlockSpec((d, tf), lambda i, j: (0, j)),
            pl.BlockSpec((d, tf), lambda i, j: (0, j)),
            pl.BlockSpec((tf, d), lambda i, j: (j, 0)),
            pl.BlockSpec((1, d), lambda i, j: (0, 0)),
        ],
        out_specs=(
            pl.BlockSpec((tm, d), lambda i, j: (i, 0)),
            pl.BlockSpec((tm, d), lambda i, j: (i, 0)),
        ),
        scratch_shapes=[pltpu.VMEM((tm, d), F32), pltpu.VMEM((tm, d), BF16), pltpu.SemaphoreType.DMA(())],
        compiler_params=_params(("arbitrary", "arbitrary"), 60),
        name="ffn",
    )(x, g1, wg, wu, wd, g2)


def _qkv_a_kernel(u_ref, wr_ref, wv_ref, gain_ref, cos_ref, sin_ref, q_ref, k_ref, v_ref):
    u = u_ref[...]
    c = cos_ref[...]
    s = sin_ref[...]
    z = jnp.dot(u, wr_ref[...], preferred_element_type=F32)
    for hh in range(N_HEADS_A + N_KV_A):
        zh = z[:, hh * HEAD_DIM:(hh + 1) * HEAD_DIM]
        zn = _rms(zh) * gain_ref[hh:hh + 1, :]
        out = (zn * c + pltpu.roll(zn, HEAD_DIM // 2, 1) * s).astype(BF16)
        if hh < N_HEADS_A:
            q_ref[:, hh * HEAD_DIM:(hh + 1) * HEAD_DIM] = out
        else:
            kk = hh - N_HEADS_A
            k_ref[:, kk * HEAD_DIM:(kk + 1) * HEAD_DIM] = out
    v_ref[...] = jnp.dot(u, wv_ref[...], preferred_element_type=F32).astype(BF16)


def _qkv_a(u, w_rot, w_in, gain, cos, sin, seq, *, tm=512):
    t, d = u.shape
    assert t % tm == 0 and seq % tm == 0 and ROT_WIDTH % KV_WIDTH_A == 0
    seq_blocks = seq // tm
    return pl.pallas_call(
        _qkv_a_kernel,
        out_shape=(
            jax.ShapeDtypeStruct((t, WIDTH_A), BF16),
            jax.ShapeDtypeStruct((t, KV_WIDTH_A), BF16),
            jax.ShapeDtypeStruct((t, KV_WIDTH_A), BF16),
        ),
        grid=(t // tm,),
        in_specs=[
            pl.BlockSpec((tm, d), lambda i: (i, 0)),
            _resident((d, ROT_WIDTH)),
            pl.BlockSpec((d, KV_WIDTH_A), lambda i: (0, ROT_WIDTH // KV_WIDTH_A), pipeline_mode=pl.Buffered(1)),
            _resident((N_HEADS_A + N_KV_A, HEAD_DIM)),
            pl.BlockSpec((tm, HEAD_DIM), lambda i: (i % seq_blocks, 0)),
            pl.BlockSpec((tm, HEAD_DIM), lambda i: (i % seq_blocks, 0)),
        ],
        out_specs=(
            pl.BlockSpec((tm, WIDTH_A), lambda i: (i, 0)),
            pl.BlockSpec((tm, KV_WIDTH_A), lambda i: (i, 0)),
            pl.BlockSpec((tm, KV_WIDTH_A), lambda i: (i, 0)),
        ),
        compiler_params=_params(("parallel",), 40),
        name="qkv_a",
    )(u, w_rot, w_in, gain, cos, sin)


def _proj_kernel(u_ref, w_ref, scale_ref, z_ref):
    z = jnp.dot(u_ref[...], w_ref[...], preferred_element_type=F32)
    z_ref[...] = (z * scale_ref[...]).astype(BF16)


def _proj(u, w, src_cols, col_scale, *, tm=2048, tn=PROJ_TN):
    t, d = u.shape
    n = len(src_cols) * tn
    assert t % tm == 0 and col_scale.shape == (1, n)

    def w_block(i, j):
        src = src_cols[0]
        for k in range(1, len(src_cols)):
            src = jnp.where(j >= k, src_cols[k], src)
        return (0, src)

    return pl.pallas_call(
        _proj_kernel,
        out_shape=jax.ShapeDtypeStruct((t, n), BF16),
        grid=(t // tm, len(src_cols)),
        in_specs=[
            pl.BlockSpec((tm, d), lambda i, j: (i, 0)),
            pl.BlockSpec((d, tn), w_block),
            pl.BlockSpec((1, tn), lambda i, j: (0, j)),
        ],
        out_specs=pl.BlockSpec((tm, tn), lambda i, j: (i, j)),
        compiler_params=_params(("parallel", "arbitrary"), 40),
        name="proj",
    )(u, w, col_scale)


def _softmax_pv(s, v):
    m = jnp.max(s, axis=-1, keepdims=True)
    p = jnp.exp2(s - m)
    l = jnp.sum(p, axis=-1, keepdims=True)
    return jnp.dot(p.astype(BF16), v, preferred_element_type=F32) / l


_NT = (((1,), (1,)), ((), ()))


def _gqa_kernel(q_ref, k_ref, v_ref, o_ref, *, bounded):
    k = k_ref[0]
    v = v_ref[0]
    for g in range(GQA_GROUP):
        q = q_ref[0, :, g * HEAD_DIM:(g + 1) * HEAD_DIM]
        s = lax.dot_general(q, k, _NT, preferred_element_type=F32)
        if bounded:
            p = jnp.exp2(s)
            l = jnp.sum(p, axis=-1, keepdims=True)
            o = jnp.dot(p.astype(BF16), v, preferred_element_type=F32) / l
        else:
            o = _softmax_pv(s, v)
        o_ref[0, :, g * HEAD_DIM:(g + 1) * HEAD_DIM] = o.astype(BF16)


def _gqa(q, k, v, *, bounded, tq=512):
    b, s, _ = q.shape
    assert s % tq == 0
    gw = GQA_GROUP * HEAD_DIM
    return pl.pallas_call(
        functools.partial(_gqa_kernel, bounded=bounded),
        out_shape=jax.ShapeDtypeStruct((b, s, WIDTH_A), BF16),
        grid=(b, N_KV_A, s // tq),
        in_specs=[
            pl.BlockSpec((1, tq, gw), lambda bi, kh, qi: (bi, qi, kh)),
            pl.BlockSpec((1, s, HEAD_DIM), lambda bi, kh, qi: (bi, 0, kh)),
            pl.BlockSpec((1, s, HEAD_DIM), lambda bi, kh, qi: (bi, 0, kh)),
        ],
        out_specs=pl.BlockSpec((1, tq, gw), lambda bi, kh, qi: (bi, qi, kh)),
        compiler_params=_params(("parallel", "parallel", "arbitrary"), 40),
        name="gqa_bounded" if bounded else "gqa",
    )(q, k, v)


def _nbr_window_start(j, rows):
    return jnp.clip(NBR_Q_ROWS * j - WIN_ROWS // 2, 0, rows - NBR_KEY_ROWS)


def _nbr_kernel(q_ref, k_ref, v_ref, cb_ref, rm_ref, o_ref, *, rows):
    j = pl.program_id(1)
    ws = _nbr_window_start(j, rows)
    start = pl.multiple_of(ws * GRID_W, NBR_Q)
    d0 = ws - NBR_Q_ROWS * j + (WIN_ROWS - 1) + NBR_DR_PAD
    for h in range(N_HEADS_B):
        cols = slice(h * HEAD_DIM, (h + 1) * HEAD_DIM)
        q = q_ref[0, :, cols]
        k = k_ref[0, pl.ds(start, NBR_KEYS), cols]
        v = v_ref[0, pl.ds(start, NBR_KEYS), cols]
        s = lax.dot_general(q, k, _NT, preferred_element_type=F32)
        s = jnp.concatenate([
            jnp.concatenate([
                s[qr * GRID_W:(qr + 1) * GRID_W, a * 2 * GRID_W:(a + 1) * 2 * GRID_W]
                + cb_ref[h, d0 + 2 * a - qr]
                + rm_ref[0, qr * NBR_KEY_PAIRS + a:qr * NBR_KEY_PAIRS + a + 1, :]
                for a in range(NBR_KEY_PAIRS)], axis=1)
            for qr in range(NBR_Q_ROWS)], axis=0)
        o_ref[0, :, cols] = _softmax_pv(s, v).astype(BF16)


def _nbr_block_type(j, n_blocks):
    return jnp.minimum(j, 2) + j // (n_blocks - 1)


def _nbr(z, col_bias, row_mask, q_col, k_col, v_col):
    b, s, _ = z.shape
    rows = s // GRID_W
    n_blocks = rows // NBR_Q_ROWS
    return pl.pallas_call(
        functools.partial(_nbr_kernel, rows=rows),
        out_shape=jax.ShapeDtypeStruct((b, s, WIDTH_B), BF16),
        grid=(b, n_blocks),
        in_specs=[
            pl.BlockSpec((1, NBR_Q, WIDTH_B), lambda bi, j: (bi, j, q_col)),
            pl.BlockSpec((1, s, WIDTH_B), lambda bi, j: (bi, 0, k_col)),
            pl.BlockSpec((1, s, WIDTH_B), lambda bi, j: (bi, 0, v_col)),
            _resident(col_bias.shape),
            pl.BlockSpec((1,) + row_mask.shape[1:], lambda bi, j: (_nbr_block_type(j, n_blocks), 0, 0)),
        ],
        out_specs=pl.BlockSpec((1, NBR_Q, WIDTH_B), lambda bi, j: (bi, j, 0)),
        compiler_params=_params(("parallel", "arbitrary"), 48),
        name="nbr",
    )(z, z, z, col_bias, row_mask)


def _nbr_col_bias(rpb):
    n_dr, n_dc = 2 * WIN_ROWS - 1, 2 * WIN_COLS - 1
    c = np.arange(GRID_W)[:, None]
    kc = np.arange(GRID_W)[None, :]
    cs = np.clip(c - WIN_COLS // 2, 0, GRID_W - WIN_COLS)
    col_valid = (kc >= cs) & (kc < cs + WIN_COLS)
    onehot = ((kc - c + (WIN_COLS - 1))[None] == np.arange(n_dc)[:, None, None]) & col_valid[None]
    onehot = jnp.asarray(onehot.reshape(n_dc, GRID_W * GRID_W), F32)
    tm = jnp.einsum("hrd,dn->hrn", rpb * LOG2_E, onehot, precision=lax.Precision.HIGHEST)
    tm = jnp.where(col_valid.reshape(-1), tm, MASKED).reshape(-1, n_dr, GRID_W, GRID_W)
    hi_pad = NBR_DR_SLOTS + 1 - NBR_DR_PAD - n_dr
    tm = jnp.pad(tm, ((0, 0), (NBR_DR_PAD, hi_pad), (0, 0), (0, 0)), constant_values=MASKED)
    return jnp.concatenate([tm[:, :-1], tm[:, 1:]], axis=-1)


def _nbr_row_mask(seq):
    rows = seq // GRID_W
    n_blocks = rows // NBR_Q_ROWS
    assert rows >= NBR_KEY_ROWS and n_blocks >= 4
    out = np.zeros((4, NBR_Q_ROWS, NBR_KEY_ROWS, GRID_W), np.float32)
    for t, j in enumerate((0, 1, 2, n_blocks - 1)):
        start = int(np.clip(NBR_Q_ROWS * j - WIN_ROWS // 2, 0, rows - NBR_KEY_ROWS))
        r = (NBR_Q_ROWS * j + np.arange(NBR_Q_ROWS))[:, None]
        kr = (start + np.arange(NBR_KEY_ROWS))[None, :]
        rs = np.clip(r - WIN_ROWS // 2, 0, rows - WIN_ROWS)
        valid = (kr >= rs) & (kr < rs + WIN_ROWS)
        out[t] = np.where(valid, 0.0, MASKED)[:, :, None]
    return jnp.asarray(out.reshape(4, NBR_Q_ROWS * NBR_KEY_PAIRS, 2 * GRID_W))


MERGE_TN = 512


def _merge_kernel(oa_ref, ob_ref, ga_ref, gb_ref, h_ref, wa_ref, wb_ref, wo_ref, out_ref, m_ref):
    oa = oa_ref[...]
    ob = ob_ref[...]
    for c in range(D_MODEL // MERGE_TN):
        cols = slice(c * MERGE_TN, (c + 1) * MERGE_TN)
        a = jnp.dot(oa, wa_ref[:, cols], preferred_element_type=F32)
        b = jnp.dot(ob, wb_ref[:, cols], preferred_element_type=F32)
        merged = (jax.nn.sigmoid(ga_ref[:, cols].astype(F32)) * a
                  + jax.nn.sigmoid(gb_ref[:, cols].astype(F32)) * b)
        m_ref[:, cols] = merged.astype(BF16)
    out_ref[...] = h_ref[...] + jnp.dot(m_ref[...], wo_ref[...], preferred_element_type=F32)


def _merge(oa, ob, z, ga_col, gb_col, h, wa, wb, wo, *, tm=512):
    t, d = h.shape
    assert t % tm == 0 and d == D_MODEL
    return pl.pallas_call(
        _merge_kernel,
        out_shape=jax.ShapeDtypeStruct((t, d), F32),
        grid=(t // tm,),
        in_specs=[
            pl.BlockSpec((tm, WIDTH_A), lambda i: (i, 0)),
            pl.BlockSpec((tm, WIDTH_B), lambda i: (i, 0)),
            pl.BlockSpec((tm, d), lambda i: (i, ga_col)),
            pl.BlockSpec((tm, d), lambda i: (i, gb_col)),
            pl.BlockSpec((tm, d), lambda i: (i, 0)),
            _resident((WIDTH_A, d)),
            _resident((WIDTH_B, d)),
            _resident((d, d)),
        ],
        out_specs=pl.BlockSpec((tm, d), lambda i: (i, 0)),
        scratch_shapes=[pltpu.VMEM((tm, d), BF16)],
        compiler_params=_params(("parallel",), 56),
        name="merge",
    )(oa, ob, z, z, h, wa, wb, wo)


def _ple_kernel(h_ref, n_ref, p_ref, wg_ref, wp_ref, gf_ref, y_ref):
    gate = jax.nn.sigmoid(jnp.dot(n_ref[...], wg_ref[...], preferred_element_type=F32))
    emb = jnp.dot(p_ref[...].astype(BF16), wp_ref[...], preferred_element_type=F32)
    y_ref[...] = _rms(h_ref[...] + gate * emb) * gf_ref[...]


def _ple(h, n, p, wg, wp, gf, *, tm=512):
    t, d = h.shape
    assert t % tm == 0
    return pl.pallas_call(
        _ple_kernel,
        out_shape=jax.ShapeDtypeStruct((t, d), F32),
        grid=(t // tm,),
        in_specs=[
            pl.BlockSpec((tm, d), lambda i: (i, 0)),
            pl.BlockSpec((tm, d), lambda i: (i, 0)),
            pl.BlockSpec((tm, D_PLE), lambda i: (i, 0)),
            _resident((d, d)),
            _resident((D_PLE, d)),
            _resident((1, d)),
        ],
        out_specs=pl.BlockSpec((tm, d), lambda i: (i, 0)),
        compiler_params=_params(("parallel",), 48),
        name="ple",
    )(h, n, p, wg, wp, gf)


CAST_BLOCK_BYTES = 4 * MIB


def _cast_kernel(w_ref, o_ref):
    o_ref[...] = w_ref[...].astype(BF16)


def _to_bf16(w):
    r, c = w.shape
    target = max(16, CAST_BLOCK_BYTES // (4 * c))
    rows = next(n for n in range(min(r, target) // 16 * 16, 0, -16) if r % n == 0)
    return pl.pallas_call(
        _cast_kernel,
        out_shape=jax.ShapeDtypeStruct((r, c), BF16),
        grid=(r // rows,),
        in_specs=[pl.BlockSpec((rows, c), lambda i: (i, 0))],
        out_specs=pl.BlockSpec((rows, c), lambda i: (i, 0)),
        compiler_params=_params(("parallel",), 32),
        name="to_bf16",
    )(w)


def _rope_tables(seq):
    n_freq = HEAD_DIM // 4
    inv_freq = ROPE_THETA ** (-jnp.arange(n_freq, dtype=F32) / n_freq)
    t = jnp.arange(seq)
    row = (t // GRID_W).astype(F32)
    col = (t % GRID_W).astype(F32)
    ang = jnp.concatenate([row[:, None] * inv_freq[None], col[:, None] * inv_freq[None]], axis=-1)
    cos, sin = jnp.cos(ang), jnp.sin(ang)
    return jnp.concatenate([cos, cos], axis=-1), jnp.concatenate([-sin, sin], axis=-1)


_W_IN_SEGMENTS = dict(q_b=QKV_A_WIDTH, k_b=QKV_A_WIDTH + WIDTH_B, v_b=QKV_A_WIDTH + 2 * WIDTH_B,
                      g_a=QKV_A_WIDTH + 3 * WIDTH_B, g_b=QKV_A_WIDTH + 3 * WIDTH_B + D_MODEL)
_PROJ_LAYOUT = (("g_a", D_MODEL), ("g_b", D_MODEL), ("q_b", WIDTH_B), ("k_b", WIDTH_B), ("v_b", WIDTH_B))
_PROJ_SRC_COLS = tuple((_W_IN_SEGMENTS[name] + off) // PROJ_TN
                       for name, width in _PROJ_LAYOUT for off in range(0, width, PROJ_TN))
assert all(start % PROJ_TN == 0 for start in _W_IN_SEGMENTS.values())
_GA_COL, _GB_COL = 0, 1
_QB_COL, _KB_COL, _VB_COL = 4, 5, 6
assert 2 * D_MODEL == _QB_COL * WIDTH_B


def _prepare_weights(ffn1_norm, ffn1_w_gate, ffn1_w_up, ffn1_w_down, mix_norm, w_in, q_norm, k_norm,
                     w_branch_a, w_branch_b, w_out, ffn2_norm, ffn2_w_gate, ffn2_w_up, ffn2_w_down,
                     ple_norm, w_ple_gate, w_ple_proj, final_norm, layer):
    i = layer
    half = np.concatenate([np.arange(0, HEAD_DIM, 2), np.arange(1, HEAD_DIM, 2)])
    n_rot = N_HEADS_A + N_KV_A
    w = w_in[i]
    w_rot = w[:, :ROT_WIDTH].reshape(D_MODEL, n_rot, HEAD_DIM // 2, 2)
    w_rot = jnp.swapaxes(w_rot, 2, 3).reshape(D_MODEL, ROT_WIDTH).astype(BF16)
    q_scale = ATTN_SCALE * LOG2_E
    proj_scale = jnp.concatenate([jnp.full((width,), q_scale if name == "q_b" else 1.0, F32)
                                  for name, width in _PROJ_LAYOUT])[None]
    gain = jnp.concatenate([jnp.tile((q_norm[i] * q_scale)[half][None], (N_HEADS_A, 1)),
                            jnp.tile(k_norm[i][half][None], (N_KV_A, 1))])
    gqa_score_bound = (HEAD_DIM * q_scale * BF16_NORM_MARGIN
                       * jnp.max(jnp.abs(q_norm[i])) * jnp.max(jnp.abs(k_norm[i])))
    return dict(
        ffn1=(ffn1_norm[i][None], _to_bf16(ffn1_w_gate[i]), _to_bf16(ffn1_w_up[i]),
              _to_bf16(ffn1_w_down[i]), mix_norm[i][None]),
        w_rot=w_rot, gain=gain, w_in=_to_bf16(w), proj_scale=proj_scale, gqa_score_bound=gqa_score_bound,
        merge=(_to_bf16(w_branch_a[i]), _to_bf16(w_branch_b[i]), _to_bf16(w_out[i])),
        ffn2=(ffn2_norm[i][None], _to_bf16(ffn2_w_gate[i]), _to_bf16(ffn2_w_up[i]),
              _to_bf16(ffn2_w_down[i]), ple_norm[i][None]),
        ple=(_to_bf16(w_ple_gate[i]), _to_bf16(w_ple_proj[i]), final_norm[None]),
    )


def _encoder(x, p, wts, cos, sin, col_bias, row_mask):
    b, s, d = x.shape
    t = b * s
    h1, u = _ffn(x.reshape(t, d), *wts["ffn1"])
    q_a, k_a, v_a = _qkv_a(u, wts["w_rot"], wts["w_in"], wts["gain"], cos, sin, s)
    z = _proj(u, wts["w_in"], _PROJ_SRC_COLS, wts["proj_scale"])
    qkv = (q_a.reshape(b, s, -1), k_a.reshape(b, s, -1), v_a.reshape(b, s, -1))
    o_a = lax.cond(wts["gqa_score_bound"] <= GQA_SAFE_LOG2_SCORE,
                   functools.partial(_gqa, bounded=True), functools.partial(_gqa, bounded=False), *qkv)
    o_b = _nbr(z.reshape(b, s, -1), col_bias, row_mask, _QB_COL, _KB_COL, _VB_COL)
    h2 = _merge(o_a.reshape(t, -1), o_b.reshape(t, -1), z, _GA_COL, _GB_COL, h1, *wts["merge"])
    h3, n = _ffn(h2, *wts["ffn2"])
    y = _ple(h3, n, p.reshape(t, -1), *wts["ple"])
    return y.reshape(b, s, d)


def kernel(x_prompt, x_sample, p_prompt, p_sample, ffn1_norm, ffn1_w_gate, ffn1_w_up, ffn1_w_down, mix_norm, w_in, q_norm, k_norm, nat_rpb, w_branch_a, w_branch_b, w_out, ffn2_norm, ffn2_w_gate, ffn2_w_up, ffn2_w_down, ple_norm, w_ple_gate, w_ple_proj, final_norm):
    assert ffn1_norm.shape[0] == 1, "single-layer encoder"
    wts = _prepare_weights(ffn1_norm, ffn1_w_gate, ffn1_w_up, ffn1_w_down, mix_norm, w_in, q_norm, k_norm,
                           w_branch_a, w_branch_b, w_out, ffn2_norm, ffn2_w_gate, ffn2_w_up, ffn2_w_down,
                           ple_norm, w_ple_gate, w_ple_proj, final_norm, 0)
    col_bias = _nbr_col_bias(nat_rpb[0])
    tables = {}
    outs = []
    for x, p in ((x_prompt, p_prompt), (x_sample, p_sample)):
        seq = x.shape[1]
        if seq not in tables:
            tables[seq] = _rope_tables(seq) + (_nbr_row_mask(seq),)
        cos, sin, row_mask = tables[seq]
        outs.append(_encoder(x, p[0], wts, cos, sin, col_bias, row_mask))
    return tuple(outs)
```

```python
import functools
import math

import numpy as np
import jax
import jax.numpy as jnp
from jax import lax
from jax.experimental import pallas as pl
from jax.experimental.pallas import tpu as pltpu

F32 = jnp.float32
BF16 = jnp.bfloat16

D_MODEL = 2048
HEAD_DIM = 128
N_HEADS_A = 8
N_KV_A = 2
GQA_GROUP = N_HEADS_A // N_KV_A
N_HEADS_B = 8
WIDTH_A = N_HEADS_A * HEAD_DIM
KV_WIDTH_A = N_KV_A * HEAD_DIM
WIDTH_B = N_HEADS_B * HEAD_DIM
ROT_WIDTH = WIDTH_A + KV_WIDTH_A
QKV_A_WIDTH = ROT_WIDTH + KV_WIDTH_A
D_FF = 5632
D_PLE = 256
GRID_W = 64
WIN_ROWS = 8
WIN_COLS = 16
ROPE_THETA = 10000.0
EPS = 1e-6
ATTN_SCALE = HEAD_DIM ** -0.5
LOG2_E = math.log2(math.e)
MASKED = -1e30
GQA_SAFE_LOG2_SCORE = 48.0
BF16_NORM_MARGIN = 1.02

NBR_Q_ROWS = 4
NBR_KEY_ROWS = 12
NBR_Q = NBR_Q_ROWS * GRID_W
NBR_KEYS = NBR_KEY_ROWS * GRID_W
NBR_DR_PAD = (NBR_KEY_ROWS - NBR_Q_ROWS) - (WIN_ROWS - 1) + (NBR_Q_ROWS - 1)
NBR_DR_SLOTS = NBR_DR_PAD + (WIN_ROWS - 1) + (NBR_KEY_ROWS - 2) + 1

PROJ_TN = 512

V7X_VMEM_BYTES = 64 * 1024 * 1024
MIB = 1024 * 1024


def _params(semantics, vmem_mib):
    assert vmem_mib * MIB < V7X_VMEM_BYTES
    return pltpu.CompilerParams(dimension_semantics=semantics, vmem_limit_bytes=vmem_mib * MIB)


def _resident(shape):
    zeros = (0,) * len(shape)
    return pl.BlockSpec(shape, lambda *_: zeros, pipeline_mode=pl.Buffered(1))


def _rms(x):
    return x * lax.rsqrt(jnp.mean(x * x, axis=-1, keepdims=True) + EPS)


FFN_RESIDUAL_SCALE = 0.5
ROW_CHUNK = 16
ROW_UNROLL = 8

def _ffn_kernel(x_hbm, g1_ref, wg_ref, wu_ref, wd_ref, g2_ref, h_ref, n_ref, x_buf, xn_ref, x_sem):
    i = pl.program_id(0)
    j = pl.program_id(1)
    tm = x_buf.shape[0]

    def x_copy(tile):
        return pltpu.make_async_copy(x_hbm.at[pl.ds(tile * tm, tm), :], x_buf, x_sem)

    @pl.when(j == 0)
    def _():
        @pl.when(i == 0)
        def _():
            x_copy(0).start()

        x_copy(i).wait()

        def first(r, carry):
            rows = pl.ds(pl.multiple_of(r * ROW_CHUNK, ROW_CHUNK), ROW_CHUNK)
            x = x_buf[rows, :]
            xn_ref[rows, :] = (_rms(x) * g1_ref[...]).astype(BF16)
            h_ref[rows, :] = x
            return carry

        lax.fori_loop(0, tm // ROW_CHUNK, first, 0, unroll=ROW_UNROLL)

        @pl.when(i + 1 < pl.num_programs(0))
        def _():
            x_copy(i + 1).start()

    xn = xn_ref[...]
    g = jnp.dot(xn, wg_ref[...], preferred_element_type=F32)
    u = jnp.dot(xn, wu_ref[...], preferred_element_type=F32)
    a = (g * jax.nn.sigmoid(g) * u).astype(BF16)
    h_ref[...] += jnp.dot(a, wd_ref[...], preferred_element_type=F32)

    @pl.when(j == pl.num_programs(1) - 1)
    def _():
        def last(r, carry):
            rows = pl.ds(pl.multiple_of(r * ROW_CHUNK, ROW_CHUNK), ROW_CHUNK)
            n_ref[rows, :] = (_rms(h_ref[rows, :]) * g2_ref[...]).astype(BF16)
            return carry

        lax.fori_loop(0, tm // ROW_CHUNK, last, 0, unroll=ROW_UNROLL)


def _ffn(x, g1, wg, wu, wd, g2, *, tm=1024, tf=512):
    t, d = x.shape
    dff = wg.shape[1]
    assert t % tm == 0 and dff % tf == 0
    return pl.pallas_call(
        _ffn_kernel,
        out_shape=(jax.ShapeDtypeStruct((t, d), F32), jax.ShapeDtypeStruct((t, d), BF16)),
        grid=(t // tm, dff // tf),
        in_specs=[
            pl.BlockSpec(memory_space=pl.ANY),
            pl.BlockSpec((1, d), lambda i, j: (0, 0)),
            pl.BlockSpec((d, tf), lambda i, j: (0, j)),
            pl.BlockSpec((d, tf), lambda i, j: (0, j)),
            pl.BlockSpec((tf, d), lambda i, j: (j, 0)),
            pl.BlockSpec((1, d), lambda i, j: (0, 0)),
        ],
        out_specs=(
            pl.BlockSpec((tm, d), lambda i, j: (i, 0)),
            pl.BlockSpec((tm, d), lambda i, j: (i, 0)),
        ),
        scratch_shapes=[pltpu.VMEM((tm, d), F32), pltpu.VMEM((tm, d), BF16), pltpu.SemaphoreType.DMA(())],
        compiler_params=_params(("arbitrary", "arbitrary"), 60),
        name="ffn",
    )(x, g1, wg, wu, wd, g2)


def _qkv_a_kernel(u_ref, wr_ref, wv_ref, gain_ref, cos_ref, sin_ref, q_ref, k_ref, v_ref):
    u = u_ref[...]
    c = cos_ref[...]
    s = sin_ref[...]
    z = jnp.dot(u, wr_ref[...], preferred_element_type=F32)
    for hh in range(N_HEADS_A + N_KV_A):
        zh = z[:, hh * HEAD_DIM:(hh + 1) * HEAD_DIM]
        zn = _rms(zh) * gain_ref[hh:hh + 1, :]
        out = (zn * c + pltpu.roll(zn, HEAD_DIM // 2, 1) * s).astype(BF16)
        if hh < N_HEADS_A:
            q_ref[:, hh * HEAD_DIM:(hh + 1) * HEAD_DIM] = out
        else:
            kk = hh - N_HEADS_A
            k_ref[:, kk * HEAD_DIM:(kk + 1) * HEAD_DIM] = out
    v_ref[...] = jnp.dot(u, wv_ref[...], preferred_element_type=F32).astype(BF16)


def _qkv_a(u, w_rot, w_in, gain, cos, sin, seq, *, tm=512):
    t, d = u.shape
    assert t % tm == 0 and seq % tm == 0 and ROT_WIDTH % KV_WIDTH_A == 0
    seq_blocks = seq // tm
    return pl.pallas_call(
        _qkv_a_kernel,
        out_shape=(
            jax.ShapeDtypeStruct((t, WIDTH_A), BF16),
            jax.ShapeDtypeStruct((t, KV_WIDTH_A), BF16),
            jax.ShapeDtypeStruct((t, KV_WIDTH_A), BF16),
        ),
        grid=(t // tm,),
        in_specs=[
            pl.BlockSpec((tm, d), lambda i: (i, 0)),
            _resident((d, ROT_WIDTH)),
            pl.BlockSpec((d, KV_WIDTH_A), lambda i: (0, ROT_WIDTH // KV_WIDTH_A), pipeline_mode=pl.Buffered(1)),
            _resident((N_HEADS_A + N_KV_A, HEAD_DIM)),
            pl.BlockSpec((tm, HEAD_DIM), lambda i: (i % seq_blocks, 0)),
            pl.BlockSpec((tm, HEAD_DIM), lambda i: (i % seq_blocks, 0)),
        ],
        out_specs=(
            pl.BlockSpec((tm, WIDTH_A), lambda i: (i, 0)),
            pl.BlockSpec((tm, KV_WIDTH_A), lambda i: (i, 0)),
            pl.BlockSpec((tm, KV_WIDTH_A), lambda i: (i, 0)),
        ),
        compiler_params=_params(("parallel",), 40),
        name="qkv_a",
    )(u, w_rot, w_in, gain, cos, sin)


def _proj_kernel(u_ref, w_ref, scale_ref, z_ref):
    z = jnp.dot(u_ref[...], w_ref[...], preferred_element_type=F32)
    z_ref[...] = (z * scale_ref[...]).astype(BF16)


def _proj(u, w, src_cols, col_scale, *, tm=2048, tn=PROJ_TN):
    t, d = u.shape
    n = len(src_cols) * tn
    assert t % tm == 0 and col_scale.shape == (1, n)

    def w_block(i, j):
        src = src_cols[0]
        for k in range(1, len(src_cols)):
            src = jnp.where(j >= k, src_cols[k], src)
        return (0, src)

    return pl.pallas_call(
        _proj_kernel,
        out_shape=jax.ShapeDtypeStruct((t, n), BF16),
        grid=(t // tm, len(src_cols)),
        in_specs=[
            pl.BlockSpec((tm, d), lambda i, j: (i, 0)),
            pl.BlockSpec((d, tn), w_block),
            pl.BlockSpec((1, tn), lambda i, j: (0, j)),
        ],
        out_specs=pl.BlockSpec((tm, tn), lambda i, j: (i, j)),
        compiler_params=_params(("parallel", "arbitrary"), 40),
        name="proj",
    )(u, w, col_scale)


def _softmax_pv(s, v):
    m = jnp.max(s, axis=-1, keepdims=True)
    p = jnp.exp2(s - m)
    l = jnp.sum(p, axis=-1, keepdims=True)
    return jnp.dot(p.astype(BF16), v, preferred_element_type=F32) / l


_NT = (((1,), (1,)), ((), ()))


def _gqa_kernel(q_ref, k_ref, v_ref, o_ref, *, bounded):
    k = k_ref[0]
    v = v_ref[0]
    for g in range(GQA_GROUP):
        q = q_ref[0, :, g * HEAD_DIM:(g + 1) * HEAD_DIM]
        s = lax.dot_general(q, k, _NT, preferred_element_type=F32)
        if bounded:
            p = jnp.exp2(s)
            l = jnp.sum(p, axis=-1, keepdims=True)
            o = jnp.dot(p.astype(BF16), v, preferred_element_type=F32) / l
        else:
            o = _softmax_pv(s, v)
        o_ref[0, :, g * HEAD_DIM:(g + 1) * HEAD_DIM] = o.astype(BF16)


def _gqa(q, k, v, *, bounded, tq=512):
    b, s, _ = q.shape
    assert s % tq == 0
    gw = GQA_GROUP * HEAD_DIM
    return pl.pallas_call(
        functools.partial(_gqa_kernel, bounded=bounded),
        out_shape=jax.ShapeDtypeStruct((b, s, WIDTH_A), BF16),
        grid=(b, N_KV_A, s // tq),
        in_specs=[
            pl.BlockSpec((1, tq, gw), lambda bi, kh, qi: (bi, qi, kh)),
            pl.BlockSpec((1, s, HEAD_DIM), lambda bi, kh, qi: (bi, 0, kh)),
            pl.BlockSpec((1, s, HEAD_DIM), lambda bi, kh, qi: (bi, 0, kh)),
        ],
        out_specs=pl.BlockSpec((1, tq, gw), lambda bi, kh, qi: (bi, qi, kh)),
        compiler_params=_params(("parallel", "parallel", "arbitrary"), 40),
        name="gqa_bounded" if bounded else "gqa",
    )(q, k, v)


def _nbr_window_start(j, rows):
    return jnp.clip(NBR_Q_ROWS * j - WIN_ROWS // 2, 0, rows - NBR_KEY_ROWS)


def _nbr_block_plans(rows):
    plans = {}
    for j in range(rows // NBR_Q_ROWS):
        ws = int(np.clip(NBR_Q_ROWS * j - WIN_ROWS // 2, 0, rows - NBR_KEY_ROWS))
        lo = tuple(int(np.clip(NBR_Q_ROWS * j + qr - WIN_ROWS // 2, 0, rows - WIN_ROWS)) - ws
                   for qr in range(NBR_Q_ROWS))
        d0 = ws - NBR_Q_ROWS * j + (WIN_ROWS - 1) + NBR_DR_PAD
        assert min(lo) >= 0 and max(lo) + WIN_ROWS <= NBR_KEY_ROWS
        plans.setdefault((lo, d0), []).append(j)
    return plans


def _nbr_block(q_ref, k_ref, v_ref, cb_ref, o_ref, ws, lo, d0):
    pair_w = 2 * GRID_W
    pa = min(lo) // 2
    pb = (max(lo) + WIN_ROWS - 1) // 2 + 1
    n_keys = (pb - pa) * pair_w
    start = pl.multiple_of(ws * GRID_W + pa * pair_w, pair_w)
    lane = lax.broadcasted_iota(jnp.int32, (GRID_W, pair_w), 1)
    zeros = jnp.zeros((GRID_W, pair_w), BF16)
    for h in range(N_HEADS_B):
        cols = slice(h * HEAD_DIM, (h + 1) * HEAD_DIM)
        q = q_ref[0, :, cols]
        k = k_ref[0, pl.ds(start, n_keys), cols]
        v = v_ref[0, pl.ds(start, n_keys), cols]
        s = lax.dot_general(q, k, _NT, preferred_element_type=F32)
        p_rows, l_rows = [], []
        for qr in range(NBR_Q_ROWS):
            first, last = lo[qr], lo[qr] + WIN_ROWS - 1
            tiles = []
            for a in range(first // 2, last // 2 + 1):
                t = (s[qr * GRID_W:(qr + 1) * GRID_W, (a - pa) * pair_w:(a - pa + 1) * pair_w]
                     + cb_ref[h, d0 + 2 * a - qr])
                if 2 * a < first:
                    t = jnp.where(lane < GRID_W, MASKED, t)
                if 2 * a + 1 > last:
                    t = jnp.where(lane >= GRID_W, MASKED, t)
                tiles.append(t)
            m = jnp.max(functools.reduce(jnp.maximum, tiles), axis=-1, keepdims=True)
            ps = [jnp.exp2(t - m) for t in tiles]
            l_rows.append(jnp.sum(functools.reduce(jnp.add, ps), axis=-1, keepdims=True))
            p_rows.append(jnp.concatenate(
                [zeros] * (first // 2 - pa) + [p.astype(BF16) for p in ps] + [zeros] * (pb - 1 - last // 2),
                axis=1))
        p = jnp.concatenate(p_rows, axis=0)
        l = jnp.concatenate(l_rows, axis=0)
        o_ref[0, :, cols] = (jnp.dot(p, v, preferred_element_type=F32) / l).astype(BF16)


def _nbr_kernel(q_ref, k_ref, v_ref, cb_ref, o_ref, *, rows):
    j = pl.program_id(1)
    ws = _nbr_window_start(j, rows)
    for (lo, d0), members in _nbr_block_plans(rows).items():
        member = functools.reduce(jnp.logical_or, [j == m for m in members])
        pl.when(member)(functools.partial(_nbr_block, q_ref, k_ref, v_ref, cb_ref, o_ref, ws, lo, d0))


def _nbr(z, col_bias, q_col, k_col, v_col):
    b, s, _ = z.shape
    rows = s // GRID_W
    assert rows % NBR_Q_ROWS == 0 and rows >= NBR_KEY_ROWS
    return pl.pallas_call(
        functools.partial(_nbr_kernel, rows=rows),
        out_shape=jax.ShapeDtypeStruct((b, s, WIDTH_B), BF16),
        grid=(b, rows // NBR_Q_ROWS),
        in_specs=[
            pl.BlockSpec((1, NBR_Q, WIDTH_B), lambda bi, j: (bi, j, q_col)),
            pl.BlockSpec((1, s, WIDTH_B), lambda bi, j: (bi, 0, k_col)),
            pl.BlockSpec((1, s, WIDTH_B), lambda bi, j: (bi, 0, v_col)),
            _resident(col_bias.shape),
        ],
        out_specs=pl.BlockSpec((1, NBR_Q, WIDTH_B), lambda bi, j: (bi, j, 0)),
        compiler_params=_params(("parallel", "arbitrary"), 48),
        name="nbr",
    )(z, z, z, col_bias)


def _nbr_col_bias(rpb):
    n_dr, n_dc = 2 * WIN_ROWS - 1, 2 * WIN_COLS - 1
    c = np.arange(GRID_W)[:, None]
    kc = np.arange(GRID_W)[None, :]
    cs = np.clip(c - WIN_COLS // 2, 0, GRID_W - WIN_COLS)
    col_valid = (kc >= cs) & (kc < cs + WIN_COLS)
    onehot = ((kc - c + (WIN_COLS - 1))[None] == np.arange(n_dc)[:, None, None]) & col_valid[None]
    onehot = jnp.asarray(onehot.reshape(n_dc, GRID_W * GRID_W), F32)
    tm = jnp.einsum("hrd,dn->hrn", rpb * LOG2_E, onehot, precision=lax.Precision.HIGHEST)
    tm = jnp.where(col_valid.reshape(-1), tm, MASKED).reshape(-1, n_dr, GRID_W, GRID_W)
    hi_pad = NBR_DR_SLOTS + 1 - NBR_DR_PAD - n_dr
    tm = jnp.pad(tm, ((0, 0), (NBR_DR_PAD, hi_pad), (0, 0), (0, 0)), constant_values=MASKED)
    return jnp.concatenate([tm[:, :-1], tm[:, 1:]], axis=-1)


MERGE_TN = 512


def _merge_kernel(oa_ref, ob_ref, ga_ref, gb_ref, h_ref, wa_ref, wb_ref, wo_ref, out_ref, m_ref):
    oa = oa_ref[...]
    ob = ob_ref[...]
    for c in range(D_MODEL // MERGE_TN):
        cols = slice(c * MERGE_TN, (c + 1) * MERGE_TN)
        a = jnp.dot(oa, wa_ref[:, cols], preferred_element_type=F32)
        b = jnp.dot(ob, wb_ref[:, cols], preferred_element_type=F32)
        merged = (jax.nn.sigmoid(ga_ref[:, cols].astype(F32)) * a
                  + jax.nn.sigmoid(gb_ref[:, cols].astype(F32)) * b)
        m_ref[:, cols] = merged.astype(BF16)
    out_ref[...] = h_ref[...] + jnp.dot(m_ref[...], wo_ref[...], preferred_element_type=F32)


def _merge(oa, ob, z, ga_col, gb_col, h, wa, wb, wo, *, tm=512):
    t, d = h.shape
    assert t % tm == 0 and d == D_MODEL
    return pl.pallas_call(
        _merge_kernel,
        out_shape=jax.ShapeDtypeStruct((t, d), F32),
        grid=(t // tm,),
        in_specs=[
            pl.BlockSpec((tm, WIDTH_A), lambda i: (i, 0)),
            pl.BlockSpec((tm, WIDTH_B), lambda i: (i, 0)),
            pl.BlockSpec((tm, d), lambda i: (i, ga_col)),
            pl.BlockSpec((tm, d), lambda i: (i, gb_col)),
            pl.BlockSpec((tm, d), lambda i: (i, 0)),
            _resident((WIDTH_A, d)),
            _resident((WIDTH_B, d)),
            _resident((d, d)),
        ],
        out_specs=pl.BlockSpec((tm, d), lambda i: (i, 0)),
        scratch_shapes=[pltpu.VMEM((tm, d), BF16)],
        compiler_params=_params(("parallel",), 56),
        name="merge",
    )(oa, ob, z, z, h, wa, wb, wo)


def _ple_kernel(h_ref, n_ref, p_ref, wg_ref, wp_ref, gf_ref, y_ref):
    gate = jax.nn.sigmoid(jnp.dot(n_ref[...], wg_ref[...], preferred_element_type=F32))
    emb = jnp.dot(p_ref[...].astype(BF16), wp_ref[...], preferred_element_type=F32)
    y_ref[...] = _rms(h_ref[...] + gate * emb) * gf_ref[...]


def _ple(h, n, p, wg, wp, gf, *, tm=512):
    t, d = h.shape
    assert t % tm == 0
    return pl.pallas_call(
        _ple_kernel,
        out_shape=jax.ShapeDtypeStruct((t, d), F32),
        grid=(t // tm,),
        in_specs=[
            pl.BlockSpec((tm, d), lambda i: (i, 0)),
            pl.BlockSpec((tm, d), lambda i: (i, 0)),
            pl.BlockSpec((tm, D_PLE), lambda i: (i, 0)),
            _resident((d, d)),
            _resident((D_PLE, d)),
            _resident((1, d)),
        ],
        out_specs=pl.BlockSpec((tm, d), lambda i: (i, 0)),
        compiler_params=_params(("parallel",), 48),
        name="ple",
    )(h, n, p, wg, wp, gf)


CAST_BLOCK_BYTES = 4 * MIB


def _cast_kernel(w_ref, o_ref, *, scale):
    w = w_ref[...]
    o_ref[...] = (w if scale == 1.0 else w * scale).astype(BF16)


def _to_bf16(w, layer, scale=1.0):
    _, r, c = w.shape
    target = max(16, CAST_BLOCK_BYTES // (4 * c))
    rows = next(n for n in range(min(r, target) // 16 * 16, 0, -16) if r % n == 0)
    return pl.pallas_call(
        functools.partial(_cast_kernel, scale=scale),
        out_shape=jax.ShapeDtypeStruct((r, c), BF16),
        grid=(r // rows,),
        in_specs=[pl.BlockSpec((None, rows, c), lambda i: (layer, i, 0))],
        out_specs=pl.BlockSpec((rows, c), lambda i: (i, 0)),
        compiler_params=_params(("parallel",), 32),
        name="to_bf16",
    )(w)


def _rope_tables(seq):
    n_freq = HEAD_DIM // 4
    inv_freq = ROPE_THETA ** (-jnp.arange(n_freq, dtype=F32) / n_freq)
    t = jnp.arange(seq)
    row = (t // GRID_W).astype(F32)
    col = (t % GRID_W).astype(F32)
    ang = jnp.concatenate([row[:, None] * inv_freq[None], col[:, None] * inv_freq[None]], axis=-1)
    cos, sin = jnp.cos(ang), jnp.sin(ang)
    return jnp.concatenate([cos, cos], axis=-1), jnp.concatenate([-sin, sin], axis=-1)


_W_IN_SEGMENTS = dict(q_b=QKV_A_WIDTH, k_b=QKV_A_WIDTH + WIDTH_B, v_b=QKV_A_WIDTH + 2 * WIDTH_B,
                      g_a=QKV_A_WIDTH + 3 * WIDTH_B, g_b=QKV_A_WIDTH + 3 * WIDTH_B + D_MODEL)
_PROJ_LAYOUT = (("g_a", D_MODEL), ("g_b", D_MODEL), ("q_b", WIDTH_B), ("k_b", WIDTH_B), ("v_b", WIDTH_B))
_PROJ_SRC_COLS = tuple((_W_IN_SEGMENTS[name] + off) // PROJ_TN
                       for name, width in _PROJ_LAYOUT for off in range(0, width, PROJ_TN))
assert all(start % PROJ_TN == 0 for start in _W_IN_SEGMENTS.values())
_GA_COL, _GB_COL = 0, 1
_QB_COL, _KB_COL, _VB_COL = 4, 5, 6
assert 2 * D_MODEL == _QB_COL * WIDTH_B


def _prepare_weights(ffn1_norm, ffn1_w_gate, ffn1_w_up, ffn1_w_down, mix_norm, w_in, q_norm, k_norm,
                     w_branch_a, w_branch_b, w_out, ffn2_norm, ffn2_w_gate, ffn2_w_up, ffn2_w_down,
                     ple_norm, w_ple_gate, w_ple_proj, final_norm, layer):
    i = layer
    half = np.concatenate([np.arange(0, HEAD_DIM, 2), np.arange(1, HEAD_DIM, 2)])
    n_rot = N_HEADS_A + N_KV_A
    w = w_in[i]
    w_rot = w[:, :ROT_WIDTH].reshape(D_MODEL, n_rot, HEAD_DIM // 2, 2)
    w_rot = jnp.swapaxes(w_rot, 2, 3).reshape(D_MODEL, ROT_WIDTH).astype(BF16)
    q_scale = ATTN_SCALE * LOG2_E
    proj_scale = jnp.concatenate([jnp.full((width,), q_scale if name == "q_b" else 1.0, F32)
                                  for name, width in _PROJ_LAYOUT])[None]
    gain = jnp.concatenate([jnp.tile((q_norm[i] * q_scale)[half][None], (N_HEADS_A, 1)),
                            jnp.tile(k_norm[i][half][None], (N_KV_A, 1))])
    gqa_score_bound = (HEAD_DIM * q_scale * BF16_NORM_MARGIN
                       * jnp.max(jnp.abs(q_norm[i])) * jnp.max(jnp.abs(k_norm[i])))
    return dict(
        ffn1=(ffn1_norm[i][None], _to_bf16(ffn1_w_gate, i), _to_bf16(ffn1_w_up, i),
              _to_bf16(ffn1_w_down, i, FFN_RESIDUAL_SCALE), mix_norm[i][None]),
        w_rot=w_rot, gain=gain, w_in=_to_bf16(w_in, i), proj_scale=proj_scale, gqa_score_bound=gqa_score_bound,
        merge=(_to_bf16(w_branch_a, i), _to_bf16(w_branch_b, i), _to_bf16(w_out, i)),
        ffn2=(ffn2_norm[i][None], _to_bf16(ffn2_w_gate, i), _to_bf16(ffn2_w_up, i),
              _to_bf16(ffn2_w_down, i, FFN_RESIDUAL_SCALE), ple_norm[i][None]),
        ple=(_to_bf16(w_ple_gate, i), _to_bf16(w_ple_proj, i), final_norm[None]),
    )


def _encoder(x, p, wts, cos, sin, col_bias):
    b, s, d = x.shape
    t = b * s
    h1, u = _ffn(x.reshape(t, d), *wts["ffn1"])
    q_a, k_a, v_a = _qkv_a(u, wts["w_rot"], wts["w_in"], wts["gain"], cos, sin, s)
    z = _proj(u, wts["w_in"], _PROJ_SRC_COLS, wts["proj_scale"])
    qkv = (q_a.reshape(b, s, -1), k_a.reshape(b, s, -1), v_a.reshape(b, s, -1))
    o_a = lax.cond(wts["gqa_score_bound"] <= GQA_SAFE_LOG2_SCORE,
                   functools.partial(_gqa, bounded=True), functools.partial(_gqa, bounded=False), *qkv)
    o_b = _nbr(z.reshape(b, s, -1), col_bias, _QB_COL, _KB_COL, _VB_COL)
    h2 = _merge(o_a.reshape(t, -1), o_b.reshape(t, -1), z, _GA_COL, _GB_COL, h1, *wts["merge"])
    h3, n = _ffn(h2, *wts["ffn2"])
    y = _ple(h3, n, p.reshape(t, -1), *wts["ple"])
    return y.reshape(b, s, d)


def kernel(x_prompt, x_sample, p_prompt, p_sample, ffn1_norm, ffn1_w_gate, ffn1_w_up, ffn1_w_down, mix_norm, w_in, q_norm, k_norm, nat_rpb, w_branch_a, w_branch_b, w_out, ffn2_norm, ffn2_w_gate, ffn2_w_up, ffn2_w_down, ple_norm, w_ple_gate, w_ple_proj, final_norm):
    assert ffn1_norm.shape[0] == 1, "single-layer encoder"
    wts = _prepare_weights(ffn1_norm, ffn1_w_gate, ffn1_w_up, ffn1_w_down, mix_norm, w_in, q_norm, k_norm,
                           w_branch_a, w_branch_b, w_out, ffn2_norm, ffn2_w_gate, ffn2_w_up, ffn2_w_down,
                           ple_norm, w_ple_gate, w_ple_proj, final_norm, 0)
    col_bias = _nbr_col_bias(nat_rpb[0])
    tables = {}
    outs = []
    for x, p in ((x_prompt, p_prompt), (x_sample, p_sample)):
        seq = x.shape[1]
        if seq not in tables:
            tables[seq] = _rope_tables(seq)
        cos, sin = tables[seq]
        outs.append(_encoder(x, p[0], wts, cos, sin, col_bias))
    return tuple(outs)
```

```python
import functools
import math

import numpy as np
import jax
import jax.numpy as jnp
from jax import lax
from jax.experimental import pallas as pl
from jax.experimental.pallas import tpu as pltpu

F32 = jnp.float32
BF16 = jnp.bfloat16

D_MODEL = 2048
HEAD_DIM = 128
N_HEADS_A = 8
N_KV_A = 2
GQA_GROUP = N_HEADS_A // N_KV_A
N_HEADS_B = 8
WIDTH_A = N_HEADS_A * HEAD_DIM
KV_WIDTH_A = N_KV_A * HEAD_DIM
WIDTH_B = N_HEADS_B * HEAD_DIM
ROT_WIDTH = WIDTH_A + KV_WIDTH_A
QKV_A_WIDTH = ROT_WIDTH + KV_WIDTH_A
D_FF = 5632
D_PLE = 256
GRID_W = 64
WIN_ROWS = 8
WIN_COLS = 16
ROPE_THETA = 10000.0
EPS = 1e-6
ATTN_SCALE = HEAD_DIM ** -0.5
LOG2_E = math.log2(math.e)
MASKED = -1e30
GQA_SAFE_LOG2_SCORE = 48.0
BF16_NORM_MARGIN = 1.02

NBR_Q_ROWS = 4
NBR_KEY_ROWS = 12
NBR_Q = NBR_Q_ROWS * GRID_W
NBR_KEYS = NBR_KEY_ROWS * GRID_W
NBR_DR_PAD = (NBR_KEY_ROWS - NBR_Q_ROWS) - (WIN_ROWS - 1) + (NBR_Q_ROWS - 1)
NBR_DR_SLOTS = NBR_DR_PAD + (WIN_ROWS - 1) + (NBR_KEY_ROWS - 2) + 1

PROJ_TN = 512

V7X_VMEM_BYTES = 64 * 1024 * 1024
MIB = 1024 * 1024


def _params(semantics, vmem_mib):
    assert vmem_mib * MIB < V7X_VMEM_BYTES
    return pltpu.CompilerParams(dimension_semantics=semantics, vmem_limit_bytes=vmem_mib * MIB)


def _resident(shape):
    zeros = (0,) * len(shape)
    return pl.BlockSpec(shape, lambda *_: zeros, pipeline_mode=pl.Buffered(1))


def _rms(x):
    return x * lax.rsqrt(jnp.mean(x * x, axis=-1, keepdims=True) + EPS)


FFN_RESIDUAL_SCALE = 0.5
ROW_CHUNK = 16
ROW_UNROLL = 8

def _ffn_kernel(x_hbm, g1_ref, wg_ref, wu_ref, wd_ref, g2_ref, h_ref, n_ref, x_buf, xn_ref, x_sem):
    i = pl.program_id(0)
    j = pl.program_id(1)
    tm = x_buf.shape[0]

    def x_copy(tile):
        return pltpu.make_async_copy(x_hbm.at[pl.ds(tile * tm, tm), :], x_buf, x_sem)

    @pl.when(j == 0)
    def _():
        @pl.when(i == 0)
        def _():
            x_copy(0).start()

        x_copy(i).wait()

        def first(r, carry):
            rows = pl.ds(pl.multiple_of(r * ROW_CHUNK, ROW_CHUNK), ROW_CHUNK)
            x = x_buf[rows, :]
            xn_ref[rows, :] = (_rms(x) * g1_ref[...]).astype(BF16)
            h_ref[rows, :] = x
            return carry

        lax.fori_loop(0, tm // ROW_CHUNK, first, 0, unroll=ROW_UNROLL)

        @pl.when(i + 1 < pl.num_programs(0))
        def _():
            x_copy(i + 1).start()

    xn = xn_ref[...]
    g = jnp.dot(xn, wg_ref[...], preferred_element_type=F32)
    u = jnp.dot(xn, wu_ref[...], preferred_element_type=F32)
    a = (g * jax.nn.sigmoid(g) * u).astype(BF16)
    h_ref[...] += jnp.dot(a, wd_ref[...], preferred_element_type=F32)

    @pl.when(j == pl.num_programs(1) - 1)
    def _():
        def last(r, carry):
            rows = pl.ds(pl.multiple_of(r * ROW_CHUNK, ROW_CHUNK), ROW_CHUNK)
            n_ref[rows, :] = (_rms(h_ref[rows, :]) * g2_ref[...]).astype(BF16)
            return carry

        lax.fori_loop(0, tm // ROW_CHUNK, last, 0, unroll=ROW_UNROLL)


def _ffn(x, g1, wg, wu, wd, g2, *, tm=1024, tf=512):
    t, d = x.shape
    dff = wg.shape[1]
    assert t % tm == 0 and dff % tf == 0
    return pl.pallas_call(
        _ffn_kernel,
        out_shape=(jax.ShapeDtypeStruct((t, d), F32), jax.ShapeDtypeStruct((t, d), BF16)),
        grid=(t // tm, dff // tf),
        in_specs=[
            pl.BlockSpec(memory_space=pl.ANY),
            pl.BlockSpec((1, d), lambda i, j: (0, 0)),
            pl.BlockSpec((d, tf), lambda i, j: (0, j)),
            pl.BlockSpec((d, tf), lambda i, j: (0, j)),
            pl.BlockSpec((tf, d), lambda i, j: (j, 0)),
            pl.BlockSpec((1, d), lambda i, j: (0, 0)),
        ],
        out_specs=(
            pl.BlockSpec((tm, d), lambda i, j: (i, 0)),
            pl.BlockSpec((tm, d), lambda i, j: (i, 0)),
        ),
        scratch_shapes=[pltpu.VMEM((tm, d), F32), pltpu.VMEM((tm, d), BF16), pltpu.SemaphoreType.DMA(())],
        compiler_params=_params(("arbitrary", "arbitrary"), 60),
        name="ffn",
    )(x, g1, wg, wu, wd, g2)


def _permute_heads_kernel(w_ref, perm_ref, o_ref):
    o_ref[...] = jnp.dot(w_ref[...], perm_ref[...], preferred_element_type=F32).astype(BF16)


def _rotary_weight_layout(w_in):
    d = w_in.shape[0]
    half = np.concatenate([np.arange(0, HEAD_DIM, 2), np.arange(1, HEAD_DIM, 2)])
    perm = jnp.asarray(np.arange(HEAD_DIM)[:, None] == half[None, :], BF16)
    return pl.pallas_call(
        _permute_heads_kernel,
        out_shape=jax.ShapeDtypeStruct((d, ROT_WIDTH), BF16),
        grid=(ROT_WIDTH // HEAD_DIM,),
        in_specs=[pl.BlockSpec((d, HEAD_DIM), lambda h: (0, h)), _resident((HEAD_DIM, HEAD_DIM))],
        out_specs=pl.BlockSpec((d, HEAD_DIM), lambda h: (0, h)),
        compiler_params=_params(("parallel",), 16),
        name="rotary_weight_layout",
    )(w_in, perm)


def _qkv_a_kernel(u_ref, wr_ref, wv_ref, gain_ref, cos_ref, sin_ref, q_ref, k_ref, v_ref):
    u = u_ref[...]
    c = cos_ref[...]
    s = sin_ref[...]
    z = jnp.dot(u, wr_ref[...], preferred_element_type=F32)
    for hh in range(N_HEADS_A + N_KV_A):
        zh = z[:, hh * HEAD_DIM:(hh + 1) * HEAD_DIM]
        zn = _rms(zh) * gain_ref[hh:hh + 1, :]
        out = (zn * c + pltpu.roll(zn, HEAD_DIM // 2, 1) * s).astype(BF16)
        if hh < N_HEADS_A:
            q_ref[:, hh * HEAD_DIM:(hh + 1) * HEAD_DIM] = out
        else:
            kk = hh - N_HEADS_A
            k_ref[:, kk * HEAD_DIM:(kk + 1) * HEAD_DIM] = out
    v_ref[...] = jnp.dot(u, wv_ref[...], preferred_element_type=F32).astype(BF16)


def _qkv_a(u, w_rot, w_in, gain, cos, sin, seq, *, tm=512):
    t, d = u.shape
    assert t % tm == 0 and seq % tm == 0 and ROT_WIDTH % KV_WIDTH_A == 0
    seq_blocks = seq // tm
    return pl.pallas_call(
        _qkv_a_kernel,
        out_shape=(
            jax.ShapeDtypeStruct((t, WIDTH_A), BF16),
            jax.ShapeDtypeStruct((t, KV_WIDTH_A), BF16),
            jax.ShapeDtypeStruct((t, KV_WIDTH_A), BF16),
        ),
        grid=(t // tm,),
        in_specs=[
            pl.BlockSpec((tm, d), lambda i: (i, 0)),
            _resident((d, ROT_WIDTH)),
            pl.BlockSpec((d, KV_WIDTH_A), lambda i: (0, ROT_WIDTH // KV_WIDTH_A), pipeline_mode=pl.Buffered(1)),
            _resident((N_HEADS_A + N_KV_A, HEAD_DIM)),
            pl.BlockSpec((tm, HEAD_DIM), lambda i: (i % seq_blocks, 0)),
            pl.BlockSpec((tm, HEAD_DIM), lambda i: (i % seq_blocks, 0)),
        ],
        out_specs=(
            pl.BlockSpec((tm, WIDTH_A), lambda i: (i, 0)),
            pl.BlockSpec((tm, KV_WIDTH_A), lambda i: (i, 0)),
            pl.BlockSpec((tm, KV_WIDTH_A), lambda i: (i, 0)),
        ),
        compiler_params=_params(("parallel",), 40),
        name="qkv_a",
    )(u, w_rot, w_in, gain, cos, sin)


def _proj_kernel(u_ref, w_ref, scale_ref, z_ref):
    z = jnp.dot(u_ref[...], w_ref[...], preferred_element_type=F32)
    z_ref[...] = (z * scale_ref[...]).astype(BF16)


def _proj(u, w, src_cols, col_scale, *, tm=2048, tn=PROJ_TN):
    t, d = u.shape
    n = len(src_cols) * tn
    assert t % tm == 0 and col_scale.shape == (1, n)

    def w_block(i, j):
        src = src_cols[0]
        for k in range(1, len(src_cols)):
            src = jnp.where(j >= k, src_cols[k], src)
        return (0, src)

    return pl.pallas_call(
        _proj_kernel,
        out_shape=jax.ShapeDtypeStruct((t, n), BF16),
        grid=(t // tm, len(src_cols)),
        in_specs=[
            pl.BlockSpec((tm, d), lambda i, j: (i, 0)),
            pl.BlockSpec((d, tn), w_block),
            pl.BlockSpec((1, tn), lambda i, j: (0, j)),
        ],
        out_specs=pl.BlockSpec((tm, tn), lambda i, j: (i, j)),
        compiler_params=_params(("parallel", "arbitrary"), 40),
        name="proj",
    )(u, w, col_scale)


def _softmax_pv(s, v):
    m = jnp.max(s, axis=-1, keepdims=True)
    p = jnp.exp2(s - m)
    l = jnp.sum(p, axis=-1, keepdims=True)
    return jnp.dot(p.astype(BF16), v, preferred_element_type=F32) / l


_NT = (((1,), (1,)), ((), ()))


def _gqa_kernel(q_ref, k_ref, v_ref, o_ref, *, bounded):
    k = k_ref[0]
    v = v_ref[0]
    for g in range(GQA_GROUP):
        q = q_ref[0, :, g * HEAD_DIM:(g + 1) * HEAD_DIM]
        s = lax.dot_general(q, k, _NT, preferred_element_type=F32)
        if bounded:
            p = jnp.exp2(s)
            l = jnp.sum(p, axis=-1, keepdims=True)
            o = jnp.dot(p.astype(BF16), v, preferred_element_type=F32) / l
        else:
            o = _softmax_pv(s, v)
        o_ref[0, :, g * HEAD_DIM:(g + 1) * HEAD_DIM] = o.astype(BF16)


def _gqa(q, k, v, *, bounded, tq=512):
    b, s, _ = q.shape
    assert s % tq == 0
    gw = GQA_GROUP * HEAD_DIM
    return pl.pallas_call(
        functools.partial(_gqa_kernel, bounded=bounded),
        out_shape=jax.ShapeDtypeStruct((b, s, WIDTH_A), BF16),
        grid=(b, N_KV_A, s // tq),
        in_specs=[
            pl.BlockSpec((1, tq, gw), lambda bi, kh, qi: (bi, qi, kh)),
            pl.BlockSpec((1, s, HEAD_DIM), lambda bi, kh, qi: (bi, 0, kh)),
            pl.BlockSpec((1, s, HEAD_DIM), lambda bi, kh, qi: (bi, 0, kh)),
        ],
        out_specs=pl.BlockSpec((1, tq, gw), lambda bi, kh, qi: (bi, qi, kh)),
        compiler_params=_params(("parallel", "parallel", "arbitrary"), 40),
        name="gqa_bounded" if bounded else "gqa",
    )(q, k, v)


def _nbr_window_start(j, rows):
    return jnp.clip(NBR_Q_ROWS * j - WIN_ROWS // 2, 0, rows - NBR_KEY_ROWS)


def _nbr_block_plans(rows):
    plans = {}
    for j in range(rows // NBR_Q_ROWS):
        ws = int(np.clip(NBR_Q_ROWS * j - WIN_ROWS // 2, 0, rows - NBR_KEY_ROWS))
        lo = tuple(int(np.clip(NBR_Q_ROWS * j + qr - WIN_ROWS // 2, 0, rows - WIN_ROWS)) - ws
                   for qr in range(NBR_Q_ROWS))
        d0 = ws - NBR_Q_ROWS * j + (WIN_ROWS - 1) + NBR_DR_PAD
        assert min(lo) >= 0 and max(lo) + WIN_ROWS <= NBR_KEY_ROWS
        plans.setdefault((lo, d0), []).append(j)
    return plans


def _nbr_block(q_ref, k_ref, v_ref, cb_ref, o_ref, ws, lo, d0):
    pair_w = 2 * GRID_W
    pa = min(lo) // 2
    pb = (max(lo) + WIN_ROWS - 1) // 2 + 1
    n_keys = (pb - pa) * pair_w
    start = pl.multiple_of(ws * GRID_W + pa * pair_w, pair_w)
    lane = lax.broadcasted_iota(jnp.int32, (GRID_W, pair_w), 1)
    zeros = jnp.zeros((GRID_W, pair_w), BF16)
    heads = [slice(h * HEAD_DIM, (h + 1) * HEAD_DIM) for h in range(N_HEADS_B)]

    def score(cols):
        return lax.dot_general(q_ref[0, :, cols], k_ref[0, pl.ds(start, n_keys), cols], _NT,
                               preferred_element_type=F32)

    lookahead = len(set(lo)) > 1
    s_next = score(heads[0])
    for h, cols in enumerate(heads):
        s = s_next if lookahead or h == 0 else score(cols)
        p_rows, l_rows = [], []
        for qr in range(NBR_Q_ROWS):
            first, last = lo[qr], lo[qr] + WIN_ROWS - 1
            tiles = []
            for a in range(first // 2, last // 2 + 1):
                t = (s[qr * GRID_W:(qr + 1) * GRID_W, (a - pa) * pair_w:(a - pa + 1) * pair_w]
                     + cb_ref[h, d0 + 2 * a - qr])
                if 2 * a < first:
                    t = jnp.where(lane < GRID_W, MASKED, t)
                if 2 * a + 1 > last:
                    t = jnp.where(lane >= GRID_W, MASKED, t)
                tiles.append(t)
            m = jnp.max(functools.reduce(jnp.maximum, tiles), axis=-1, keepdims=True)
            ps = [jnp.exp2(t - m) for t in tiles]
            l_rows.append(jnp.sum(functools.reduce(jnp.add, ps), axis=-1, keepdims=True))
            p_rows.append(jnp.concatenate(
                [zeros] * (first // 2 - pa) + [p.astype(BF16) for p in ps] + [zeros] * (pb - 1 - last // 2),
                axis=1))
        p = jnp.concatenate(p_rows, axis=0)
        l = jnp.concatenate(l_rows, axis=0)
        if lookahead and h + 1 < N_HEADS_B:
            s_next = score(heads[h + 1])
        v = v_ref[0, pl.ds(start, n_keys), cols]
        o_ref[0, :, cols] = (jnp.dot(p, v, preferred_element_type=F32) / l).astype(BF16)


def _nbr_kernel(q_ref, k_ref, v_ref, cb_ref, o_ref, *, rows):
    j = pl.program_id(1)
    ws = _nbr_window_start(j, rows)
    for (lo, d0), members in _nbr_block_plans(rows).items():
        member = functools.reduce(jnp.logical_or, [j == m for m in members])
        pl.when(member)(functools.partial(_nbr_block, q_ref, k_ref, v_ref, cb_ref, o_ref, ws, lo, d0))


def _nbr(z, col_bias, q_col, k_col, v_col):
    b, s, _ = z.shape
    rows = s // GRID_W
    assert rows % NBR_Q_ROWS == 0 and rows >= NBR_KEY_ROWS
    return pl.pallas_call(
        functools.partial(_nbr_kernel, rows=rows),
        out_shape=jax.ShapeDtypeStruct((b, s, WIDTH_B), BF16),
        grid=(b, rows // NBR_Q_ROWS),
        in_specs=[
            pl.BlockSpec((1, NBR_Q, WIDTH_B), lambda bi, j: (bi, j, q_col)),
            pl.BlockSpec((1, s, WIDTH_B), lambda bi, j: (bi, 0, k_col)),
            pl.BlockSpec((1, s, WIDTH_B), lambda bi, j: (bi, 0, v_col)),
            _resident(col_bias.shape),
        ],
        out_specs=pl.BlockSpec((1, NBR_Q, WIDTH_B), lambda bi, j: (bi, j, 0)),
        compiler_params=_params(("parallel", "arbitrary"), 48),
        name="nbr",
    )(z, z, z, col_bias)


def _nbr_col_bias(rpb):
    n_dr, n_dc = 2 * WIN_ROWS - 1, 2 * WIN_COLS - 1
    c = np.arange(GRID_W)[:, None]
    kc = np.arange(GRID_W)[None, :]
    cs = np.clip(c - WIN_COLS // 2, 0, GRID_W - WIN_COLS)
    col_valid = (kc >= cs) & (kc < cs + WIN_COLS)
    onehot = ((kc - c + (WIN_COLS - 1))[None] == np.arange(n_dc)[:, None, None]) & col_valid[None]
    onehot = jnp.asarray(onehot.reshape(n_dc, GRID_W * GRID_W), F32)
    tm = jnp.einsum("hrd,dn->hrn", rpb * LOG2_E, onehot, precision=lax.Precision.HIGHEST)
    tm = jnp.where(col_valid.reshape(-1), tm, MASKED).reshape(-1, n_dr, GRID_W, GRID_W)
    hi_pad = NBR_DR_SLOTS + 1 - NBR_DR_PAD - n_dr
    tm = jnp.pad(tm, ((0, 0), (NBR_DR_PAD, hi_pad), (0, 0), (0, 0)), constant_values=MASKED)
    return jnp.concatenate([tm[:, :-1], tm[:, 1:]], axis=-1)


MERGE_TN = 512


def _merge_kernel(oa_ref, ob_ref, ga_ref, gb_ref, h_ref, wa_ref, wb_ref, wo_ref, out_ref, m_ref):
    oa = oa_ref[...]
    ob = ob_ref[...]
    for c in range(D_MODEL // MERGE_TN):
        cols = slice(c * MERGE_TN, (c + 1) * MERGE_TN)
        a = jnp.dot(oa, wa_ref[:, cols], preferred_element_type=F32)
        b = jnp.dot(ob, wb_ref[:, cols], preferred_element_type=F32)
        merged = (jax.nn.sigmoid(ga_ref[:, cols].astype(F32)) * a
                  + jax.nn.sigmoid(gb_ref[:, cols].astype(F32)) * b)
        m_ref[:, cols] = merged.astype(BF16)
    out_ref[...] = h_ref[...] + jnp.dot(m_ref[...], wo_ref[...], preferred_element_type=F32)


def _merge(oa, ob, z, ga_col, gb_col, h, wa, wb, wo, *, tm=512):
    t, d = h.shape
    assert t % tm == 0 and d == D_MODEL
    return pl.pallas_call(
        _merge_kernel,
        out_shape=jax.ShapeDtypeStruct((t, d), F32),
        grid=(t // tm,),
        in_specs=[
            pl.BlockSpec((tm, WIDTH_A), lambda i: (i, 0)),
            pl.BlockSpec((tm, WIDTH_B), lambda i: (i, 0)),
            pl.BlockSpec((tm, d), lambda i: (i, ga_col)),
            pl.BlockSpec((tm, d), lambda i: (i, gb_col)),
            pl.BlockSpec((tm, d), lambda i: (i, 0)),
            _resident((WIDTH_A, d)),
            _resident((WIDTH_B, d)),
            _resident((d, d)),
        ],
        out_specs=pl.BlockSpec((tm, d), lambda i: (i, 0)),
        scratch_shapes=[pltpu.VMEM((tm, d), BF16)],
        compiler_params=_params(("parallel",), 56),
        name="merge",
    )(oa, ob, z, z, h, wa, wb, wo)


def _ple_kernel(h_ref, n_ref, p_ref, wg_ref, wp_ref, gf_ref, y_ref):
    gate = jax.nn.sigmoid(jnp.dot(n_ref[...], wg_ref[...], preferred_element_type=F32))
    emb = jnp.dot(p_ref[...].astype(BF16), wp_ref[...], preferred_element_type=F32)
    y_ref[...] = _rms(h_ref[...] + gate * emb) * gf_ref[...]


def _ple(h, n, p, wg, wp, gf, *, tm=512):
    t, d = h.shape
    assert t % tm == 0
    return pl.pallas_call(
        _ple_kernel,
        out_shape=jax.ShapeDtypeStruct((t, d), F32),
        grid=(t // tm,),
        in_specs=[
            pl.BlockSpec((tm, d), lambda i: (i, 0)),
            pl.BlockSpec((tm, d), lambda i: (i, 0)),
            pl.BlockSpec((tm, D_PLE), lambda i: (i, 0)),
            _resident((d, d)),
            _resident((D_PLE, d)),
            _resident((1, d)),
        ],
        out_specs=pl.BlockSpec((tm, d), lambda i: (i, 0)),
        compiler_params=_params(("parallel",), 48),
        name="ple",
    )(h, n, p, wg, wp, gf)


CAST_BLOCK_BYTES = 4 * MIB


def _cast_kernel(w_ref, o_ref, *, scale):
    w = w_ref[...]
    o_ref[...] = (w if scale == 1.0 else w * scale).astype(BF16)


def _to_bf16(w, layer, scale=1.0):
    _, r, c = w.shape
    target = max(16, CAST_BLOCK_BYTES // (4 * c))
    rows = next(n for n in range(min(r, target) // 16 * 16, 0, -16) if r % n == 0)
    return pl.pallas_call(
        functools.partial(_cast_kernel, scale=scale),
        out_shape=jax.ShapeDtypeStruct((r, c), BF16),
        grid=(r // rows,),
        in_specs=[pl.BlockSpec((None, rows, c), lambda i: (layer, i, 0))],
        out_specs=pl.BlockSpec((rows, c), lambda i: (i, 0)),
        compiler_params=_params(("parallel",), 32),
        name="to_bf16",
    )(w)


def _rope_tables(seq):
    n_freq = HEAD_DIM // 4
    inv_freq = ROPE_THETA ** (-jnp.arange(n_freq, dtype=F32) / n_freq)
    t = jnp.arange(seq)
    row = (t // GRID_W).astype(F32)
    col = (t % GRID_W).astype(F32)
    ang = jnp.concatenate([row[:, None] * inv_freq[None], col[:, None] * inv_freq[None]], axis=-1)
    cos, sin = jnp.cos(ang), jnp.sin(ang)
    return jnp.concatenate([cos, cos], axis=-1), jnp.concatenate([-sin, sin], axis=-1)


_W_IN_SEGMENTS = dict(q_b=QKV_A_WIDTH, k_b=QKV_A_WIDTH + WIDTH_B, v_b=QKV_A_WIDTH + 2 * WIDTH_B,
                      g_a=QKV_A_WIDTH + 3 * WIDTH_B, g_b=QKV_A_WIDTH + 3 * WIDTH_B + D_MODEL)
_PROJ_LAYOUT = (("g_a", D_MODEL), ("g_b", D_MODEL), ("q_b", WIDTH_B), ("k_b", WIDTH_B), ("v_b", WIDTH_B))
_PROJ_SRC_COLS = tuple((_W_IN_SEGMENTS[name] + off) // PROJ_TN
                       for name, width in _PROJ_LAYOUT for off in range(0, width, PROJ_TN))
assert all(start % PROJ_TN == 0 for start in _W_IN_SEGMENTS.values())
_GA_COL, _GB_COL = 0, 1
_QB_COL, _KB_COL, _VB_COL = 4, 5, 6
assert 2 * D_MODEL == _QB_COL * WIDTH_B


def _prepare_weights(ffn1_norm, ffn1_w_gate, ffn1_w_up, ffn1_w_down, mix_norm, w_in, q_norm, k_norm,
                     w_branch_a, w_branch_b, w_out, ffn2_norm, ffn2_w_gate, ffn2_w_up, ffn2_w_down,
                     ple_norm, w_ple_gate, w_ple_proj, final_norm, layer):
    i = layer
    half = np.concatenate([np.arange(0, HEAD_DIM, 2), np.arange(1, HEAD_DIM, 2)])
    w_in_bf16 = _to_bf16(w_in, i)
    w_rot = _rotary_weight_layout(w_in_bf16)
    q_scale = ATTN_SCALE * LOG2_E
    proj_scale = jnp.concatenate([jnp.full((width,), q_scale if name == "q_b" else 1.0, F32)
                                  for name, width in _PROJ_LAYOUT])[None]
    gain = jnp.concatenate([jnp.tile((q_norm[i] * q_scale)[half][None], (N_HEADS_A, 1)),
                            jnp.tile(k_norm[i][half][None], (N_KV_A, 1))])
    gqa_score_bound = (HEAD_DIM * q_scale * BF16_NORM_MARGIN
                       * jnp.max(jnp.abs(q_norm[i])) * jnp.max(jnp.abs(k_norm[i])))
    return dict(
        ffn1=(ffn1_norm[i][None], _to_bf16(ffn1_w_gate, i), _to_bf16(ffn1_w_up, i),
              _to_bf16(ffn1_w_down, i, FFN_RESIDUAL_SCALE), mix_norm[i][None]),
        w_rot=w_rot, gain=gain, w_in=w_in_bf16, proj_scale=proj_scale, gqa_score_bound=gqa_score_bound,
        merge=(_to_bf16(w_branch_a, i), _to_bf16(w_branch_b, i), _to_bf16(w_out, i)),
        ffn2=(ffn2_norm[i][None], _to_bf16(ffn2_w_gate, i), _to_bf16(ffn2_w_up, i),
              _to_bf16(ffn2_w_down, i, FFN_RESIDUAL_SCALE), ple_norm[i][None]),
        ple=(_to_bf16(w_ple_gate, i), _to_bf16(w_ple_proj, i), final_norm[None]),
    )


def _encoder(x, p, wts, cos, sin, col_bias):
    b, s, d = x.shape
    t = b * s
    h1, u = _ffn(x.reshape(t, d), *wts["ffn1"])
    q_a, k_a, v_a = _qkv_a(u, wts["w_rot"], wts["w_in"], wts["gain"], cos, sin, s)
    z = _proj(u, wts["w_in"], _PROJ_SRC_COLS, wts["proj_scale"])
    qkv = (q_a.reshape(b, s, -1), k_a.reshape(b, s, -1), v_a.reshape(b, s, -1))
    o_a = lax.cond(wts["gqa_score_bound"] <= GQA_SAFE_LOG2_SCORE,
                   functools.partial(_gqa, bounded=True), functools.partial(_gqa, bounded=False), *qkv)
    o_b = _nbr(z.reshape(b, s, -1), col_bias, _QB_COL, _KB_COL, _VB_COL)
    h2 = _merge(o_a.reshape(t, -1), o_b.reshape(t, -1), z, _GA_COL, _GB_COL, h1, *wts["merge"])
    h3, n = _ffn(h2, *wts["ffn2"])
    y = _ple(h3, n, p.reshape(t, -1), *wts["ple"])
    return y.reshape(b, s, d)


def kernel(x_prompt, x_sample, p_prompt, p_sample, ffn1_norm, ffn1_w_gate, ffn1_w_up, ffn1_w_down, mix_norm, w_in, q_norm, k_norm, nat_rpb, w_branch_a, w_branch_b, w_out, ffn2_norm, ffn2_w_gate, ffn2_w_up, ffn2_w_down, ple_norm, w_ple_gate, w_ple_proj, final_norm):
    assert ffn1_norm.shape[0] == 1, "single-layer encoder"
    wts = _prepare_weights(ffn1_norm, ffn1_w_gate, ffn1_w_up, ffn1_w_down, mix_norm, w_in, q_norm, k_norm,
                           w_branch_a, w_branch_b, w_out, ffn2_norm, ffn2_w_gate, ffn2_w_up, ffn2_w_down,
                           ple_norm, w_ple_gate, w_ple_proj, final_norm, 0)
    col_bias = _nbr_col_bias(nat_rpb[0])
    tables = {}
    outs = []
    for x, p in ((x_prompt, p_prompt), (x_sample, p_sample)):
        seq = x.shape[1]
        if seq not in tables:
            tables[seq] = _rope_tables(seq)
        cos, sin = tables[seq]
        outs.append(_encoder(x, p[0], wts, cos, sin, col_bias))
    return tuple(outs)
```

```python
import functools
import math

import numpy as np
import jax
import jax.numpy as jnp
from jax import lax
from jax.experimental import pallas as pl
from jax.experimental.pallas import tpu as pltpu

F32 = jnp.float32
BF16 = jnp.bfloat16

D_MODEL = 2048
HEAD_DIM = 128
N_HEADS_A = 8
N_KV_A = 2
GQA_GROUP = N_HEADS_A // N_KV_A
N_HEADS_B = 8
WIDTH_A = N_HEADS_A * HEAD_DIM
KV_WIDTH_A = N_KV_A * HEAD_DIM
WIDTH_B = N_HEADS_B * HEAD_DIM
ROT_WIDTH = WIDTH_A + KV_WIDTH_A
QKV_A_WIDTH = ROT_WIDTH + KV_WIDTH_A
D_FF = 5632
D_PLE = 256
GRID_W = 64
WIN_ROWS = 8
WIN_COLS = 16
ROPE_THETA = 10000.0
EPS = 1e-6
ATTN_SCALE = HEAD_DIM ** -0.5
LOG2_E = math.log2(math.e)
MASKED = -1e30
GQA_SAFE_LOG2_SCORE = 48.0
BF16_NORM_MARGIN = 1.02

NBR_Q_ROWS = 4
NBR_KEY_ROWS = 12
NBR_Q = NBR_Q_ROWS * GRID_W
NBR_KEYS = NBR_KEY_ROWS * GRID_W
NBR_DR_PAD = (NBR_KEY_ROWS - NBR_Q_ROWS) - (WIN_ROWS - 1) + (NBR_Q_ROWS - 1)
NBR_DR_SLOTS = NBR_DR_PAD + (WIN_ROWS - 1) + (NBR_KEY_ROWS - 2) + 1

PROJ_TN = 512

V7X_VMEM_BYTES = 64 * 1024 * 1024
MIB = 1024 * 1024


def _params(semantics, vmem_mib):
    assert vmem_mib * MIB < V7X_VMEM_BYTES
    return pltpu.CompilerParams(dimension_semantics=semantics, vmem_limit_bytes=vmem_mib * MIB)


def _resident(shape):
    zeros = (0,) * len(shape)
    return pl.BlockSpec(shape, lambda *_: zeros, pipeline_mode=pl.Buffered(1))


def _rms(x):
    return x * lax.rsqrt(jnp.mean(x * x, axis=-1, keepdims=True) + EPS)


FFN_RESIDUAL_SCALE = 0.5
FFN_TF = 512
ROW_CHUNK = 16
ROW_UNROLL = 8

def _ffn_kernel(x_hbm, g1_ref, wgu_ref, wd_ref, g2_ref, h_ref, n_ref, x_buf, xn_ref, x_sem):
    i = pl.program_id(0)
    j = pl.program_id(1)
    tm = x_buf.shape[0]

    def x_copy(tile):
        return pltpu.make_async_copy(x_hbm.at[pl.ds(tile * tm, tm), :], x_buf, x_sem)

    @pl.when(j == 0)
    def _():
        @pl.when(i == 0)
        def _():
            x_copy(0).start()

        x_copy(i).wait()

        def first(r, carry):
            rows = pl.ds(pl.multiple_of(r * ROW_CHUNK, ROW_CHUNK), ROW_CHUNK)
            x = x_buf[rows, :]
            xn_ref[rows, :] = (_rms(x) * g1_ref[...]).astype(BF16)
            h_ref[rows, :] = x
            return carry

        lax.fori_loop(0, tm // ROW_CHUNK, first, 0, unroll=ROW_UNROLL)

        @pl.when(i + 1 < pl.num_programs(0))
        def _():
            x_copy(i + 1).start()

    tf = wd_ref.shape[0]
    gu = jnp.dot(xn_ref[...], wgu_ref[...], preferred_element_type=F32)
    g, u = gu[:, :tf], gu[:, tf:]
    a = (g * jax.nn.sigmoid(g) * u).astype(BF16)
    h_ref[...] += jnp.dot(a, wd_ref[...], preferred_element_type=F32)

    @pl.when(j == pl.num_programs(1) - 1)
    def _():
        def last(r, carry):
            rows = pl.ds(pl.multiple_of(r * ROW_CHUNK, ROW_CHUNK), ROW_CHUNK)
            n_ref[rows, :] = (_rms(h_ref[rows, :]) * g2_ref[...]).astype(BF16)
            return carry

        lax.fori_loop(0, tm // ROW_CHUNK, last, 0, unroll=ROW_UNROLL)


def _ffn(x, g1, w_gate_up, slot, wd, g2, *, tm=1024):
    t, d = x.shape
    dff, tf = wd.shape[0], FFN_TF
    assert t % tm == 0 and w_gate_up.shape[1:] == (dff // tf, d, 2 * tf)
    return pl.pallas_call(
        _ffn_kernel,
        out_shape=(jax.ShapeDtypeStruct((t, d), F32), jax.ShapeDtypeStruct((t, d), BF16)),
        grid=(t // tm, dff // tf),
        in_specs=[
            pl.BlockSpec(memory_space=pl.ANY),
            pl.BlockSpec((1, d), lambda i, j: (0, 0)),
            pl.BlockSpec((None, None, d, 2 * tf), lambda i, j: (slot, j, 0, 0)),
            pl.BlockSpec((tf, d), lambda i, j: (j, 0)),
            pl.BlockSpec((1, d), lambda i, j: (0, 0)),
        ],
        out_specs=(
            pl.BlockSpec((tm, d), lambda i, j: (i, 0)),
            pl.BlockSpec((tm, d), lambda i, j: (i, 0)),
        ),
        scratch_shapes=[pltpu.VMEM((tm, d), F32), pltpu.VMEM((tm, d), BF16), pltpu.SemaphoreType.DMA(())],
        compiler_params=_params(("arbitrary", "arbitrary"), 60),
        name="ffn",
    )(x, g1, w_gate_up, wd, g2)


def _permute_heads_kernel(w_ref, perm_ref, o_ref):
    o_ref[...] = jnp.dot(w_ref[...], perm_ref[...], preferred_element_type=F32).astype(BF16)


def _rotary_weight_layout(w_in):
    d = w_in.shape[0]
    half = np.concatenate([np.arange(0, HEAD_DIM, 2), np.arange(1, HEAD_DIM, 2)])
    perm = jnp.asarray(np.arange(HEAD_DIM)[:, None] == half[None, :], BF16)
    return pl.pallas_call(
        _permute_heads_kernel,
        out_shape=jax.ShapeDtypeStruct((d, ROT_WIDTH), BF16),
        grid=(ROT_WIDTH // HEAD_DIM,),
        in_specs=[pl.BlockSpec((d, HEAD_DIM), lambda h: (0, h)), _resident((HEAD_DIM, HEAD_DIM))],
        out_specs=pl.BlockSpec((d, HEAD_DIM), lambda h: (0, h)),
        compiler_params=_params(("parallel",), 16),
        name="rotary_weight_layout",
    )(w_in, perm)


def _qkv_a_kernel(u_ref, wr_ref, wv_ref, rot_ref, q_ref, k_ref, v_ref):
    u = u_ref[...]
    pair_w = 2 * HEAD_DIM

    def project(pair):
        return jnp.dot(u, wr_ref[:, pair * pair_w:(pair + 1) * pair_w], preferred_element_type=F32)

    z_next = project(0)
    for pair in range((N_HEADS_A + N_KV_A) // 2):
        z = z_next
        if 2 * (pair + 1) < N_HEADS_A + N_KV_A:
            z_next = project(pair + 1)
        for sub in range(2):
            hh = 2 * pair + sub
            zh = z[:, sub * HEAD_DIM:(sub + 1) * HEAD_DIM]
            tab = 0 if hh < N_HEADS_A else 2
            cos_g = rot_ref[:, tab * HEAD_DIM:(tab + 1) * HEAD_DIM]
            sin_g = rot_ref[:, (tab + 1) * HEAD_DIM:(tab + 2) * HEAD_DIM]
            r = lax.rsqrt(jnp.mean(zh * zh, axis=-1, keepdims=True) + EPS)
            out = ((zh * cos_g + pltpu.roll(zh, HEAD_DIM // 2, 1) * sin_g) * r).astype(BF16)
            if hh < N_HEADS_A:
                q_ref[:, hh * HEAD_DIM:(hh + 1) * HEAD_DIM] = out
            else:
                kk = hh - N_HEADS_A
                k_ref[:, kk * HEAD_DIM:(kk + 1) * HEAD_DIM] = out
    v_ref[...] = jnp.dot(u, wv_ref[...], preferred_element_type=F32).astype(BF16)


def _qkv_a(u, w_rot, w_in, rot, seq, *, tm=512):
    t, d = u.shape
    assert t % tm == 0 and seq % tm == 0 and ROT_WIDTH % KV_WIDTH_A == 0
    assert N_HEADS_A % 2 == 0 and N_KV_A % 2 == 0
    seq_blocks = seq // tm
    return pl.pallas_call(
        _qkv_a_kernel,
        out_shape=(
            jax.ShapeDtypeStruct((t, WIDTH_A), BF16),
            jax.ShapeDtypeStruct((t, KV_WIDTH_A), BF16),
            jax.ShapeDtypeStruct((t, KV_WIDTH_A), BF16),
        ),
        grid=(t // tm,),
        in_specs=[
            pl.BlockSpec((tm, d), lambda i: (i, 0)),
            _resident((d, ROT_WIDTH)),
            pl.BlockSpec((d, KV_WIDTH_A), lambda i: (0, ROT_WIDTH // KV_WIDTH_A), pipeline_mode=pl.Buffered(1)),
            pl.BlockSpec((tm, 4 * HEAD_DIM), lambda i: (i % seq_blocks, 0)),
        ],
        out_specs=(
            pl.BlockSpec((tm, WIDTH_A), lambda i: (i, 0)),
            pl.BlockSpec((tm, KV_WIDTH_A), lambda i: (i, 0)),
            pl.BlockSpec((tm, KV_WIDTH_A), lambda i: (i, 0)),
        ),
        compiler_params=_params(("parallel",), 40),
        name="qkv_a",
    )(u, w_rot, w_in, rot)


def _proj_kernel(u_ref, w_ref, scale_ref, z_ref):
    z = jnp.dot(u_ref[...], w_ref[...], preferred_element_type=F32)
    z_ref[...] = (z * scale_ref[...]).astype(BF16)


def _proj(u, w, src_cols, col_scale, *, tm=2048, tn=PROJ_TN):
    t, d = u.shape
    n = len(src_cols) * tn
    assert t % tm == 0 and col_scale.shape == (1, n)

    def w_block(i, j):
        src = src_cols[0]
        for k in range(1, len(src_cols)):
            src = jnp.where(j >= k, src_cols[k], src)
        return (0, src)

    return pl.pallas_call(
        _proj_kernel,
        out_shape=jax.ShapeDtypeStruct((t, n), BF16),
        grid=(t // tm, len(src_cols)),
        in_specs=[
            pl.BlockSpec((tm, d), lambda i, j: (i, 0)),
            pl.BlockSpec((d, tn), w_block),
            pl.BlockSpec((1, tn), lambda i, j: (0, j)),
        ],
        out_specs=pl.BlockSpec((tm, tn), lambda i, j: (i, j)),
        compiler_params=_params(("parallel", "arbitrary"), 40),
        name="proj",
    )(u, w, col_scale)


def _softmax_pv(s, v):
    m = jnp.max(s, axis=-1, keepdims=True)
    p = jnp.exp2(s - m)
    l = jnp.sum(p, axis=-1, keepdims=True)
    return jnp.dot(p.astype(BF16), v, preferred_element_type=F32) / l


_NT = (((1,), (1,)), ((), ()))


def _gqa_kernel(q_ref, k_ref, v_ref, o_ref, *, bounded):
    k = k_ref[0]
    v = v_ref[0]

    def score(g):
        return lax.dot_general(q_ref[0, :, g * HEAD_DIM:(g + 1) * HEAD_DIM], k, _NT, preferred_element_type=F32)

    s_next = score(0)
    for g in range(GQA_GROUP):
        s = s_next
        if g + 1 < GQA_GROUP:
            s_next = score(g + 1)
        if bounded:
            p = jnp.exp2(s)
            l = jnp.sum(p, axis=-1, keepdims=True)
            o = jnp.dot(p.astype(BF16), v, preferred_element_type=F32) / l
        else:
            o = _softmax_pv(s, v)
        o_ref[0, :, g * HEAD_DIM:(g + 1) * HEAD_DIM] = o.astype(BF16)


def _gqa(q, k, v, *, bounded, tq=512):
    b, s, _ = q.shape
    assert s % tq == 0
    gw = GQA_GROUP * HEAD_DIM
    return pl.pallas_call(
        functools.partial(_gqa_kernel, bounded=bounded),
        out_shape=jax.ShapeDtypeStruct((b, s, WIDTH_A), BF16),
        grid=(b, N_KV_A, s // tq),
        in_specs=[
            pl.BlockSpec((1, tq, gw), lambda bi, kh, qi: (bi, qi, kh)),
            pl.BlockSpec((1, s, HEAD_DIM), lambda bi, kh, qi: (bi, 0, kh)),
            pl.BlockSpec((1, s, HEAD_DIM), lambda bi, kh, qi: (bi, 0, kh)),
        ],
        out_specs=pl.BlockSpec((1, tq, gw), lambda bi, kh, qi: (bi, qi, kh)),
        compiler_params=_params(("parallel", "parallel", "arbitrary"), 40),
        name="gqa_bounded" if bounded else "gqa",
    )(q, k, v)


def _nbr_window_start(j, rows):
    return jnp.clip(NBR_Q_ROWS * j - WIN_ROWS // 2, 0, rows - NBR_KEY_ROWS)


def _nbr_block_plans(rows):
    plans = {}
    for j in range(rows // NBR_Q_ROWS):
        ws = int(np.clip(NBR_Q_ROWS * j - WIN_ROWS // 2, 0, rows - NBR_KEY_ROWS))
        lo = tuple(int(np.clip(NBR_Q_ROWS * j + qr - WIN_ROWS // 2, 0, rows - WIN_ROWS)) - ws
                   for qr in range(NBR_Q_ROWS))
        d0 = ws - NBR_Q_ROWS * j + (WIN_ROWS - 1) + NBR_DR_PAD
        assert min(lo) >= 0 and max(lo) + WIN_ROWS <= NBR_KEY_ROWS
        plans.setdefault((lo, d0), []).append(j)
    return plans


def _nbr_block(q_ref, k_ref, v_ref, cb_ref, o_ref, ws, lo, d0):
    pair_w = 2 * GRID_W
    pa = min(lo) // 2
    pb = (max(lo) + WIN_ROWS - 1) // 2 + 1
    n_keys = (pb - pa) * pair_w
    start = pl.multiple_of(ws * GRID_W + pa * pair_w, pair_w)
    lane = lax.broadcasted_iota(jnp.int32, (GRID_W, pair_w), 1)
    zeros = jnp.zeros((GRID_W, pair_w), BF16)
    heads = [slice(h * HEAD_DIM, (h + 1) * HEAD_DIM) for h in range(N_HEADS_B)]

    def score(cols):
        return lax.dot_general(q_ref[0, :, cols], k_ref[0, pl.ds(start, n_keys), cols], _NT,
                               preferred_element_type=F32)

    lookahead = len(set(lo)) > 1
    s_next = score(heads[0])
    for h, cols in enumerate(heads):
        s = s_next if lookahead or h == 0 else score(cols)
        p_rows, l_rows = [], []
        for qr in range(NBR_Q_ROWS):
            first, last = lo[qr], lo[qr] + WIN_ROWS - 1
            tiles = []
            for a in range(first // 2, last // 2 + 1):
                t = (s[qr * GRID_W:(qr + 1) * GRID_W, (a - pa) * pair_w:(a - pa + 1) * pair_w]
                     + cb_ref[h, d0 + 2 * a - qr])
                if 2 * a < first:
                    t = jnp.where(lane < GRID_W, MASKED, t)
                if 2 * a + 1 > last:
                    t = jnp.where(lane >= GRID_W, MASKED, t)
                tiles.append(t)
            m = jnp.max(functools.reduce(jnp.maximum, tiles), axis=-1, keepdims=True)
            ps = [jnp.exp2(t - m) for t in tiles]
            l_rows.append(jnp.sum(functools.reduce(jnp.add, ps), axis=-1, keepdims=True))
            p_rows.append(jnp.concatenate(
                [zeros] * (first // 2 - pa) + [p.astype(BF16) for p in ps] + [zeros] * (pb - 1 - last // 2),
                axis=1))
        p = jnp.concatenate(p_rows, axis=0)
        l = jnp.concatenate(l_rows, axis=0)
        if lookahead and h + 1 < N_HEADS_B:
            s_next = score(heads[h + 1])
        v = v_ref[0, pl.ds(start, n_keys), cols]
        o_ref[0, :, cols] = (jnp.dot(p, v, preferred_element_type=F32) / l).astype(BF16)


def _nbr_kernel(q_ref, k_ref, v_ref, cb_ref, o_ref, *, rows):
    j = pl.program_id(1)
    ws = _nbr_window_start(j, rows)
    for (lo, d0), members in _nbr_block_plans(rows).items():
        member = functools.reduce(jnp.logical_or, [j == m for m in members])
        pl.when(member)(functools.partial(_nbr_block, q_ref, k_ref, v_ref, cb_ref, o_ref, ws, lo, d0))


def _nbr(z, col_bias, q_col, k_col, v_col):
    b, s, _ = z.shape
    rows = s // GRID_W
    assert rows % NBR_Q_ROWS == 0 and rows >= NBR_KEY_ROWS
    return pl.pallas_call(
        functools.partial(_nbr_kernel, rows=rows),
        out_shape=jax.ShapeDtypeStruct((b, s, WIDTH_B), BF16),
        grid=(b, rows // NBR_Q_ROWS),
        in_specs=[
            pl.BlockSpec((1, NBR_Q, WIDTH_B), lambda bi, j: (bi, j, q_col)),
            pl.BlockSpec((1, s, WIDTH_B), lambda bi, j: (bi, 0, k_col)),
            pl.BlockSpec((1, s, WIDTH_B), lambda bi, j: (bi, 0, v_col)),
            _resident(col_bias.shape),
        ],
        out_specs=pl.BlockSpec((1, NBR_Q, WIDTH_B), lambda bi, j: (bi, j, 0)),
        compiler_params=_params(("parallel", "arbitrary"), 48),
        name="nbr",
    )(z, z, z, col_bias)


def _nbr_col_bias(rpb):
    n_dr, n_dc = 2 * WIN_ROWS - 1, 2 * WIN_COLS - 1
    c = np.arange(GRID_W)[:, None]
    kc = np.arange(GRID_W)[None, :]
    cs = np.clip(c - WIN_COLS // 2, 0, GRID_W - WIN_COLS)
    col_valid = (kc >= cs) & (kc < cs + WIN_COLS)
    onehot = ((kc - c + (WIN_COLS - 1))[None] == np.arange(n_dc)[:, None, None]) & col_valid[None]
    onehot = jnp.asarray(onehot.reshape(n_dc, GRID_W * GRID_W), F32)
    tm = jnp.einsum("hrd,dn->hrn", rpb * LOG2_E, onehot, precision=lax.Precision.HIGHEST)
    tm = jnp.where(col_valid.reshape(-1), tm, MASKED).reshape(-1, n_dr, GRID_W, GRID_W)
    hi_pad = NBR_DR_SLOTS + 1 - NBR_DR_PAD - n_dr
    tm = jnp.pad(tm, ((0, 0), (NBR_DR_PAD, hi_pad), (0, 0), (0, 0)), constant_values=MASKED)
    return jnp.concatenate([tm[:, :-1], tm[:, 1:]], axis=-1)


MERGE_TN = 512


def _merge_kernel(oa_ref, ob_ref, ga_ref, gb_ref, h_ref, wa_ref, wb_ref, wo_ref, out_ref, m_ref):
    oa = oa_ref[...]
    ob = ob_ref[...]
    for c in range(D_MODEL // MERGE_TN):
        cols = slice(c * MERGE_TN, (c + 1) * MERGE_TN)
        a = jnp.dot(oa, wa_ref[:, cols], preferred_element_type=F32)
        b = jnp.dot(ob, wb_ref[:, cols], preferred_element_type=F32)
        merged = (jax.nn.sigmoid(ga_ref[:, cols].astype(F32)) * a
                  + jax.nn.sigmoid(gb_ref[:, cols].astype(F32)) * b)
        m_ref[:, cols] = merged.astype(BF16)
    out_ref[...] = h_ref[...] + jnp.dot(m_ref[...], wo_ref[...], preferred_element_type=F32)


def _merge(oa, ob, z, ga_col, gb_col, h, wa, wb, wo, *, tm=512):
    t, d = h.shape
    assert t % tm == 0 and d == D_MODEL
    return pl.pallas_call(
        _merge_kernel,
        out_shape=jax.ShapeDtypeStruct((t, d), F32),
        grid=(t // tm,),
        in_specs=[
            pl.BlockSpec((tm, WIDTH_A), lambda i: (i, 0)),
            pl.BlockSpec((tm, WIDTH_B), lambda i: (i, 0)),
            pl.BlockSpec((tm, d), lambda i: (i, ga_col)),
            pl.BlockSpec((tm, d), lambda i: (i, gb_col)),
            pl.BlockSpec((tm, d), lambda i: (i, 0)),
            _resident((WIDTH_A, d)),
            _resident((WIDTH_B, d)),
            _resident((d, d)),
        ],
        out_specs=pl.BlockSpec((tm, d), lambda i: (i, 0)),
        scratch_shapes=[pltpu.VMEM((tm, d), BF16)],
        compiler_params=_params(("parallel",), 56),
        name="merge",
    )(oa, ob, z, z, h, wa, wb, wo)


def _ple_kernel(h_ref, n_ref, p_ref, wg_ref, wp_ref, gf_ref, y_ref):
    gate = jax.nn.sigmoid(jnp.dot(n_ref[...], wg_ref[...], preferred_element_type=F32))
    emb = jnp.dot(p_ref[...].astype(BF16), wp_ref[...], preferred_element_type=F32)
    y_ref[...] = _rms(h_ref[...] + gate * emb) * gf_ref[...]


def _ple(h, n, p, wg, wp, gf, *, tm=512):
    t, d = h.shape
    assert t % tm == 0
    return pl.pallas_call(
        _ple_kernel,
        out_shape=jax.ShapeDtypeStruct((t, d), F32),
        grid=(t // tm,),
        in_specs=[
            pl.BlockSpec((tm, d), lambda i: (i, 0)),
            pl.BlockSpec((tm, d), lambda i: (i, 0)),
            pl.BlockSpec((tm, D_PLE), lambda i: (i, 0)),
            _resident((d, d)),
            _resident((D_PLE, d)),
            _resident((1, d)),
        ],
        out_specs=pl.BlockSpec((tm, d), lambda i: (i, 0)),
        compiler_params=_params(("parallel",), 48),
        name="ple",
    )(h, n, p, wg, wp, gf)


CAST_BLOCK_BYTES = 4 * MIB


def _cast_kernel(w_ref, o_ref, *, scale):
    w = w_ref[...]
    o_ref[...] = (w if scale == 1.0 else w * scale).astype(BF16)


def _to_bf16(w, layer, scale=1.0):
    _, r, c = w.shape
    target = max(16, CAST_BLOCK_BYTES // (4 * c))
    rows = next(n for n in range(min(r, target) // 16 * 16, 0, -16) if r % n == 0)
    return pl.pallas_call(
        functools.partial(_cast_kernel, scale=scale),
        out_shape=jax.ShapeDtypeStruct((r, c), BF16),
        grid=(r // rows,),
        in_specs=[pl.BlockSpec((None, rows, c), lambda i: (layer, i, 0))],
        out_specs=pl.BlockSpec((rows, c), lambda i: (i, 0)),
        compiler_params=_params(("parallel",), 32),
        name="to_bf16",
    )(w)


def _pack_gate_up_kernel(*refs, tf):
    o_ref = refs[-1]
    for slot in range(len(refs) // 2):
        @pl.when(pl.program_id(0) == slot)
        def _(g_ref=refs[2 * slot], u_ref=refs[2 * slot + 1]):
            for j in range(o_ref.shape[0]):
                o_ref[j, :, :tf] = g_ref[:, j * tf:(j + 1) * tf].astype(BF16)
                o_ref[j, :, tf:] = u_ref[:, j * tf:(j + 1) * tf].astype(BF16)


def _pack_gate_up(gate_up_pairs, layer, *, tf, rows=64):
    _, r, c = gate_up_pairs[0][0].shape
    assert r % rows == 0 and c % tf == 0
    n_chunks, n_blocks = c // tf, r // rows

    def in_spec(slot):
        return pl.BlockSpec((None, rows, c), lambda s, i: (layer, jnp.where(s == slot, i, n_blocks - 1), 0))

    weights = [w for pair in gate_up_pairs for w in pair]
    assert all(w.shape == weights[0].shape for w in weights)
    return pl.pallas_call(
        functools.partial(_pack_gate_up_kernel, tf=tf),
        out_shape=jax.ShapeDtypeStruct((len(gate_up_pairs), n_chunks, r, 2 * tf), BF16),
        grid=(len(gate_up_pairs), n_blocks),
        in_specs=[in_spec(slot) for slot in range(len(gate_up_pairs)) for _ in range(2)],
        out_specs=pl.BlockSpec((None, n_chunks, rows, 2 * tf), lambda s, i: (s, 0, i, 0)),
        compiler_params=_params(("arbitrary", "arbitrary"), 32),
        name="pack_gate_up",
    )(*weights)


def _rotary_tables(seq, gain_q, gain_k):
    n_freq = HEAD_DIM // 4
    inv_freq = ROPE_THETA ** (-jnp.arange(n_freq, dtype=F32) / n_freq)
    t = jnp.arange(seq)
    row = (t // GRID_W).astype(F32)
    col = (t % GRID_W).astype(F32)
    ang = jnp.concatenate([row[:, None] * inv_freq[None], col[:, None] * inv_freq[None]], axis=-1)
    cos = jnp.concatenate([jnp.cos(ang), jnp.cos(ang)], axis=-1)
    sin = jnp.concatenate([-jnp.sin(ang), jnp.sin(ang)], axis=-1)
    return jnp.concatenate([cos * g if part == 0 else sin * jnp.roll(g, HEAD_DIM // 2)
                            for g in (gain_q, gain_k) for part in (0, 1)], axis=-1)


_W_IN_SEGMENTS = dict(q_b=QKV_A_WIDTH, k_b=QKV_A_WIDTH + WIDTH_B, v_b=QKV_A_WIDTH + 2 * WIDTH_B,
                      g_a=QKV_A_WIDTH + 3 * WIDTH_B, g_b=QKV_A_WIDTH + 3 * WIDTH_B + D_MODEL)
_PROJ_LAYOUT = (("g_a", D_MODEL), ("g_b", D_MODEL), ("q_b", WIDTH_B), ("k_b", WIDTH_B), ("v_b", WIDTH_B))
_PROJ_SRC_COLS = tuple((_W_IN_SEGMENTS[name] + off) // PROJ_TN
                       for name, width in _PROJ_LAYOUT for off in range(0, width, PROJ_TN))
assert all(start % PROJ_TN == 0 for start in _W_IN_SEGMENTS.values())
_GA_COL, _GB_COL = 0, 1
_QB_COL, _KB_COL, _VB_COL = 4, 5, 6
assert 2 * D_MODEL == _QB_COL * WIDTH_B


def _prepare_weights(ffn1_norm, ffn1_w_gate, ffn1_w_up, ffn1_w_down, mix_norm, w_in, q_norm, k_norm,
                     w_branch_a, w_branch_b, w_out, ffn2_norm, ffn2_w_gate, ffn2_w_up, ffn2_w_down,
                     ple_norm, w_ple_gate, w_ple_proj, final_norm, layer):
    i = layer
    half = np.concatenate([np.arange(0, HEAD_DIM, 2), np.arange(1, HEAD_DIM, 2)])
    w_in_bf16 = _to_bf16(w_in, i)
    w_rot = _rotary_weight_layout(w_in_bf16)
    w_gate_up = _pack_gate_up([(ffn1_w_gate, ffn1_w_up), (ffn2_w_gate, ffn2_w_up)], i, tf=FFN_TF)
    q_scale = ATTN_SCALE * LOG2_E
    proj_scale = jnp.concatenate([jnp.full((width,), q_scale if name == "q_b" else 1.0, F32)
                                  for name, width in _PROJ_LAYOUT])[None]
    rot_gains = ((q_norm[i] * q_scale)[half], k_norm[i][half])
    gqa_score_bound = (HEAD_DIM * q_scale * BF16_NORM_MARGIN
                       * jnp.max(jnp.abs(q_norm[i])) * jnp.max(jnp.abs(k_norm[i])))
    return dict(
        ffn1=(ffn1_norm[i][None], w_gate_up, 0, _to_bf16(ffn1_w_down, i, FFN_RESIDUAL_SCALE), mix_norm[i][None]),
        w_rot=w_rot, rot_gains=rot_gains, w_in=w_in_bf16, proj_scale=proj_scale, gqa_score_bound=gqa_score_bound,
        merge=(_to_bf16(w_branch_a, i), _to_bf16(w_branch_b, i), _to_bf16(w_out, i)),
        ffn2=(ffn2_norm[i][None], w_gate_up, 1, _to_bf16(ffn2_w_down, i, FFN_RESIDUAL_SCALE), ple_norm[i][None]),
        ple=(_to_bf16(w_ple_gate, i), _to_bf16(w_ple_proj, i), final_norm[None]),
    )


def _encoder(x, p, wts, rot, col_bias):
    b, s, d = x.shape
    t = b * s
    h1, u = _ffn(x.reshape(t, d), *wts["ffn1"])
    q_a, k_a, v_a = _qkv_a(u, wts["w_rot"], wts["w_in"], rot, s)
    z = _proj(u, wts["w_in"], _PROJ_SRC_COLS, wts["proj_scale"])
    qkv = (q_a.reshape(b, s, -1), k_a.reshape(b, s, -1), v_a.reshape(b, s, -1))
    o_a = lax.cond(wts["gqa_score_bound"] <= GQA_SAFE_LOG2_SCORE,
                   functools.partial(_gqa, bounded=True), functools.partial(_gqa, bounded=False), *qkv)
    o_b = _nbr(z.reshape(b, s, -1), col_bias, _QB_COL, _KB_COL, _VB_COL)
    h2 = _merge(o_a.reshape(t, -1), o_b.reshape(t, -1), z, _GA_COL, _GB_COL, h1, *wts["merge"])
    h3, n = _ffn(h2, *wts["ffn2"])
    y = _ple(h3, n, p.reshape(t, -1), *wts["ple"])
    return y.reshape(b, s, d)


def kernel(x_prompt, x_sample, p_prompt, p_sample, ffn1_norm, ffn1_w_gate, ffn1_w_up, ffn1_w_down, mix_norm, w_in, q_norm, k_norm, nat_rpb, w_branch_a, w_branch_b, w_out, ffn2_norm, ffn2_w_gate, ffn2_w_up, ffn2_w_down, ple_norm, w_ple_gate, w_ple_proj, final_norm):
    assert ffn1_norm.shape[0] == 1, "single-layer encoder"
    wts = _prepare_weights(ffn1_norm, ffn1_w_gate, ffn1_w_up, ffn1_w_down, mix_norm, w_in, q_norm, k_norm,
                           w_branch_a, w_branch_b, w_out, ffn2_norm, ffn2_w_gate, ffn2_w_up, ffn2_w_down,
                           ple_norm, w_ple_gate, w_ple_proj, final_norm, 0)
    col_bias = _nbr_col_bias(nat_rpb[0])
    tables = {}
    outs = []
    for x, p in ((x_prompt, p_prompt), (x_sample, p_sample)):
        seq = x.shape[1]
        if seq not in tables:
            tables[seq] = _rotary_tables(seq, *wts["rot_gains"])
        outs.append(_encoder(x, p[0], wts, tables[seq], col_bias))
    return tuple(outs)
```

```python
import functools
import math

import numpy as np
import jax
import jax.numpy as jnp
from jax import lax
from jax.experimental import pallas as pl
from jax.experimental.pallas import tpu as pltpu

F32 = jnp.float32
BF16 = jnp.bfloat16

D_MODEL = 2048
HEAD_DIM = 128
N_HEADS_A = 8
N_KV_A = 2
GQA_GROUP = N_HEADS_A // N_KV_A
N_HEADS_B = 8
WIDTH_A = N_HEADS_A * HEAD_DIM
KV_WIDTH_A = N_KV_A * HEAD_DIM
WIDTH_B = N_HEADS_B * HEAD_DIM
ROT_WIDTH = WIDTH_A + KV_WIDTH_A
QKV_A_WIDTH = ROT_WIDTH + KV_WIDTH_A
D_FF = 5632
D_PLE = 256
GRID_W = 64
WIN_ROWS = 8
WIN_COLS = 16
ROPE_THETA = 10000.0
EPS = 1e-6
ATTN_SCALE = HEAD_DIM ** -0.5
LOG2_E = math.log2(math.e)
MASKED = -1e30
GQA_SAFE_LOG2_SCORE = 48.0
BF16_NORM_MARGIN = 1.02

NBR_Q_ROWS = 4
NBR_KEY_ROWS = 12
NBR_Q = NBR_Q_ROWS * GRID_W
NBR_KEYS = NBR_KEY_ROWS * GRID_W
NBR_DR_PAD = (NBR_KEY_ROWS - NBR_Q_ROWS) - (WIN_ROWS - 1) + (NBR_Q_ROWS - 1)
NBR_DR_SLOTS = NBR_DR_PAD + (WIN_ROWS - 1) + (NBR_KEY_ROWS - 2) + 1

PROJ_TN = 512

V7X_VMEM_BYTES = 64 * 1024 * 1024
MIB = 1024 * 1024


def _params(semantics, vmem_mib):
    assert vmem_mib * MIB < V7X_VMEM_BYTES
    return pltpu.CompilerParams(dimension_semantics=semantics, vmem_limit_bytes=vmem_mib * MIB)


def _resident(shape):
    zeros = (0,) * len(shape)
    return pl.BlockSpec(shape, lambda *_: zeros, pipeline_mode=pl.Buffered(1))


def _rms(x):
    return x * lax.rsqrt(jnp.mean(x * x, axis=-1, keepdims=True) + EPS)


FFN_RESIDUAL_SCALE = 0.5
FFN_TF = 512
ROW_CHUNK = 16
ROW_UNROLL = 8
ROW_UNROLL_EXIT = 16

def _ffn_kernel(*refs, tile_starts):
    n_src = len(tile_starts) - 1
    x_srcs = refs[:n_src]
    g1_ref, wgu_ref, wd_ref, g2_ref, h_ref, n_ref, x_buf, xn_ref, x_sem = refs[n_src:]
    i = pl.program_id(0)
    j = pl.program_id(1)
    tm = x_buf.shape[0]

    def x_copy(tile, wait):
        for s, x_hbm in enumerate(x_srcs):
            @pl.when((tile >= tile_starts[s]) & (tile < tile_starts[s + 1]))
            def _(s=s, x_hbm=x_hbm):
                row = pl.multiple_of((tile - tile_starts[s]) * tm, tm)
                copy = pltpu.make_async_copy(x_hbm.at[pl.ds(row, tm), :], x_buf, x_sem)
                copy.wait() if wait else copy.start()

    @pl.when(j == 0)
    def _():
        @pl.when(i == 0)
        def _():
            x_copy(0, wait=False)

        x_copy(i, wait=True)

        def first(r, carry):
            rows = pl.ds(pl.multiple_of(r * ROW_CHUNK, ROW_CHUNK), ROW_CHUNK)
            x = x_buf[rows, :]
            xn_ref[rows, :] = (_rms(x) * g1_ref[...]).astype(BF16)
            h_ref[rows, :] = x
            return carry

        lax.fori_loop(0, tm // ROW_CHUNK, first, 0, unroll=ROW_UNROLL)

        @pl.when(i + 1 < pl.num_programs(0))
        def _():
            x_copy(i + 1, wait=False)

    tf = wd_ref.shape[0]
    gu = jnp.dot(xn_ref[...], wgu_ref[...], preferred_element_type=F32)
    g, u = gu[:, :tf], gu[:, tf:]
    a = (g * jax.nn.sigmoid(g) * u).astype(BF16)
    h_ref[...] += jnp.dot(a, wd_ref[...], preferred_element_type=F32)

    @pl.when(j == pl.num_programs(1) - 1)
    def _():
        def last(r, carry):
            rows = pl.ds(pl.multiple_of(r * ROW_CHUNK, ROW_CHUNK), ROW_CHUNK)
            n_ref[rows, :] = (_rms(h_ref[rows, :]) * g2_ref[...]).astype(BF16)
            return carry

        lax.fori_loop(0, tm // ROW_CHUNK, last, 0, unroll=ROW_UNROLL_EXIT)


def _ffn(xs, g1, w_gate_up, slot, wd, g2, *, tm=1024):
    d = xs[0].shape[1]
    dff, tf = wd.shape[0], FFN_TF
    assert all(x.shape[0] % tm == 0 and x.shape[1] == d for x in xs)
    assert w_gate_up.shape[1:] == (dff // tf, d, 2 * tf)
    tile_starts = tuple(int(n) for n in np.cumsum([0] + [x.shape[0] // tm for x in xs]))
    t = tile_starts[-1] * tm
    return pl.pallas_call(
        functools.partial(_ffn_kernel, tile_starts=tile_starts),
        out_shape=(jax.ShapeDtypeStruct((t, d), F32), jax.ShapeDtypeStruct((t, d), BF16)),
        grid=(t // tm, dff // tf),
        in_specs=[pl.BlockSpec(memory_space=pl.ANY)] * len(xs) + [
            pl.BlockSpec((1, d), lambda i, j: (0, 0)),
            pl.BlockSpec((None, None, d, 2 * tf), lambda i, j: (slot, j, 0, 0)),
            pl.BlockSpec((tf, d), lambda i, j: (j, 0)),
            pl.BlockSpec((1, d), lambda i, j: (0, 0)),
        ],
        out_specs=(
            pl.BlockSpec((tm, d), lambda i, j: (i, 0)),
            pl.BlockSpec((tm, d), lambda i, j: (i, 0)),
        ),
        scratch_shapes=[pltpu.VMEM((tm, d), F32), pltpu.VMEM((tm, d), BF16), pltpu.SemaphoreType.DMA(())],
        compiler_params=_params(("arbitrary", "arbitrary"), 60),
        name="ffn",
    )(*xs, g1, w_gate_up, wd, g2)


def _permute_heads_kernel(w_ref, perm_ref, o_ref):
    o_ref[...] = jnp.dot(w_ref[...], perm_ref[...], preferred_element_type=F32).astype(BF16)


def _rotary_weight_layout(w_in):
    d = w_in.shape[0]
    half = np.concatenate([np.arange(0, HEAD_DIM, 2), np.arange(1, HEAD_DIM, 2)])
    perm = jnp.asarray(np.arange(HEAD_DIM)[:, None] == half[None, :], BF16)
    return pl.pallas_call(
        _permute_heads_kernel,
        out_shape=jax.ShapeDtypeStruct((d, ROT_WIDTH), BF16),
        grid=(ROT_WIDTH // HEAD_DIM,),
        in_specs=[pl.BlockSpec((d, HEAD_DIM), lambda h: (0, h)), _resident((HEAD_DIM, HEAD_DIM))],
        out_specs=pl.BlockSpec((d, HEAD_DIM), lambda h: (0, h)),
        compiler_params=_params(("parallel",), 16),
        name="rotary_weight_layout",
    )(w_in, perm)


def _qkv_a_kernel(u_ref, wr_ref, wv_ref, rot_ref, q_ref, k_ref, v_ref):
    u = u_ref[...]
    pair_w = 2 * HEAD_DIM

    def project(pair):
        return jnp.dot(u, wr_ref[:, pair * pair_w:(pair + 1) * pair_w], preferred_element_type=F32)

    z_next = project(0)
    for pair in range((N_HEADS_A + N_KV_A) // 2):
        z = z_next
        if 2 * (pair + 1) < N_HEADS_A + N_KV_A:
            z_next = project(pair + 1)
        for sub in range(2):
            hh = 2 * pair + sub
            zh = z[:, sub * HEAD_DIM:(sub + 1) * HEAD_DIM]
            tab = 0 if hh < N_HEADS_A else 2
            cos_g = rot_ref[:, tab * HEAD_DIM:(tab + 1) * HEAD_DIM]
            sin_g = rot_ref[:, (tab + 1) * HEAD_DIM:(tab + 2) * HEAD_DIM]
            r = lax.rsqrt(jnp.mean(zh * zh, axis=-1, keepdims=True) + EPS)
            out = ((zh * cos_g + pltpu.roll(zh, HEAD_DIM // 2, 1) * sin_g) * r).astype(BF16)
            if hh < N_HEADS_A:
                q_ref[:, hh * HEAD_DIM:(hh + 1) * HEAD_DIM] = out
            else:
                kk = hh - N_HEADS_A
                k_ref[:, kk * HEAD_DIM:(kk + 1) * HEAD_DIM] = out
    v_ref[...] = jnp.dot(u, wv_ref[...], preferred_element_type=F32).astype(BF16)


def _qkv_a(u, w_rot, w_in, rot, seq, *, tm=512):
    t, d = u.shape
    assert t % tm == 0 and seq % tm == 0 and ROT_WIDTH % KV_WIDTH_A == 0
    assert N_HEADS_A % 2 == 0 and N_KV_A % 2 == 0
    seq_blocks = seq // tm
    return pl.pallas_call(
        _qkv_a_kernel,
        out_shape=(
            jax.ShapeDtypeStruct((t, WIDTH_A), BF16),
            jax.ShapeDtypeStruct((t, KV_WIDTH_A), BF16),
            jax.ShapeDtypeStruct((t, KV_WIDTH_A), BF16),
        ),
        grid=(t // tm,),
        in_specs=[
            pl.BlockSpec((tm, d), lambda i: (i, 0)),
            _resident((d, ROT_WIDTH)),
            pl.BlockSpec((d, KV_WIDTH_A), lambda i: (0, ROT_WIDTH // KV_WIDTH_A), pipeline_mode=pl.Buffered(1)),
            pl.BlockSpec((tm, 4 * HEAD_DIM), lambda i: (i % seq_blocks, 0)),
        ],
        out_specs=(
            pl.BlockSpec((tm, WIDTH_A), lambda i: (i, 0)),
            pl.BlockSpec((tm, KV_WIDTH_A), lambda i: (i, 0)),
            pl.BlockSpec((tm, KV_WIDTH_A), lambda i: (i, 0)),
        ),
        compiler_params=_params(("parallel",), 40),
        name="qkv_a",
    )(u, w_rot, w_in, rot)


def _proj_kernel(u_ref, w_ref, scale_ref, z_ref):
    z = jnp.dot(u_ref[...], w_ref[...], preferred_element_type=F32)
    z_ref[...] = (z * scale_ref[...]).astype(BF16)


def _proj(u, w, src_cols, col_scale, *, tm=2048, tn=PROJ_TN):
    t, d = u.shape
    n = len(src_cols) * tn
    assert t % tm == 0 and col_scale.shape == (1, n)

    def w_block(i, j):
        src = src_cols[0]
        for k in range(1, len(src_cols)):
            src = jnp.where(j >= k, src_cols[k], src)
        return (0, src)

    return pl.pallas_call(
        _proj_kernel,
        out_shape=jax.ShapeDtypeStruct((t, n), BF16),
        grid=(t // tm, len(src_cols)),
        in_specs=[
            pl.BlockSpec((tm, d), lambda i, j: (i, 0)),
            pl.BlockSpec((d, tn), w_block),
            pl.BlockSpec((1, tn), lambda i, j: (0, j)),
        ],
        out_specs=pl.BlockSpec((tm, tn), lambda i, j: (i, j)),
        compiler_params=_params(("parallel", "arbitrary"), 40),
        name="proj",
    )(u, w, col_scale)


def _softmax_pv(s, v):
    m = jnp.max(s, axis=-1, keepdims=True)
    p = jnp.exp2(s - m)
    l = jnp.sum(p, axis=-1, keepdims=True)
    return jnp.dot(p.astype(BF16), v, preferred_element_type=F32) / l


_NT = (((1,), (1,)), ((), ()))


def _gqa_kernel(q_ref, k_ref, v_ref, o_ref, *, bounded):
    k = k_ref[0]
    v = v_ref[0]

    def score(g):
        return lax.dot_general(q_ref[0, :, g * HEAD_DIM:(g + 1) * HEAD_DIM], k, _NT, preferred_element_type=F32)

    s_next = score(0)
    for g in range(GQA_GROUP):
        s = s_next
        if g + 1 < GQA_GROUP:
            s_next = score(g + 1)
        if bounded:
            p = jnp.exp2(s)
            l = jnp.sum(p, axis=-1, keepdims=True)
            o = jnp.dot(p.astype(BF16), v, preferred_element_type=F32) / l
        else:
            o = _softmax_pv(s, v)
        o_ref[0, :, g * HEAD_DIM:(g + 1) * HEAD_DIM] = o.astype(BF16)


def _gqa(q, k, v, *, bounded, tq=512):
    b, s, _ = q.shape
    assert s % tq == 0
    gw = GQA_GROUP * HEAD_DIM
    return pl.pallas_call(
        functools.partial(_gqa_kernel, bounded=bounded),
        out_shape=jax.ShapeDtypeStruct((b, s, WIDTH_A), BF16),
        grid=(b, N_KV_A, s // tq),
        in_specs=[
            pl.BlockSpec((1, tq, gw), lambda bi, kh, qi: (bi, qi, kh)),
            pl.BlockSpec((1, s, HEAD_DIM), lambda bi, kh, qi: (bi, 0, kh)),
            pl.BlockSpec((1, s, HEAD_DIM), lambda bi, kh, qi: (bi, 0, kh)),
        ],
        out_specs=pl.BlockSpec((1, tq, gw), lambda bi, kh, qi: (bi, qi, kh)),
        compiler_params=_params(("parallel", "parallel", "arbitrary"), 40),
        name="gqa_bounded" if bounded else "gqa",
    )(q, k, v)


def _nbr_window_start(j, rows):
    return jnp.clip(NBR_Q_ROWS * j - WIN_ROWS // 2, 0, rows - NBR_KEY_ROWS)


def _nbr_block_plans(rows):
    plans = {}
    for j in range(rows // NBR_Q_ROWS):
        ws = int(np.clip(NBR_Q_ROWS * j - WIN_ROWS // 2, 0, rows - NBR_KEY_ROWS))
        lo = tuple(int(np.clip(NBR_Q_ROWS * j + qr - WIN_ROWS // 2, 0, rows - WIN_ROWS)) - ws
                   for qr in range(NBR_Q_ROWS))
        d0 = ws - NBR_Q_ROWS * j + (WIN_ROWS - 1) + NBR_DR_PAD
        assert min(lo) >= 0 and max(lo) + WIN_ROWS <= NBR_KEY_ROWS
        plans.setdefault((lo, d0), []).append(j)
    return plans


def _nbr_block(q_ref, k_ref, v_ref, cb_ref, o_ref, ws, lo, d0):
    pair_w = 2 * GRID_W
    pa = min(lo) // 2
    pb = (max(lo) + WIN_ROWS - 1) // 2 + 1
    n_keys = (pb - pa) * pair_w
    start = pl.multiple_of(ws * GRID_W + pa * pair_w, pair_w)
    lane = lax.broadcasted_iota(jnp.int32, (GRID_W, pair_w), 1)
    zeros = jnp.zeros((GRID_W, pair_w), BF16)
    heads = [slice(h * HEAD_DIM, (h + 1) * HEAD_DIM) for h in range(N_HEADS_B)]

    def score(cols):
        return lax.dot_general(q_ref[0, :, cols], k_ref[0, pl.ds(start, n_keys), cols], _NT,
                               preferred_element_type=F32)

    lookahead = len(set(lo)) > 1
    s_next = score(heads[0])
    for h, cols in enumerate(heads):
        s = s_next if lookahead or h == 0 else score(cols)
        p_rows, l_rows = [], []
        for qr in range(NBR_Q_ROWS):
            first, last = lo[qr], lo[qr] + WIN_ROWS - 1
            tiles = []
            for a in range(first // 2, last // 2 + 1):
                t = (s[qr * GRID_W:(qr + 1) * GRID_W, (a - pa) * pair_w:(a - pa + 1) * pair_w]
                     + cb_ref[h, d0 + 2 * a - qr])
                if 2 * a < first:
                    t = jnp.where(lane < GRID_W, MASKED, t)
                if 2 * a + 1 > last:
                    t = jnp.where(lane >= GRID_W, MASKED, t)
                tiles.append(t)
            m = jnp.max(functools.reduce(jnp.maximum, tiles), axis=-1, keepdims=True)
            ps = [jnp.exp2(t - m) for t in tiles]
            l_rows.append(jnp.sum(functools.reduce(jnp.add, ps), axis=-1, keepdims=True))
            p_rows.append(jnp.concatenate(
                [zeros] * (first // 2 - pa) + [p.astype(BF16) for p in ps] + [zeros] * (pb - 1 - last // 2),
                axis=1))
        p = jnp.concatenate(p_rows, axis=0)
        l = jnp.concatenate(l_rows, axis=0)
        if lookahead and h + 1 < N_HEADS_B:
            s_next = score(heads[h + 1])
        v = v_ref[0, pl.ds(start, n_keys), cols]
        o_ref[0, :, cols] = (jnp.dot(p, v, preferred_element_type=F32) / l).astype(BF16)


def _nbr_kernel(q_ref, k_ref, v_ref, cb_ref, o_ref, *, rows):
    j = pl.program_id(1)
    ws = _nbr_window_start(j, rows)
    for (lo, d0), members in _nbr_block_plans(rows).items():
        member = functools.reduce(jnp.logical_or, [j == m for m in members])
        pl.when(member)(functools.partial(_nbr_block, q_ref, k_ref, v_ref, cb_ref, o_ref, ws, lo, d0))


def _nbr(z, col_bias, q_col, k_col, v_col):
    b, s, _ = z.shape
    rows = s // GRID_W
    assert rows % NBR_Q_ROWS == 0 and rows >= NBR_KEY_ROWS
    return pl.pallas_call(
        functools.partial(_nbr_kernel, rows=rows),
        out_shape=jax.ShapeDtypeStruct((b, s, WIDTH_B), BF16),
        grid=(b, rows // NBR_Q_ROWS),
        in_specs=[
            pl.BlockSpec((1, NBR_Q, WIDTH_B), lambda bi, j: (bi, j, q_col)),
            pl.BlockSpec((1, s, WIDTH_B), lambda bi, j: (bi, 0, k_col)),
            pl.BlockSpec((1, s, WIDTH_B), lambda bi, j: (bi, 0, v_col)),
            _resident(col_bias.shape),
        ],
        out_specs=pl.BlockSpec((1, NBR_Q, WIDTH_B), lambda bi, j: (bi, j, 0)),
        compiler_params=_params(("parallel", "arbitrary"), 48),
        name="nbr",
    )(z, z, z, col_bias)


def _nbr_col_bias(rpb):
    n_dr, n_dc = 2 * WIN_ROWS - 1, 2 * WIN_COLS - 1
    c = np.arange(GRID_W)[:, None]
    kc = np.arange(GRID_W)[None, :]
    cs = np.clip(c - WIN_COLS // 2, 0, GRID_W - WIN_COLS)
    col_valid = (kc >= cs) & (kc < cs + WIN_COLS)
    onehot = ((kc - c + (WIN_COLS - 1))[None] == np.arange(n_dc)[:, None, None]) & col_valid[None]
    onehot = jnp.asarray(onehot.reshape(n_dc, GRID_W * GRID_W), F32)
    tm = jnp.einsum("hrd,dn->hrn", rpb * LOG2_E, onehot, precision=lax.Precision.HIGHEST)
    tm = jnp.where(col_valid.reshape(-1), tm, MASKED).reshape(-1, n_dr, GRID_W, GRID_W)
    hi_pad = NBR_DR_SLOTS + 1 - NBR_DR_PAD - n_dr
    tm = jnp.pad(tm, ((0, 0), (NBR_DR_PAD, hi_pad), (0, 0), (0, 0)), constant_values=MASKED)
    return jnp.concatenate([tm[:, :-1], tm[:, 1:]], axis=-1)


MERGE_TN = 512


def _merge_kernel(oa_ref, ob_ref, ga_ref, gb_ref, h_ref, wa_ref, wb_ref, wo_ref, out_ref, m_ref):
    oa = oa_ref[...]
    ob = ob_ref[...]
    for c in range(D_MODEL // MERGE_TN):
        cols = slice(c * MERGE_TN, (c + 1) * MERGE_TN)
        a = jnp.dot(oa, wa_ref[:, cols], preferred_element_type=F32)
        b = jnp.dot(ob, wb_ref[:, cols], preferred_element_type=F32)
        merged = (jax.nn.sigmoid(ga_ref[:, cols].astype(F32)) * a
                  + jax.nn.sigmoid(gb_ref[:, cols].astype(F32)) * b)
        m_ref[:, cols] = merged.astype(BF16)
    out_ref[...] = h_ref[...] + jnp.dot(m_ref[...], wo_ref[...], preferred_element_type=F32)


def _merge(oa, ob, z, ga_col, gb_col, h, wa, wb, wo, *, tm=512):
    t, d = h.shape
    assert t % tm == 0 and d == D_MODEL
    return pl.pallas_call(
        _merge_kernel,
        out_shape=jax.ShapeDtypeStruct((t, d), F32),
        grid=(t // tm,),
        in_specs=[
            pl.BlockSpec((tm, WIDTH_A), lambda i: (i, 0)),
            pl.BlockSpec((tm, WIDTH_B), lambda i: (i, 0)),
            pl.BlockSpec((tm, d), lambda i: (i, ga_col)),
            pl.BlockSpec((tm, d), lambda i: (i, gb_col)),
            pl.BlockSpec((tm, d), lambda i: (i, 0)),
            _resident((WIDTH_A, d)),
            _resident((WIDTH_B, d)),
            _resident((d, d)),
        ],
        out_specs=pl.BlockSpec((tm, d), lambda i: (i, 0)),
        scratch_shapes=[pltpu.VMEM((tm, d), BF16)],
        compiler_params=_params(("parallel",), 56),
        name="merge",
    )(oa, ob, z, z, h, wa, wb, wo)


def _ple_kernel(h_ref, n_ref, p_ref, wg_ref, wp_ref, gf_ref, y_ref):
    gate = jax.nn.sigmoid(jnp.dot(n_ref[...], wg_ref[...], preferred_element_type=F32))
    emb = jnp.dot(p_ref[...].astype(BF16), wp_ref[...], preferred_element_type=F32)
    y_ref[...] = _rms(h_ref[...] + gate * emb) * gf_ref[...]


def _ple(h, n, row0, p, wg, wp, gf, *, tm=512):
    t, d = p.shape[0], h.shape[1]
    assert t % tm == 0 and row0 % tm == 0
    tile0 = row0 // tm
    return pl.pallas_call(
        _ple_kernel,
        out_shape=jax.ShapeDtypeStruct((t, d), F32),
        grid=(t // tm,),
        in_specs=[
            pl.BlockSpec((tm, d), lambda i: (i + tile0, 0)),
            pl.BlockSpec((tm, d), lambda i: (i + tile0, 0)),
            pl.BlockSpec((tm, D_PLE), lambda i: (i, 0)),
            _resident((d, d)),
            _resident((D_PLE, d)),
            _resident((1, d)),
        ],
        out_specs=pl.BlockSpec((tm, d), lambda i: (i, 0)),
        compiler_params=_params(("parallel",), 48),
        name="ple",
    )(h, n, p, wg, wp, gf)


CAST_BLOCK_BYTES = 4 * MIB


def _cast_kernel(w_ref, o_ref, *, scale):
    w = w_ref[...]
    o_ref[...] = (w if scale == 1.0 else w * scale).astype(BF16)


def _to_bf16(w, layer, scale=1.0):
    _, r, c = w.shape
    target = max(16, CAST_BLOCK_BYTES // (4 * c))
    rows = next(n for n in range(min(r, target) // 16 * 16, 0, -16) if r % n == 0)
    return pl.pallas_call(
        functools.partial(_cast_kernel, scale=scale),
        out_shape=jax.ShapeDtypeStruct((r, c), BF16),
        grid=(r // rows,),
        in_specs=[pl.BlockSpec((None, rows, c), lambda i: (layer, i, 0))],
        out_specs=pl.BlockSpec((rows, c), lambda i: (i, 0)),
        compiler_params=_params(("parallel",), 32),
        name="to_bf16",
    )(w)


def _pack_gate_up_kernel(*refs, tf):
    o_ref = refs[-1]
    for slot in range(len(refs) // 2):
        @pl.when(pl.program_id(0) == slot)
        def _(g_ref=refs[2 * slot], u_ref=refs[2 * slot + 1]):
            for j in range(o_ref.shape[0]):
                o_ref[j, :, :tf] = g_ref[:, j * tf:(j + 1) * tf].astype(BF16)
                o_ref[j, :, tf:] = u_ref[:, j * tf:(j + 1) * tf].astype(BF16)


def _pack_gate_up(gate_up_pairs, layer, *, tf, rows=64):
    _, r, c = gate_up_pairs[0][0].shape
    assert r % rows == 0 and c % tf == 0
    n_chunks, n_blocks = c // tf, r // rows

    def in_spec(slot):
        return pl.BlockSpec((None, rows, c), lambda s, i: (layer, jnp.where(s == slot, i, n_blocks - 1), 0))

    weights = [w for pair in gate_up_pairs for w in pair]
    assert all(w.shape == weights[0].shape for w in weights)
    return pl.pallas_call(
        functools.partial(_pack_gate_up_kernel, tf=tf),
        out_shape=jax.ShapeDtypeStruct((len(gate_up_pairs), n_chunks, r, 2 * tf), BF16),
        grid=(len(gate_up_pairs), n_blocks),
        in_specs=[in_spec(slot) for slot in range(len(gate_up_pairs)) for _ in range(2)],
        out_specs=pl.BlockSpec((None, n_chunks, rows, 2 * tf), lambda s, i: (s, 0, i, 0)),
        compiler_params=_params(("arbitrary", "arbitrary"), 32),
        name="pack_gate_up",
    )(*weights)


def _rotary_tables(seq, gain_q, gain_k):
    n_freq = HEAD_DIM // 4
    inv_freq = ROPE_THETA ** (-jnp.arange(n_freq, dtype=F32) / n_freq)
    t = jnp.arange(seq)
    row = (t // GRID_W).astype(F32)
    col = (t % GRID_W).astype(F32)
    ang = jnp.concatenate([row[:, None] * inv_freq[None], col[:, None] * inv_freq[None]], axis=-1)
    cos = jnp.concatenate([jnp.cos(ang), jnp.cos(ang)], axis=-1)
    sin = jnp.concatenate([-jnp.sin(ang), jnp.sin(ang)], axis=-1)
    return jnp.concatenate([cos * g if part == 0 else sin * jnp.roll(g, HEAD_DIM // 2)
                            for g in (gain_q, gain_k) for part in (0, 1)], axis=-1)


_W_IN_SEGMENTS = dict(q_b=QKV_A_WIDTH, k_b=QKV_A_WIDTH + WIDTH_B, v_b=QKV_A_WIDTH + 2 * WIDTH_B,
                      g_a=QKV_A_WIDTH + 3 * WIDTH_B, g_b=QKV_A_WIDTH + 3 * WIDTH_B + D_MODEL)
_PROJ_LAYOUT = (("g_a", D_MODEL), ("g_b", D_MODEL), ("q_b", WIDTH_B), ("k_b", WIDTH_B), ("v_b", WIDTH_B))
_PROJ_SRC_COLS = tuple((_W_IN_SEGMENTS[name] + off) // PROJ_TN
                       for name, width in _PROJ_LAYOUT for off in range(0, width, PROJ_TN))
assert all(start % PROJ_TN == 0 for start in _W_IN_SEGMENTS.values())
_GA_COL, _GB_COL = 0, 1
_QB_COL, _KB_COL, _VB_COL = 4, 5, 6
assert 2 * D_MODEL == _QB_COL * WIDTH_B


def _prepare_weights(ffn1_norm, ffn1_w_gate, ffn1_w_up, ffn1_w_down, mix_norm, w_in, q_norm, k_norm,
                     w_branch_a, w_branch_b, w_out, ffn2_norm, ffn2_w_gate, ffn2_w_up, ffn2_w_down,
                     ple_norm, w_ple_gate, w_ple_proj, final_norm, layer):
    i = layer
    half = np.concatenate([np.arange(0, HEAD_DIM, 2), np.arange(1, HEAD_DIM, 2)])
    w_in_bf16 = _to_bf16(w_in, i)
    w_rot = _rotary_weight_layout(w_in_bf16)
    w_gate_up = _pack_gate_up([(ffn1_w_gate, ffn1_w_up), (ffn2_w_gate, ffn2_w_up)], i, tf=FFN_TF)
    q_scale = ATTN_SCALE * LOG2_E
    proj_scale = jnp.concatenate([jnp.full((width,), q_scale if name == "q_b" else 1.0, F32)
                                  for name, width in _PROJ_LAYOUT])[None]
    rot_gains = ((q_norm[i] * q_scale)[half], k_norm[i][half])
    gqa_score_bound = (HEAD_DIM * q_scale * BF16_NORM_MARGIN
                       * jnp.max(jnp.abs(q_norm[i])) * jnp.max(jnp.abs(k_norm[i])))
    return dict(
        ffn1=(ffn1_norm[i][None], w_gate_up, 0, _to_bf16(ffn1_w_down, i, FFN_RESIDUAL_SCALE), mix_norm[i][None]),
        w_rot=w_rot, rot_gains=rot_gains, w_in=w_in_bf16, proj_scale=proj_scale, gqa_score_bound=gqa_score_bound,
        merge=(_to_bf16(w_branch_a, i), _to_bf16(w_branch_b, i), _to_bf16(w_out, i)),
        ffn2=(ffn2_norm[i][None], w_gate_up, 1, _to_bf16(ffn2_w_down, i, FFN_RESIDUAL_SCALE), ple_norm[i][None]),
        ple=(_to_bf16(w_ple_gate, i), _to_bf16(w_ple_proj, i), final_norm[None]),
    )


def _encoder(xs, ps, wts, rot, col_bias):
    s, d = xs[0].shape[1:]
    assert all(x.shape[1:] == (s, d) for x in xs)
    b = sum(x.shape[0] for x in xs)
    t = b * s
    h1, u = _ffn([x.reshape(-1, d) for x in xs], *wts["ffn1"])
    q_a, k_a, v_a = _qkv_a(u, wts["w_rot"], wts["w_in"], rot, s)
    z = _proj(u, wts["w_in"], _PROJ_SRC_COLS, wts["proj_scale"])
    qkv = (q_a.reshape(b, s, -1), k_a.reshape(b, s, -1), v_a.reshape(b, s, -1))
    o_a = lax.cond(wts["gqa_score_bound"] <= GQA_SAFE_LOG2_SCORE,
                   functools.partial(_gqa, bounded=True), functools.partial(_gqa, bounded=False), *qkv)
    o_b = _nbr(z.reshape(b, s, -1), col_bias, _QB_COL, _KB_COL, _VB_COL)
    h2 = _merge(o_a.reshape(t, -1), o_b.reshape(t, -1), z, _GA_COL, _GB_COL, h1, *wts["merge"])
    h3, n = _ffn([h2], *wts["ffn2"])
    row_starts = np.cumsum([0] + [x.shape[0] * s for x in xs])
    return tuple(_ple(h3, n, int(row0), p.reshape(-1, p.shape[-1]), *wts["ple"]).reshape(x.shape)
                 for row0, p, x in zip(row_starts, ps, xs))


def kernel(x_prompt, x_sample, p_prompt, p_sample, ffn1_norm, ffn1_w_gate, ffn1_w_up, ffn1_w_down, mix_norm, w_in, q_norm, k_norm, nat_rpb, w_branch_a, w_branch_b, w_out, ffn2_norm, ffn2_w_gate, ffn2_w_up, ffn2_w_down, ple_norm, w_ple_gate, w_ple_proj, final_norm):
    assert ffn1_norm.shape[0] == 1, "single-layer encoder"
    wts = _prepare_weights(ffn1_norm, ffn1_w_gate, ffn1_w_up, ffn1_w_down, mix_norm, w_in, q_norm, k_norm,
                           w_branch_a, w_branch_b, w_out, ffn2_norm, ffn2_w_gate, ffn2_w_up, ffn2_w_down,
                           ple_norm, w_ple_gate, w_ple_proj, final_norm, 0)
    col_bias = _nbr_col_bias(nat_rpb[0])
    groups = ((x_prompt, p_prompt[0]), (x_sample, p_sample[0]))
    outs = {}
    for seq in sorted({x.shape[1] for x, _ in groups}):
        members = [k for k, (x, _) in enumerate(groups) if x.shape[1] == seq]
        rot = _rotary_tables(seq, *wts["rot_gains"])
        ys = _encoder([groups[k][0] for k in members], [groups[k][1] for k in members], wts, rot, col_bias)
        outs.update(zip(members, ys))
    return tuple(outs[k] for k in range(len(groups)))
```

```python
import functools
import math

import numpy as np
import jax
import jax.numpy as jnp
from jax import lax
from jax.experimental import pallas as pl
from jax.experimental.pallas import tpu as pltpu

F32 = jnp.float32
BF16 = jnp.bfloat16

D_MODEL = 2048
HEAD_DIM = 128
N_HEADS_A = 8
N_KV_A = 2
GQA_GROUP = N_HEADS_A // N_KV_A
N_HEADS_B = 8
WIDTH_A = N_HEADS_A * HEAD_DIM
KV_WIDTH_A = N_KV_A * HEAD_DIM
WIDTH_B = N_HEADS_B * HEAD_DIM
ROT_WIDTH = WIDTH_A + KV_WIDTH_A
QKV_A_WIDTH = ROT_WIDTH + KV_WIDTH_A
D_FF = 5632
D_PLE = 256
GRID_W = 64
WIN_ROWS = 8
WIN_COLS = 16
ROPE_THETA = 10000.0
EPS = 1e-6
ATTN_SCALE = HEAD_DIM ** -0.5
LOG2_E = math.log2(math.e)
MASKED = -1e30
GQA_SAFE_LOG2_SCORE = 48.0
BF16_NORM_MARGIN = 1.02

NBR_Q_ROWS = 4
NBR_KEY_ROWS = 12
NBR_Q = NBR_Q_ROWS * GRID_W
NBR_KEYS = NBR_KEY_ROWS * GRID_W
NBR_DR_PAD = (NBR_KEY_ROWS - NBR_Q_ROWS) - (WIN_ROWS - 1) + (NBR_Q_ROWS - 1)
NBR_DR_SLOTS = NBR_DR_PAD + (WIN_ROWS - 1) + (NBR_KEY_ROWS - 2) + 1

PROJ_TN = 512

V7X_VMEM_BYTES = 64 * 1024 * 1024
MIB = 1024 * 1024


def _params(semantics, vmem_mib):
    assert vmem_mib * MIB < V7X_VMEM_BYTES
    return pltpu.CompilerParams(dimension_semantics=semantics, vmem_limit_bytes=vmem_mib * MIB)


def _resident(shape):
    zeros = (0,) * len(shape)
    return pl.BlockSpec(shape, lambda *_: zeros, pipeline_mode=pl.Buffered(1))


def _rms(x):
    return x * lax.rsqrt(jnp.mean(x * x, axis=-1, keepdims=True) + EPS)


FFN_RESIDUAL_SCALE = 0.5
FFN_TF = 512
ROW_CHUNK = 16
NORM_ROWS = 128
ROW_UNROLL_EXIT = 16

def _ffn_kernel(*refs, tile_starts):
    n_src = len(tile_starts) - 1
    x_srcs = refs[:n_src]
    g1_ref, wgu_ref, wd_ref, g2_ref, h_ref, n_ref, x_buf, xn_ref, x_sem = refs[n_src:]
    i = pl.program_id(0)
    j = pl.program_id(1)
    tm = x_buf.shape[0]

    def x_copy(tile, wait):
        for s, x_hbm in enumerate(x_srcs):
            @pl.when((tile >= tile_starts[s]) & (tile < tile_starts[s + 1]))
            def _(s=s, x_hbm=x_hbm):
                row = pl.multiple_of((tile - tile_starts[s]) * tm, tm)
                copy = pltpu.make_async_copy(x_hbm.at[pl.ds(row, tm), :], x_buf, x_sem)
                copy.wait() if wait else copy.start()

    cur = i % 2
    has_next = i + 1 < pl.num_programs(0)
    n_chunks = tm // NORM_ROWS

    def norm_rows(slot, row0, n_rows):
        for r in range(0, n_rows, ROW_CHUNK):
            rows = pl.ds(row0 + r, ROW_CHUNK)
            xn_ref[slot, rows, :] = (_rms(x_buf[rows, :]) * g1_ref[...]).astype(BF16)

    @pl.when((i == 0) & (j == 0))
    def _():
        x_copy(0, wait=False)
        x_copy(0, wait=True)

        def first(c, carry):
            norm_rows(0, pl.multiple_of(c * NORM_ROWS, NORM_ROWS), NORM_ROWS)
            return carry

        lax.fori_loop(0, n_chunks, first, 0)

    @pl.when((j == 1) & has_next)
    def _():
        x_copy(i + 1, wait=False)

    @pl.when((j == 2) & has_next)
    def _():
        x_copy(i + 1, wait=True)

    tf = wd_ref.shape[0]

    def step(first_step, prepare_next):
        gu = jnp.dot(xn_ref[cur], wgu_ref[...], preferred_element_type=F32)
        g, u = gu[:, :tf], gu[:, tf:]
        a = (g * jax.nn.sigmoid(g) * u).astype(BF16)
        down = jnp.dot(a, wd_ref[...], preferred_element_type=F32)
        h_ref[...] = (x_buf[...] if first_step else h_ref[...]) + down
        if prepare_next:
            chunk = jnp.minimum(j - 2, n_chunks - 1)
            norm_rows(1 - cur, pl.multiple_of(chunk * NORM_ROWS, NORM_ROWS), NORM_ROWS)

    pl.when(j == 0)(functools.partial(step, True, False))
    pl.when(j == 1)(functools.partial(step, False, False))
    pl.when(j >= 2)(functools.partial(step, False, True))

    @pl.when(j == pl.num_programs(1) - 1)
    def _():
        def last(r, carry):
            rows = pl.ds(pl.multiple_of(r * ROW_CHUNK, ROW_CHUNK), ROW_CHUNK)
            n_ref[rows, :] = (_rms(h_ref[rows, :]) * g2_ref[...]).astype(BF16)
            return carry

        lax.fori_loop(0, tm // ROW_CHUNK, last, 0, unroll=ROW_UNROLL_EXIT)


def _ffn(xs, g1, w_gate_up, slot, wd, g2, *, tm=1024):
    d = xs[0].shape[1]
    dff, tf = wd.shape[0], FFN_TF
    assert all(x.shape[0] % tm == 0 and x.shape[1] == d for x in xs)
    assert w_gate_up.shape[1:] == (dff // tf, d, 2 * tf)
    assert tm % NORM_ROWS == 0 and tm // NORM_ROWS <= dff // tf - 2
    tile_starts = tuple(int(n) for n in np.cumsum([0] + [x.shape[0] // tm for x in xs]))
    t = tile_starts[-1] * tm
    return pl.pallas_call(
        functools.partial(_ffn_kernel, tile_starts=tile_starts),
        out_shape=(jax.ShapeDtypeStruct((t, d), F32), jax.ShapeDtypeStruct((t, d), BF16)),
        grid=(t // tm, dff // tf),
        in_specs=[pl.BlockSpec(memory_space=pl.ANY)] * len(xs) + [
            pl.BlockSpec((1, d), lambda i, j: (0, 0)),
            pl.BlockSpec((None, None, d, 2 * tf), lambda i, j: (slot, j, 0, 0)),
            pl.BlockSpec((tf, d), lambda i, j: (j, 0)),
            pl.BlockSpec((1, d), lambda i, j: (0, 0)),
        ],
        out_specs=(
            pl.BlockSpec((tm, d), lambda i, j: (i, 0)),
            pl.BlockSpec((tm, d), lambda i, j: (i, 0)),
        ),
        scratch_shapes=[pltpu.VMEM((tm, d), F32), pltpu.VMEM((2, tm, d), BF16), pltpu.SemaphoreType.DMA(())],
        compiler_params=_params(("arbitrary", "arbitrary"), 60),
        name="ffn",
    )(*xs, g1, w_gate_up, wd, g2)


def _permute_heads_kernel(w_ref, perm_ref, o_ref):
    o_ref[...] = jnp.dot(w_ref[...], perm_ref[...], preferred_element_type=F32).astype(BF16)


def _rotary_weight_layout(w_in):
    d = w_in.shape[0]
    half = np.concatenate([np.arange(0, HEAD_DIM, 2), np.arange(1, HEAD_DIM, 2)])
    perm = jnp.asarray(np.arange(HEAD_DIM)[:, None] == half[None, :], BF16)
    return pl.pallas_call(
        _permute_heads_kernel,
        out_shape=jax.ShapeDtypeStruct((d, ROT_WIDTH), BF16),
        grid=(ROT_WIDTH // HEAD_DIM,),
        in_specs=[pl.BlockSpec((d, HEAD_DIM), lambda h: (0, h)), _resident((HEAD_DIM, HEAD_DIM))],
        out_specs=pl.BlockSpec((d, HEAD_DIM), lambda h: (0, h)),
        compiler_params=_params(("parallel",), 16),
        name="rotary_weight_layout",
    )(w_in, perm)


def _qkv_a_kernel(u_ref, wr_ref, wv_ref, rot_ref, q_ref, k_ref, v_ref):
    u = u_ref[...]
    pair_w = 2 * HEAD_DIM

    def project(pair):
        return jnp.dot(u, wr_ref[:, pair * pair_w:(pair + 1) * pair_w], preferred_element_type=F32)

    z_next = project(0)
    for pair in range((N_HEADS_A + N_KV_A) // 2):
        z = z_next
        if 2 * (pair + 1) < N_HEADS_A + N_KV_A:
            z_next = project(pair + 1)
        for sub in range(2):
            hh = 2 * pair + sub
            zh = z[:, sub * HEAD_DIM:(sub + 1) * HEAD_DIM]
            tab = 0 if hh < N_HEADS_A else 2
            cos_g = rot_ref[:, tab * HEAD_DIM:(tab + 1) * HEAD_DIM]
            sin_g = rot_ref[:, (tab + 1) * HEAD_DIM:(tab + 2) * HEAD_DIM]
            r = lax.rsqrt(jnp.mean(zh * zh, axis=-1, keepdims=True) + EPS)
            out = ((zh * cos_g + pltpu.roll(zh, HEAD_DIM // 2, 1) * sin_g) * r).astype(BF16)
            if hh < N_HEADS_A:
                q_ref[:, hh * HEAD_DIM:(hh + 1) * HEAD_DIM] = out
            else:
                kk = hh - N_HEADS_A
                k_ref[:, kk * HEAD_DIM:(kk + 1) * HEAD_DIM] = out
    v_ref[...] = jnp.dot(u, wv_ref[...], preferred_element_type=F32).astype(BF16)


def _qkv_a(u, w_rot, w_in, rot, seq, *, tm=512):
    t, d = u.shape
    assert t % tm == 0 and seq % tm == 0 and ROT_WIDTH % KV_WIDTH_A == 0
    assert N_HEADS_A % 2 == 0 and N_KV_A % 2 == 0
    seq_blocks = seq // tm
    return pl.pallas_call(
        _qkv_a_kernel,
        out_shape=(
            jax.ShapeDtypeStruct((t, WIDTH_A), BF16),
            jax.ShapeDtypeStruct((t, KV_WIDTH_A), BF16),
            jax.ShapeDtypeStruct((t, KV_WIDTH_A), BF16),
        ),
        grid=(t // tm,),
        in_specs=[
            pl.BlockSpec((tm, d), lambda i: (i, 0)),
            _resident((d, ROT_WIDTH)),
            pl.BlockSpec((d, KV_WIDTH_A), lambda i: (0, ROT_WIDTH // KV_WIDTH_A), pipeline_mode=pl.Buffered(1)),
            pl.BlockSpec((tm, 4 * HEAD_DIM), lambda i: (i % seq_blocks, 0)),
        ],
        out_specs=(
            pl.BlockSpec((tm, WIDTH_A), lambda i: (i, 0)),
            pl.BlockSpec((tm, KV_WIDTH_A), lambda i: (i, 0)),
            pl.BlockSpec((tm, KV_WIDTH_A), lambda i: (i, 0)),
        ),
        compiler_params=_params(("parallel",), 40),
        name="qkv_a",
    )(u, w_rot, w_in, rot)


def _proj_kernel(u_ref, w_ref, scale_ref, z_ref):
    z = jnp.dot(u_ref[...], w_ref[...], preferred_element_type=F32)
    z_ref[...] = (z * scale_ref[...]).astype(BF16)


def _proj(u, w, src_cols, col_scale, *, tm=2048, tn=PROJ_TN):
    t, d = u.shape
    n = len(src_cols) * tn
    assert t % tm == 0 and col_scale.shape == (1, n)

    def w_block(i, j):
        src = src_cols[0]
        for k in range(1, len(src_cols)):
            src = jnp.where(j >= k, src_cols[k], src)
        return (0, src)

    return pl.pallas_call(
        _proj_kernel,
        out_shape=jax.ShapeDtypeStruct((t, n), BF16),
        grid=(t // tm, len(src_cols)),
        in_specs=[
            pl.BlockSpec((tm, d), lambda i, j: (i, 0)),
            pl.BlockSpec((d, tn), w_block),
            pl.BlockSpec((1, tn), lambda i, j: (0, j)),
        ],
        out_specs=pl.BlockSpec((tm, tn), lambda i, j: (i, j)),
        compiler_params=_params(("parallel", "arbitrary"), 40),
        name="proj",
    )(u, w, col_scale)


def _softmax_pv(s, v):
    m = jnp.max(s, axis=-1, keepdims=True)
    p = jnp.exp2(s - m)
    l = jnp.sum(p, axis=-1, keepdims=True)
    return jnp.dot(p.astype(BF16), v, preferred_element_type=F32) / l


_NT = (((1,), (1,)), ((), ()))


def _gqa_kernel(q_ref, k_ref, v_ref, o_ref, *, bounded):
    k = k_ref[0]
    v = v_ref[0]

    def score(g):
        return lax.dot_general(q_ref[0, :, g * HEAD_DIM:(g + 1) * HEAD_DIM], k, _NT, preferred_element_type=F32)

    s_next = score(0)
    for g in range(GQA_GROUP):
        s = s_next
        if g + 1 < GQA_GROUP:
            s_next = score(g + 1)
        if bounded:
            p = jnp.exp2(s)
            l = jnp.sum(p, axis=-1, keepdims=True)
            o = jnp.dot(p.astype(BF16), v, preferred_element_type=F32) / l
        else:
            o = _softmax_pv(s, v)
        o_ref[0, :, g * HEAD_DIM:(g + 1) * HEAD_DIM] = o.astype(BF16)


def _gqa(q, k, v, *, bounded, tq=512):
    b, s, _ = q.shape
    assert s % tq == 0
    gw = GQA_GROUP * HEAD_DIM
    return pl.pallas_call(
        functools.partial(_gqa_kernel, bounded=bounded),
        out_shape=jax.ShapeDtypeStruct((b, s, WIDTH_A), BF16),
        grid=(b, N_KV_A, s // tq),
        in_specs=[
            pl.BlockSpec((1, tq, gw), lambda bi, kh, qi: (bi, qi, kh)),
            pl.BlockSpec((1, s, HEAD_DIM), lambda bi, kh, qi: (bi, 0, kh)),
            pl.BlockSpec((1, s, HEAD_DIM), lambda bi, kh, qi: (bi, 0, kh)),
        ],
        out_specs=pl.BlockSpec((1, tq, gw), lambda bi, kh, qi: (bi, qi, kh)),
        compiler_params=_params(("parallel", "parallel", "arbitrary"), 40),
        name="gqa_bounded" if bounded else "gqa",
    )(q, k, v)


def _nbr_window_start(j, rows):
    return jnp.clip(NBR_Q_ROWS * j - WIN_ROWS // 2, 0, rows - NBR_KEY_ROWS)


def _nbr_block_plans(rows):
    plans = {}
    for j in range(rows // NBR_Q_ROWS):
        ws = int(np.clip(NBR_Q_ROWS * j - WIN_ROWS // 2, 0, rows - NBR_KEY_ROWS))
        lo = tuple(int(np.clip(NBR_Q_ROWS * j + qr - WIN_ROWS // 2, 0, rows - WIN_ROWS)) - ws
                   for qr in range(NBR_Q_ROWS))
        d0 = ws - NBR_Q_ROWS * j + (WIN_ROWS - 1) + NBR_DR_PAD
        assert min(lo) >= 0 and max(lo) + WIN_ROWS <= NBR_KEY_ROWS
        plans.setdefault((lo, d0), []).append(j)
    return plans


def _nbr_block(q_ref, k_ref, v_ref, cb_ref, o_ref, ws, lo, d0):
    pair_w = 2 * GRID_W
    pa = min(lo) // 2
    pb = (max(lo) + WIN_ROWS - 1) // 2 + 1
    n_keys = (pb - pa) * pair_w
    start = pl.multiple_of(ws * GRID_W + pa * pair_w, pair_w)
    lane = lax.broadcasted_iota(jnp.int32, (GRID_W, pair_w), 1)
    zeros = jnp.zeros((GRID_W, pair_w), BF16)
    heads = [slice(h * HEAD_DIM, (h + 1) * HEAD_DIM) for h in range(N_HEADS_B)]

    def score(cols):
        return lax.dot_general(q_ref[0, :, cols], k_ref[0, pl.ds(start, n_keys), cols], _NT,
                               preferred_element_type=F32)

    lookahead = len(set(lo)) > 1
    s_next = score(heads[0])
    for h, cols in enumerate(heads):
        s = s_next if lookahead or h == 0 else score(cols)
        p_rows, l_rows = [], []
        for qr in range(NBR_Q_ROWS):
            first, last = lo[qr], lo[qr] + WIN_ROWS - 1
            tiles = []
            for a in range(first // 2, last // 2 + 1):
                t = (s[qr * GRID_W:(qr + 1) * GRID_W, (a - pa) * pair_w:(a - pa + 1) * pair_w]
                     + cb_ref[h, d0 + 2 * a - qr])
                if 2 * a < first:
                    t = jnp.where(lane < GRID_W, MASKED, t)
                if 2 * a + 1 > last:
                    t = jnp.where(lane >= GRID_W, MASKED, t)
                tiles.append(t)
            m = jnp.max(functools.reduce(jnp.maximum, tiles), axis=-1, keepdims=True)
            ps = [jnp.exp2(t - m) for t in tiles]
            l_rows.append(jnp.sum(functools.reduce(jnp.add, ps), axis=-1, keepdims=True))
            p_rows.append(jnp.concatenate(
                [zeros] * (first // 2 - pa) + [p.astype(BF16) for p in ps] + [zeros] * (pb - 1 - last // 2),
                axis=1))
        p = jnp.concatenate(p_rows, axis=0)
        l = jnp.concatenate(l_rows, axis=0)
        if lookahead and h + 1 < N_HEADS_B:
            s_next = score(heads[h + 1])
        v = v_ref[0, pl.ds(start, n_keys), cols]
        o_ref[0, :, cols] = (jnp.dot(p, v, preferred_element_type=F32) / l).astype(BF16)


def _nbr_kernel(q_ref, k_ref, v_ref, cb_ref, o_ref, *, rows):
    j = pl.program_id(1)
    ws = _nbr_window_start(j, rows)
    for (lo, d0), members in _nbr_block_plans(rows).items():
        member = functools.reduce(jnp.logical_or, [j == m for m in members])
        pl.when(member)(functools.partial(_nbr_block, q_ref, k_ref, v_ref, cb_ref, o_ref, ws, lo, d0))


def _nbr(z, col_bias, q_col, k_col, v_col):
    b, s, _ = z.shape
    rows = s // GRID_W
    assert rows % NBR_Q_ROWS == 0 and rows >= NBR_KEY_ROWS
    return pl.pallas_call(
        functools.partial(_nbr_kernel, rows=rows),
        out_shape=jax.ShapeDtypeStruct((b, s, WIDTH_B), BF16),
        grid=(b, rows // NBR_Q_ROWS),
        in_specs=[
            pl.BlockSpec((1, NBR_Q, WIDTH_B), lambda bi, j: (bi, j, q_col)),
            pl.BlockSpec((1, s, WIDTH_B), lambda bi, j: (bi, 0, k_col)),
            pl.BlockSpec((1, s, WIDTH_B), lambda bi, j: (bi, 0, v_col)),
            _resident(col_bias.shape),
        ],
        out_specs=pl.BlockSpec((1, NBR_Q, WIDTH_B), lambda bi, j: (bi, j, 0)),
        compiler_params=_params(("parallel", "arbitrary"), 48),
        name="nbr",
    )(z, z, z, col_bias)


def _nbr_col_bias(rpb):
    n_dr, n_dc = 2 * WIN_ROWS - 1, 2 * WIN_COLS - 1
    c = np.arange(GRID_W)[:, None]
    kc = np.arange(GRID_W)[None, :]
    cs = np.clip(c - WIN_COLS // 2, 0, GRID_W - WIN_COLS)
    col_valid = (kc >= cs) & (kc < cs + WIN_COLS)
    onehot = ((kc - c + (WIN_COLS - 1))[None] == np.arange(n_dc)[:, None, None]) & col_valid[None]
    onehot = jnp.asarray(onehot.reshape(n_dc, GRID_W * GRID_W), F32)
    tm = jnp.einsum("hrd,dn->hrn", rpb * LOG2_E, onehot, precision=lax.Precision.HIGHEST)
    tm = jnp.where(col_valid.reshape(-1), tm, MASKED).reshape(-1, n_dr, GRID_W, GRID_W)
    hi_pad = NBR_DR_SLOTS + 1 - NBR_DR_PAD - n_dr
    tm = jnp.pad(tm, ((0, 0), (NBR_DR_PAD, hi_pad), (0, 0), (0, 0)), constant_values=MASKED)
    return jnp.concatenate([tm[:, :-1], tm[:, 1:]], axis=-1)


MERGE_TN = 512


def _merge_kernel(oa_ref, ob_ref, ga_ref, gb_ref, h_ref, wa_ref, wb_ref, wo_ref, out_ref, m_ref):
    oa = oa_ref[...]
    ob = ob_ref[...]
    for c in range(D_MODEL // MERGE_TN):
        cols = slice(c * MERGE_TN, (c + 1) * MERGE_TN)
        a = jnp.dot(oa, wa_ref[:, cols], preferred_element_type=F32)
        b = jnp.dot(ob, wb_ref[:, cols], preferred_element_type=F32)
        merged = (jax.nn.sigmoid(ga_ref[:, cols].astype(F32)) * a
                  + jax.nn.sigmoid(gb_ref[:, cols].astype(F32)) * b)
        m_ref[:, cols] = merged.astype(BF16)
    out_ref[...] = h_ref[...] + jnp.dot(m_ref[...], wo_ref[...], preferred_element_type=F32)


def _merge(oa, ob, z, ga_col, gb_col, h, wa, wb, wo, *, tm=512):
    t, d = h.shape
    assert t % tm == 0 and d == D_MODEL
    return pl.pallas_call(
        _merge_kernel,
        out_shape=jax.ShapeDtypeStruct((t, d), F32),
        grid=(t // tm,),
        in_specs=[
            pl.BlockSpec((tm, WIDTH_A), lambda i: (i, 0)),
            pl.BlockSpec((tm, WIDTH_B), lambda i: (i, 0)),
            pl.BlockSpec((tm, d), lambda i: (i, ga_col)),
            pl.BlockSpec((tm, d), lambda i: (i, gb_col)),
            pl.BlockSpec((tm, d), lambda i: (i, 0)),
            _resident((WIDTH_A, d)),
            _resident((WIDTH_B, d)),
            _resident((d, d)),
        ],
        out_specs=pl.BlockSpec((tm, d), lambda i: (i, 0)),
        scratch_shapes=[pltpu.VMEM((tm, d), BF16)],
        compiler_params=_params(("parallel",), 56),
        name="merge",
    )(oa, ob, z, z, h, wa, wb, wo)


def _ple_kernel(h_ref, n_ref, p_ref, wg_ref, wp_ref, gf_ref, y_ref):
    gate = jax.nn.sigmoid(jnp.dot(n_ref[...], wg_ref[...], preferred_element_type=F32))
    emb = jnp.dot(p_ref[...].astype(BF16), wp_ref[...], preferred_element_type=F32)
    y_ref[...] = _rms(h_ref[...] + gate * emb) * gf_ref[...]


def _ple(h, n, row0, p, wg, wp, gf, *, tm=512):
    t, d = p.shape[0], h.shape[1]
    assert t % tm == 0 and row0 % tm == 0
    tile0 = row0 // tm
    return pl.pallas_call(
        _ple_kernel,
        out_shape=jax.ShapeDtypeStruct((t, d), F32),
        grid=(t // tm,),
        in_specs=[
            pl.BlockSpec((tm, d), lambda i: (i + tile0, 0)),
            pl.BlockSpec((tm, d), lambda i: (i + tile0, 0)),
            pl.BlockSpec((tm, D_PLE), lambda i: (i, 0)),
            _resident((d, d)),
            _resident((D_PLE, d)),
            _resident((1, d)),
        ],
        out_specs=pl.BlockSpec((tm, d), lambda i: (i, 0)),
        compiler_params=_params(("parallel",), 48),
        name="ple",
    )(h, n, p, wg, wp, gf)


CAST_BLOCK_BYTES = 4 * MIB


def _cast_kernel(w_ref, o_ref, *, scale):
    w = w_ref[...]
    o_ref[...] = (w if scale == 1.0 else w * scale).astype(BF16)


def _to_bf16(w, layer, scale=1.0):
    _, r, c = w.shape
    target = max(16, CAST_BLOCK_BYTES // (4 * c))
    rows = next(n for n in range(min(r, target) // 16 * 16, 0, -16) if r % n == 0)
    return pl.pallas_call(
        functools.partial(_cast_kernel, scale=scale),
        out_shape=jax.ShapeDtypeStruct((r, c), BF16),
        grid=(r // rows,),
        in_specs=[pl.BlockSpec((None, rows, c), lambda i: (layer, i, 0))],
        out_specs=pl.BlockSpec((rows, c), lambda i: (i, 0)),
        compiler_params=_params(("parallel",), 32),
        name="to_bf16",
    )(w)


def _pack_gate_up_kernel(*refs, tf):
    o_ref = refs[-1]
    for slot in range(len(refs) // 2):
        @pl.when(pl.program_id(0) == slot)
        def _(g_ref=refs[2 * slot], u_ref=refs[2 * slot + 1]):
            for j in range(o_ref.shape[0]):
                o_ref[j, :, :tf] = g_ref[:, j * tf:(j + 1) * tf].astype(BF16)
                o_ref[j, :, tf:] = u_ref[:, j * tf:(j + 1) * tf].astype(BF16)


def _pack_gate_up(gate_up_pairs, layer, *, tf, rows=64):
    _, r, c = gate_up_pairs[0][0].shape
    assert r % rows == 0 and c % tf == 0
    n_chunks, n_blocks = c // tf, r // rows

    def in_spec(slot):
        return pl.BlockSpec((None, rows, c), lambda s, i: (layer, jnp.where(s == slot, i, n_blocks - 1), 0))

    weights = [w for pair in gate_up_pairs for w in pair]
    assert all(w.shape == weights[0].shape for w in weights)
    return pl.pallas_call(
        functools.partial(_pack_gate_up_kernel, tf=tf),
        out_shape=jax.ShapeDtypeStruct((len(gate_up_pairs), n_chunks, r, 2 * tf), BF16),
        grid=(len(gate_up_pairs), n_blocks),
        in_specs=[in_spec(slot) for slot in range(len(gate_up_pairs)) for _ in range(2)],
        out_specs=pl.BlockSpec((None, n_chunks, rows, 2 * tf), lambda s, i: (s, 0, i, 0)),
        compiler_params=_params(("arbitrary", "arbitrary"), 32),
        name="pack_gate_up",
    )(*weights)


def _rotary_tables(seq, gain_q, gain_k):
    n_freq = HEAD_DIM // 4
    inv_freq = ROPE_THETA ** (-jnp.arange(n_freq, dtype=F32) / n_freq)
    t = jnp.arange(seq)
    row = (t // GRID_W).astype(F32)
    col = (t % GRID_W).astype(F32)
    ang = jnp.concatenate([row[:, None] * inv_freq[None], col[:, None] * inv_freq[None]], axis=-1)
    cos = jnp.concatenate([jnp.cos(ang), jnp.cos(ang)], axis=-1)
    sin = jnp.concatenate([-jnp.sin(ang), jnp.sin(ang)], axis=-1)
    return jnp.concatenate([cos * g if part == 0 else sin * jnp.roll(g, HEAD_DIM // 2)
                            for g in (gain_q, gain_k) for part in (0, 1)], axis=-1)


_W_IN_SEGMENTS = dict(q_b=QKV_A_WIDTH, k_b=QKV_A_WIDTH + WIDTH_B, v_b=QKV_A_WIDTH + 2 * WIDTH_B,
                      g_a=QKV_A_WIDTH + 3 * WIDTH_B, g_b=QKV_A_WIDTH + 3 * WIDTH_B + D_MODEL)
_PROJ_LAYOUT = (("g_a", D_MODEL), ("g_b", D_MODEL), ("q_b", WIDTH_B), ("k_b", WIDTH_B), ("v_b", WIDTH_B))
_PROJ_SRC_COLS = tuple((_W_IN_SEGMENTS[name] + off) // PROJ_TN
                       for name, width in _PROJ_LAYOUT for off in range(0, width, PROJ_TN))
assert all(start % PROJ_TN == 0 for start in _W_IN_SEGMENTS.values())
_GA_COL, _GB_COL = 0, 1
_QB_COL, _KB_COL, _VB_COL = 4, 5, 6
assert 2 * D_MODEL == _QB_COL * WIDTH_B


def _prepare_weights(ffn1_norm, ffn1_w_gate, ffn1_w_up, ffn1_w_down, mix_norm, w_in, q_norm, k_norm,
                     w_branch_a, w_branch_b, w_out, ffn2_norm, ffn2_w_gate, ffn2_w_up, ffn2_w_down,
                     ple_norm, w_ple_gate, w_ple_proj, final_norm, layer):
    i = layer
    half = np.concatenate([np.arange(0, HEAD_DIM, 2), np.arange(1, HEAD_DIM, 2)])
    w_in_bf16 = _to_bf16(w_in, i)
    w_rot = _rotary_weight_layout(w_in_bf16)
    w_gate_up = _pack_gate_up([(ffn1_w_gate, ffn1_w_up), (ffn2_w_gate, ffn2_w_up)], i, tf=FFN_TF)
    q_scale = ATTN_SCALE * LOG2_E
    proj_scale = jnp.concatenate([jnp.full((width,), q_scale if name == "q_b" else 1.0, F32)
                                  for name, width in _PROJ_LAYOUT])[None]
    rot_gains = ((q_norm[i] * q_scale)[half], k_norm[i][half])
    gqa_score_bound = (HEAD_DIM * q_scale * BF16_NORM_MARGIN
                       * jnp.max(jnp.abs(q_norm[i])) * jnp.max(jnp.abs(k_norm[i])))
    return dict(
        ffn1=(ffn1_norm[i][None], w_gate_up, 0, _to_bf16(ffn1_w_down, i, FFN_RESIDUAL_SCALE), mix_norm[i][None]),
        w_rot=w_rot, rot_gains=rot_gains, w_in=w_in_bf16, proj_scale=proj_scale, gqa_score_bound=gqa_score_bound,
        merge=(_to_bf16(w_branch_a, i), _to_bf16(w_branch_b, i), _to_bf16(w_out, i)),
        ffn2=(ffn2_norm[i][None], w_gate_up, 1, _to_bf16(ffn2_w_down, i, FFN_RESIDUAL_SCALE), ple_norm[i][None]),
        ple=(_to_bf16(w_ple_gate, i), _to_bf16(w_ple_proj, i), final_norm[None]),
    )


def _encoder(xs, ps, wts, rot, col_bias):
    s, d = xs[0].shape[1:]
    assert all(x.shape[1:] == (s, d) for x in xs)
    b = sum(x.shape[0] for x in xs)
    t = b * s
    h1, u = _ffn([x.reshape(-1, d) for x in xs], *wts["ffn1"])
    q_a, k_a, v_a = _qkv_a(u, wts["w_rot"], wts["w_in"], rot, s)
    z = _proj(u, wts["w_in"], _PROJ_SRC_COLS, wts["proj_scale"])
    qkv = (q_a.reshape(b, s, -1), k_a.reshape(b, s, -1), v_a.reshape(b, s, -1))
    o_a = lax.cond(wts["gqa_score_bound"] <= GQA_SAFE_LOG2_SCORE,
                   functools.partial(_gqa, bounded=True), functools.partial(_gqa, bounded=False), *qkv)
    o_b = _nbr(z.reshape(b, s, -1), col_bias, _QB_COL, _KB_COL, _VB_COL)
    h2 = _merge(o_a.reshape(t, -1), o_b.reshape(t, -1), z, _GA_COL, _GB_COL, h1, *wts["merge"])
    h3, n = _ffn([h2], *wts["ffn2"])
    row_starts = np.cumsum([0] + [x.shape[0] * s for x in xs])
    return tuple(_ple(h3, n, int(row0), p.reshape(-1, p.shape[-1]), *wts["ple"]).reshape(x.shape)
                 for row0, p, x in zip(row_starts, ps, xs))


def kernel(x_prompt, x_sample, p_prompt, p_sample, ffn1_norm, ffn1_w_gate, ffn1_w_up, ffn1_w_down, mix_norm, w_in, q_norm, k_norm, nat_rpb, w_branch_a, w_branch_b, w_out, ffn2_norm, ffn2_w_gate, ffn2_w_up, ffn2_w_down, ple_norm, w_ple_gate, w_ple_proj, final_norm):
    assert ffn1_norm.shape[0] == 1, "single-layer encoder"
    wts = _prepare_weights(ffn1_norm, ffn1_w_gate, ffn1_w_up, ffn1_w_down, mix_norm, w_in, q_norm, k_norm,
                           w_branch_a, w_branch_b, w_out, ffn2_norm, ffn2_w_gate, ffn2_w_up, ffn2_w_down,
                           ple_norm, w_ple_gate, w_ple_proj, final_norm, 0)
    col_bias = _nbr_col_bias(nat_rpb[0])
    groups = ((x_prompt, p_prompt[0]), (x_sample, p_sample[0]))
    outs = {}
    for seq in sorted({x.shape[1] for x, _ in groups}):
        members = [k for k, (x, _) in enumerate(groups) if x.shape[1] == seq]
        rot = _rotary_tables(seq, *wts["rot_gains"])
        ys = _encoder([groups[k][0] for k in members], [groups[k][1] for k in members], wts, rot, col_bias)
        outs.update(zip(members, ys))
    return tuple(outs[k] for k in range(len(groups)))
```

```python
import functools
import math

import numpy as np
import jax
import jax.numpy as jnp
from jax import lax
from jax.experimental import pallas as pl
from jax.experimental.pallas import tpu as pltpu

F32 = jnp.float32
BF16 = jnp.bfloat16

D_MODEL = 2048
HEAD_DIM = 128
N_HEADS_A = 8
N_KV_A = 2
GQA_GROUP = N_HEADS_A // N_KV_A
N_HEADS_B = 8
WIDTH_A = N_HEADS_A * HEAD_DIM
KV_WIDTH_A = N_KV_A * HEAD_DIM
WIDTH_B = N_HEADS_B * HEAD_DIM
ROT_WIDTH = WIDTH_A + KV_WIDTH_A
QKV_A_WIDTH = ROT_WIDTH + KV_WIDTH_A
D_FF = 5632
D_PLE = 256
GRID_W = 64
WIN_ROWS = 8
WIN_COLS = 16
ROPE_THETA = 10000.0
EPS = 1e-6
ATTN_SCALE = HEAD_DIM ** -0.5
LOG2_E = math.log2(math.e)
MASKED = -1e30
GQA_SAFE_LOG2_SCORE = 48.0
BF16_NORM_MARGIN = 1.02

NBR_Q_ROWS = 4
NBR_KEY_ROWS = 12
NBR_Q = NBR_Q_ROWS * GRID_W
NBR_KEYS = NBR_KEY_ROWS * GRID_W
NBR_DR_PAD = (NBR_KEY_ROWS - NBR_Q_ROWS) - (WIN_ROWS - 1) + (NBR_Q_ROWS - 1)
NBR_DR_SLOTS = NBR_DR_PAD + (WIN_ROWS - 1) + (NBR_KEY_ROWS - 2) + 1

PROJ_TN = 512

V7X_VMEM_BYTES = 64 * 1024 * 1024
MIB = 1024 * 1024


def _params(semantics, vmem_mib):
    assert vmem_mib * MIB < V7X_VMEM_BYTES
    return pltpu.CompilerParams(dimension_semantics=semantics, vmem_limit_bytes=vmem_mib * MIB)


def _resident(shape):
    zeros = (0,) * len(shape)
    return pl.BlockSpec(shape, lambda *_: zeros, pipeline_mode=pl.Buffered(1))


def _rms(x):
    return x * lax.rsqrt(jnp.mean(x * x, axis=-1, keepdims=True) + EPS)


FFN_RESIDUAL_SCALE = 0.5
FFN_TF = 512
ROW_CHUNK = 16
NORM_ROWS = 128
ROW_UNROLL_EXIT = 16

def _ffn_kernel(*refs, tile_starts):
    n_src = len(tile_starts) - 1
    x_srcs = refs[:n_src]
    g1_ref, wgu_ref, wd_ref, g2_ref, h_ref, n_ref, x_buf, xn_ref, x_sem = refs[n_src:]
    i = pl.program_id(0)
    j = pl.program_id(1)
    tm = x_buf.shape[0]

    def x_copy(tile, wait):
        for s, x_hbm in enumerate(x_srcs):
            @pl.when((tile >= tile_starts[s]) & (tile < tile_starts[s + 1]))
            def _(s=s, x_hbm=x_hbm):
                row = pl.multiple_of((tile - tile_starts[s]) * tm, tm)
                copy = pltpu.make_async_copy(x_hbm.at[pl.ds(row, tm), :], x_buf, x_sem)
                copy.wait() if wait else copy.start()

    cur = i % 2
    has_next = i + 1 < pl.num_programs(0)
    n_chunks = tm // NORM_ROWS

    def norm_rows(slot, row0, n_rows):
        for r in range(0, n_rows, ROW_CHUNK):
            rows = pl.ds(row0 + r, ROW_CHUNK)
            xn_ref[slot, rows, :] = (_rms(x_buf[rows, :]) * g1_ref[...]).astype(BF16)

    @pl.when((i == 0) & (j == 0))
    def _():
        x_copy(0, wait=False)
        x_copy(0, wait=True)

        def first(c, carry):
            norm_rows(0, pl.multiple_of(c * NORM_ROWS, NORM_ROWS), NORM_ROWS)
            return carry

        lax.fori_loop(0, n_chunks, first, 0)

    @pl.when((j == 1) & has_next)
    def _():
        x_copy(i + 1, wait=False)

    @pl.when((j == 2) & has_next)
    def _():
        x_copy(i + 1, wait=True)

    tf = wd_ref.shape[0]

    def step(first_step, prepare_next):
        gu = jnp.dot(xn_ref[cur], wgu_ref[...], preferred_element_type=F32)
        g, u = gu[:, :tf], gu[:, tf:]
        a = (g * jax.nn.sigmoid(g) * u).astype(BF16)
        down = jnp.dot(a, wd_ref[...], preferred_element_type=F32)
        h_ref[...] = (x_buf[...] if first_step else h_ref[...]) + down
        if prepare_next:
            chunk = jnp.minimum(j - 2, n_chunks - 1)
            norm_rows(1 - cur, pl.multiple_of(chunk * NORM_ROWS, NORM_ROWS), NORM_ROWS)

    pl.when(j == 0)(functools.partial(step, True, False))
    pl.when(j == 1)(functools.partial(step, False, False))
    pl.when(j >= 2)(functools.partial(step, False, True))

    @pl.when(j == pl.num_programs(1) - 1)
    def _():
        def last(r, carry):
            rows = pl.ds(pl.multiple_of(r * ROW_CHUNK, ROW_CHUNK), ROW_CHUNK)
            n_ref[rows, :] = (_rms(h_ref[rows, :]) * g2_ref[...]).astype(BF16)
            return carry

        lax.fori_loop(0, tm // ROW_CHUNK, last, 0, unroll=ROW_UNROLL_EXIT)


def _ffn(xs, g1, w_gate_up, slot, wd, g2, *, tm=1024):
    d = xs[0].shape[1]
    dff, tf = wd.shape[0], FFN_TF
    assert all(x.shape[0] % tm == 0 and x.shape[1] == d for x in xs)
    assert w_gate_up.shape[1:] == (dff // tf, d, 2 * tf)
    assert tm % NORM_ROWS == 0 and tm // NORM_ROWS <= dff // tf - 2
    tile_starts = tuple(int(n) for n in np.cumsum([0] + [x.shape[0] // tm for x in xs]))
    t = tile_starts[-1] * tm
    return pl.pallas_call(
        functools.partial(_ffn_kernel, tile_starts=tile_starts),
        out_shape=(jax.ShapeDtypeStruct((t, d), F32), jax.ShapeDtypeStruct((t, d), BF16)),
        grid=(t // tm, dff // tf),
        in_specs=[pl.BlockSpec(memory_space=pl.ANY)] * len(xs) + [
            pl.BlockSpec((1, d), lambda i, j: (0, 0)),
            pl.BlockSpec((None, None, d, 2 * tf), lambda i, j: (slot, j, 0, 0)),
            pl.BlockSpec((tf, d), lambda i, j: (j, 0)),
            pl.BlockSpec((1, d), lambda i, j: (0, 0)),
        ],
        out_specs=(
            pl.BlockSpec((tm, d), lambda i, j: (i, 0)),
            pl.BlockSpec((tm, d), lambda i, j: (i, 0)),
        ),
        scratch_shapes=[pltpu.VMEM((tm, d), F32), pltpu.VMEM((2, tm, d), BF16), pltpu.SemaphoreType.DMA(())],
        compiler_params=_params(("arbitrary", "arbitrary"), 60),
        name="ffn",
    )(*xs, g1, w_gate_up, wd, g2)


def _permute_heads_kernel(w_ref, perm_ref, o_ref):
    o_ref[...] = jnp.dot(w_ref[...], perm_ref[...], preferred_element_type=F32).astype(BF16)


def _rotary_weight_layout(w_in):
    d = w_in.shape[0]
    half = np.concatenate([np.arange(0, HEAD_DIM, 2), np.arange(1, HEAD_DIM, 2)])
    perm = jnp.asarray(np.arange(HEAD_DIM)[:, None] == half[None, :], BF16)
    return pl.pallas_call(
        _permute_heads_kernel,
        out_shape=jax.ShapeDtypeStruct((d, ROT_WIDTH), BF16),
        grid=(ROT_WIDTH // HEAD_DIM,),
        in_specs=[pl.BlockSpec((d, HEAD_DIM), lambda h: (0, h)), _resident((HEAD_DIM, HEAD_DIM))],
        out_specs=pl.BlockSpec((d, HEAD_DIM), lambda h: (0, h)),
        compiler_params=_params(("parallel",), 16),
        name="rotary_weight_layout",
    )(w_in, perm)


def _qkv_a_kernel(u_ref, wr_ref, wv_ref, rot_ref, q_ref, k_ref, v_ref, z_buf, v_buf, *, n_tiles):
    i = pl.program_id(0)

    def project(slot):
        u = u_ref[...]
        z_buf[slot] = jnp.dot(u, wr_ref[...], preferred_element_type=F32)
        v_buf[slot] = jnp.dot(u, wv_ref[...], preferred_element_type=F32).astype(BF16)

    def finish(slot):
        for hh in range(N_HEADS_A + N_KV_A):
            zh = z_buf[slot, :, hh * HEAD_DIM:(hh + 1) * HEAD_DIM]
            tab = 0 if hh < N_HEADS_A else 2
            cos_g = rot_ref[:, tab * HEAD_DIM:(tab + 1) * HEAD_DIM]
            sin_g = rot_ref[:, (tab + 1) * HEAD_DIM:(tab + 2) * HEAD_DIM]
            r = lax.rsqrt(jnp.mean(zh * zh, axis=-1, keepdims=True) + EPS)
            out = ((zh * cos_g + pltpu.roll(zh, HEAD_DIM // 2, 1) * sin_g) * r).astype(BF16)
            if hh < N_HEADS_A:
                q_ref[:, hh * HEAD_DIM:(hh + 1) * HEAD_DIM] = out
            else:
                kk = hh - N_HEADS_A
                k_ref[:, kk * HEAD_DIM:(kk + 1) * HEAD_DIM] = out
        v_ref[...] = v_buf[slot]

    @pl.when(i == 0)
    def _():
        project(0)

    for parity in range(2):
        @pl.when((i > 0) & (i < n_tiles) & (i % 2 == parity))
        def _(parity=parity):
            project(parity)
            finish(1 - parity)

    @pl.when(i == n_tiles)
    def _():
        finish((n_tiles - 1) % 2)


def _qkv_a(u, w_rot, w_in, rot, seq, *, tm=512):
    t, d = u.shape
    assert t % tm == 0 and seq % tm == 0 and ROT_WIDTH % KV_WIDTH_A == 0
    n_tiles, seq_blocks = t // tm, seq // tm

    def lagged(i):
        return jnp.maximum(i - 1, 0)

    return pl.pallas_call(
        functools.partial(_qkv_a_kernel, n_tiles=n_tiles),
        out_shape=(
            jax.ShapeDtypeStruct((t, WIDTH_A), BF16),
            jax.ShapeDtypeStruct((t, KV_WIDTH_A), BF16),
            jax.ShapeDtypeStruct((t, KV_WIDTH_A), BF16),
        ),
        grid=(n_tiles + 1,),
        in_specs=[
            pl.BlockSpec((tm, d), lambda i: (jnp.minimum(i, n_tiles - 1), 0)),
            _resident((d, ROT_WIDTH)),
            pl.BlockSpec((d, KV_WIDTH_A), lambda i: (0, ROT_WIDTH // KV_WIDTH_A), pipeline_mode=pl.Buffered(1)),
            pl.BlockSpec((tm, 4 * HEAD_DIM), lambda i: (lagged(i) % seq_blocks, 0)),
        ],
        out_specs=(
            pl.BlockSpec((tm, WIDTH_A), lambda i: (lagged(i), 0)),
            pl.BlockSpec((tm, KV_WIDTH_A), lambda i: (lagged(i), 0)),
            pl.BlockSpec((tm, KV_WIDTH_A), lambda i: (lagged(i), 0)),
        ),
        scratch_shapes=[pltpu.VMEM((2, tm, ROT_WIDTH), F32), pltpu.VMEM((2, tm, KV_WIDTH_A), BF16)],
        compiler_params=_params(("arbitrary",), 40),
        name="qkv_a",
    )(u, w_rot, w_in, rot)


def _proj_kernel(u_ref, w_ref, scale_ref, z_ref):
    z = jnp.dot(u_ref[...], w_ref[...], preferred_element_type=F32)
    z_ref[...] = (z * scale_ref[...]).astype(BF16)


def _proj(u, w, src_cols, col_scale, *, tm=2048, tn=PROJ_TN):
    t, d = u.shape
    n = len(src_cols) * tn
    assert t % tm == 0 and col_scale.shape == (1, n)

    def w_block(i, j):
        src = src_cols[0]
        for k in range(1, len(src_cols)):
            src = jnp.where(j >= k, src_cols[k], src)
        return (0, src)

    return pl.pallas_call(
        _proj_kernel,
        out_shape=jax.ShapeDtypeStruct((t, n), BF16),
        grid=(t // tm, len(src_cols)),
        in_specs=[
            pl.BlockSpec((tm, d), lambda i, j: (i, 0)),
            pl.BlockSpec((d, tn), w_block),
            pl.BlockSpec((1, tn), lambda i, j: (0, j)),
        ],
        out_specs=pl.BlockSpec((tm, tn), lambda i, j: (i, j)),
        compiler_params=_params(("parallel", "arbitrary"), 40),
        name="proj",
    )(u, w, col_scale)


def _softmax_pv(s, v):
    m = jnp.max(s, axis=-1, keepdims=True)
    p = jnp.exp2(s - m)
    l = jnp.sum(p, axis=-1, keepdims=True)
    return jnp.dot(p.astype(BF16), v, preferred_element_type=F32) / l


_NT = (((1,), (1,)), ((), ()))


def _gqa_kernel(q_ref, k_ref, v_ref, o_ref, *, bounded):
    k = k_ref[0]
    v = v_ref[0]

    def score(g):
        return lax.dot_general(q_ref[0, :, g * HEAD_DIM:(g + 1) * HEAD_DIM], k, _NT, preferred_element_type=F32)

    s_next = score(0)
    for g in range(GQA_GROUP):
        s = s_next
        if g + 1 < GQA_GROUP:
            s_next = score(g + 1)
        if bounded:
            p = jnp.exp2(s)
            l = jnp.sum(p, axis=-1, keepdims=True)
            o = jnp.dot(p.astype(BF16), v, preferred_element_type=F32) / l
        else:
            o = _softmax_pv(s, v)
        o_ref[0, :, g * HEAD_DIM:(g + 1) * HEAD_DIM] = o.astype(BF16)


def _gqa(q, k, v, *, bounded, tq=512):
    b, s, _ = q.shape
    assert s % tq == 0
    gw = GQA_GROUP * HEAD_DIM
    return pl.pallas_call(
        functools.partial(_gqa_kernel, bounded=bounded),
        out_shape=jax.ShapeDtypeStruct((b, s, WIDTH_A), BF16),
        grid=(b, N_KV_A, s // tq),
        in_specs=[
            pl.BlockSpec((1, tq, gw), lambda bi, kh, qi: (bi, qi, kh)),
            pl.BlockSpec((1, s, HEAD_DIM), lambda bi, kh, qi: (bi, 0, kh)),
            pl.BlockSpec((1, s, HEAD_DIM), lambda bi, kh, qi: (bi, 0, kh)),
        ],
        out_specs=pl.BlockSpec((1, tq, gw), lambda bi, kh, qi: (bi, qi, kh)),
        compiler_params=_params(("parallel", "parallel", "arbitrary"), 40),
        name="gqa_bounded" if bounded else "gqa",
    )(q, k, v)


def _nbr_window_start(j, rows):
    return jnp.clip(NBR_Q_ROWS * j - WIN_ROWS // 2, 0, rows - NBR_KEY_ROWS)


def _nbr_block_plans(rows):
    plans = {}
    for j in range(rows // NBR_Q_ROWS):
        ws = int(np.clip(NBR_Q_ROWS * j - WIN_ROWS // 2, 0, rows - NBR_KEY_ROWS))
        lo = tuple(int(np.clip(NBR_Q_ROWS * j + qr - WIN_ROWS // 2, 0, rows - WIN_ROWS)) - ws
                   for qr in range(NBR_Q_ROWS))
        d0 = ws - NBR_Q_ROWS * j + (WIN_ROWS - 1) + NBR_DR_PAD
        assert min(lo) >= 0 and max(lo) + WIN_ROWS <= NBR_KEY_ROWS
        plans.setdefault((lo, d0), []).append(j)
    return plans


def _nbr_block(q_ref, k_ref, v_ref, cb_ref, o_ref, ws, lo, d0):
    pair_w = 2 * GRID_W
    pa = min(lo) // 2
    pb = (max(lo) + WIN_ROWS - 1) // 2 + 1
    n_keys = (pb - pa) * pair_w
    start = pl.multiple_of(ws * GRID_W + pa * pair_w, pair_w)
    lane = lax.broadcasted_iota(jnp.int32, (GRID_W, pair_w), 1)
    zeros = jnp.zeros((GRID_W, pair_w), BF16)
    heads = [slice(h * HEAD_DIM, (h + 1) * HEAD_DIM) for h in range(N_HEADS_B)]

    def score(cols):
        return lax.dot_general(q_ref[0, :, cols], k_ref[0, pl.ds(start, n_keys), cols], _NT,
                               preferred_element_type=F32)

    lookahead = len(set(lo)) > 1
    s_next = score(heads[0])
    for h, cols in enumerate(heads):
        s = s_next if lookahead or h == 0 else score(cols)
        p_rows, l_rows = [], []
        for qr in range(NBR_Q_ROWS):
            first, last = lo[qr], lo[qr] + WIN_ROWS - 1
            tiles = []
            for a in range(first // 2, last // 2 + 1):
                t = (s[qr * GRID_W:(qr + 1) * GRID_W, (a - pa) * pair_w:(a - pa + 1) * pair_w]
                     + cb_ref[h, d0 + 2 * a - qr])
                if 2 * a < first:
                    t = jnp.where(lane < GRID_W, MASKED, t)
                if 2 * a + 1 > last:
                    t = jnp.where(lane >= GRID_W, MASKED, t)
                tiles.append(t)
            m = jnp.max(functools.reduce(jnp.maximum, tiles), axis=-1, keepdims=True)
            ps = [jnp.exp2(t - m) for t in tiles]
            l_rows.append(jnp.sum(functools.reduce(jnp.add, ps), axis=-1, keepdims=True))
            p_rows.append(jnp.concatenate(
                [zeros] * (first // 2 - pa) + [p.astype(BF16) for p in ps] + [zeros] * (pb - 1 - last // 2),
                axis=1))
        p = jnp.concatenate(p_rows, axis=0)
        l = jnp.concatenate(l_rows, axis=0)
        if lookahead and h + 1 < N_HEADS_B:
            s_next = score(heads[h + 1])
        v = v_ref[0, pl.ds(start, n_keys), cols]
        o_ref[0, :, cols] = (jnp.dot(p, v, preferred_element_type=F32) / l).astype(BF16)


def _nbr_kernel(q_ref, k_ref, v_ref, cb_ref, o_ref, *, rows):
    j = pl.program_id(1)
    ws = _nbr_window_start(j, rows)
    for (lo, d0), members in _nbr_block_plans(rows).items():
        member = functools.reduce(jnp.logical_or, [j == m for m in members])
        pl.when(member)(functools.partial(_nbr_block, q_ref, k_ref, v_ref, cb_ref, o_ref, ws, lo, d0))


def _nbr(z, col_bias, q_col, k_col, v_col):
    b, s, _ = z.shape
    rows = s // GRID_W
    assert rows % NBR_Q_ROWS == 0 and rows >= NBR_KEY_ROWS
    return pl.pallas_call(
        functools.partial(_nbr_kernel, rows=rows),
        out_shape=jax.ShapeDtypeStruct((b, s, WIDTH_B), BF16),
        grid=(b, rows // NBR_Q_ROWS),
        in_specs=[
            pl.BlockSpec((1, NBR_Q, WIDTH_B), lambda bi, j: (bi, j, q_col)),
            pl.BlockSpec((1, s, WIDTH_B), lambda bi, j: (bi, 0, k_col)),
            pl.BlockSpec((1, s, WIDTH_B), lambda bi, j: (bi, 0, v_col)),
            _resident(col_bias.shape),
        ],
        out_specs=pl.BlockSpec((1, NBR_Q, WIDTH_B), lambda bi, j: (bi, j, 0)),
        compiler_params=_params(("parallel", "arbitrary"), 48),
        name="nbr",
    )(z, z, z, col_bias)


def _nbr_col_bias(rpb):
    n_dr, n_dc = 2 * WIN_ROWS - 1, 2 * WIN_COLS - 1
    c = np.arange(GRID_W)[:, None]
    kc = np.arange(GRID_W)[None, :]
    cs = np.clip(c - WIN_COLS // 2, 0, GRID_W - WIN_COLS)
    col_valid = (kc >= cs) & (kc < cs + WIN_COLS)
    onehot = ((kc - c + (WIN_COLS - 1))[None] == np.arange(n_dc)[:, None, None]) & col_valid[None]
    onehot = jnp.asarray(onehot.reshape(n_dc, GRID_W * GRID_W), F32)
    tm = jnp.einsum("hrd,dn->hrn", rpb * LOG2_E, onehot, precision=lax.Precision.HIGHEST)
    tm = jnp.where(col_valid.reshape(-1), tm, MASKED).reshape(-1, n_dr, GRID_W, GRID_W)
    hi_pad = NBR_DR_SLOTS + 1 - NBR_DR_PAD - n_dr
    tm = jnp.pad(tm, ((0, 0), (NBR_DR_PAD, hi_pad), (0, 0), (0, 0)), constant_values=MASKED)
    return jnp.concatenate([tm[:, :-1], tm[:, 1:]], axis=-1)


MERGE_TN = 512


def _merge_kernel(oa_ref, ob_ref, ga_ref, gb_ref, h_ref, wa_ref, wb_ref, wo_ref, out_ref, m_ref):
    oa = oa_ref[...]
    ob = ob_ref[...]
    for c in range(D_MODEL // MERGE_TN):
        cols = slice(c * MERGE_TN, (c + 1) * MERGE_TN)
        a = jnp.dot(oa, wa_ref[:, cols], preferred_element_type=F32)
        b = jnp.dot(ob, wb_ref[:, cols], preferred_element_type=F32)
        merged = (jax.nn.sigmoid(ga_ref[:, cols].astype(F32)) * a
                  + jax.nn.sigmoid(gb_ref[:, cols].astype(F32)) * b)
        m_ref[:, cols] = merged.astype(BF16)
    out_ref[...] = h_ref[...] + jnp.dot(m_ref[...], wo_ref[...], preferred_element_type=F32)


def _merge(oa, ob, z, ga_col, gb_col, h, wa, wb, wo, *, tm=512):
    t, d = h.shape
    assert t % tm == 0 and d == D_MODEL
    return pl.pallas_call(
        _merge_kernel,
        out_shape=jax.ShapeDtypeStruct((t, d), F32),
        grid=(t // tm,),
        in_specs=[
            pl.BlockSpec((tm, WIDTH_A), lambda i: (i, 0)),
            pl.BlockSpec((tm, WIDTH_B), lambda i: (i, 0)),
            pl.BlockSpec((tm, d), lambda i: (i, ga_col)),
            pl.BlockSpec((tm, d), lambda i: (i, gb_col)),
            pl.BlockSpec((tm, d), lambda i: (i, 0)),
            _resident((WIDTH_A, d)),
            _resident((WIDTH_B, d)),
            _resident((d, d)),
        ],
        out_specs=pl.BlockSpec((tm, d), lambda i: (i, 0)),
        scratch_shapes=[pltpu.VMEM((tm, d), BF16)],
        compiler_params=_params(("parallel",), 56),
        name="merge",
    )(oa, ob, z, z, h, wa, wb, wo)


def _ple_kernel(h_ref, n_ref, p_ref, wg_ref, wp_ref, gf_ref, y_ref):
    gate = jax.nn.sigmoid(jnp.dot(n_ref[...], wg_ref[...], preferred_element_type=F32))
    emb = jnp.dot(p_ref[...].astype(BF16), wp_ref[...], preferred_element_type=F32)
    y_ref[...] = _rms(h_ref[...] + gate * emb) * gf_ref[...]


def _ple(h, n, row0, p, wg, wp, gf, *, tm=512):
    t, d = p.shape[0], h.shape[1]
    assert t % tm == 0 and row0 % tm == 0
    tile0 = row0 // tm
    return pl.pallas_call(
        _ple_kernel,
        out_shape=jax.ShapeDtypeStruct((t, d), F32),
        grid=(t // tm,),
        in_specs=[
            pl.BlockSpec((tm, d), lambda i: (i + tile0, 0)),
            pl.BlockSpec((tm, d), lambda i: (i + tile0, 0)),
            pl.BlockSpec((tm, D_PLE), lambda i: (i, 0)),
            _resident((d, d)),
            _resident((D_PLE, d)),
            _resident((1, d)),
        ],
        out_specs=pl.BlockSpec((tm, d), lambda i: (i, 0)),
        compiler_params=_params(("parallel",), 48),
        name="ple",
    )(h, n, p, wg, wp, gf)


CAST_BLOCK_BYTES = 4 * MIB


def _cast_kernel(w_ref, o_ref, *, scale):
    w = w_ref[...]
    o_ref[...] = (w if scale == 1.0 else w * scale).astype(BF16)


def _to_bf16(w, layer, scale=1.0):
    _, r, c = w.shape
    target = max(16, CAST_BLOCK_BYTES // (4 * c))
    rows = next(n for n in range(min(r, target) // 16 * 16, 0, -16) if r % n == 0)
    return pl.pallas_call(
        functools.partial(_cast_kernel, scale=scale),
        out_shape=jax.ShapeDtypeStruct((r, c), BF16),
        grid=(r // rows,),
        in_specs=[pl.BlockSpec((None, rows, c), lambda i: (layer, i, 0))],
        out_specs=pl.BlockSpec((rows, c), lambda i: (i, 0)),
        compiler_params=_params(("parallel",), 32),
        name="to_bf16",
    )(w)


def _pack_gate_up_kernel(*refs, tf):
    o_ref = refs[-1]
    for slot in range(len(refs) // 2):
        @pl.when(pl.program_id(0) == slot)
        def _(g_ref=refs[2 * slot], u_ref=refs[2 * slot + 1]):
            for j in range(o_ref.shape[0]):
                o_ref[j, :, :tf] = g_ref[:, j * tf:(j + 1) * tf].astype(BF16)
                o_ref[j, :, tf:] = u_ref[:, j * tf:(j + 1) * tf].astype(BF16)


def _pack_gate_up(gate_up_pairs, layer, *, tf, rows=64):
    _, r, c = gate_up_pairs[0][0].shape
    assert r % rows == 0 and c % tf == 0
    n_chunks, n_blocks = c // tf, r // rows

    def in_spec(slot):
        return pl.BlockSpec((None, rows, c), lambda s, i: (layer, jnp.where(s == slot, i, n_blocks - 1), 0))

    weights = [w for pair in gate_up_pairs for w in pair]
    assert all(w.shape == weights[0].shape for w in weights)
    return pl.pallas_call(
        functools.partial(_pack_gate_up_kernel, tf=tf),
        out_shape=jax.ShapeDtypeStruct((len(gate_up_pairs), n_chunks, r, 2 * tf), BF16),
        grid=(len(gate_up_pairs), n_blocks),
        in_specs=[in_spec(slot) for slot in range(len(gate_up_pairs)) for _ in range(2)],
        out_specs=pl.BlockSpec((None, n_chunks, rows, 2 * tf), lambda s, i: (s, 0, i, 0)),
        compiler_params=_params(("arbitrary", "arbitrary"), 32),
        name="pack_gate_up",
    )(*weights)


def _rotary_tables(seq, gain_q, gain_k):
    n_freq = HEAD_DIM // 4
    inv_freq = ROPE_THETA ** (-jnp.arange(n_freq, dtype=F32) / n_freq)
    t = jnp.arange(seq)
    row = (t // GRID_W).astype(F32)
    col = (t % GRID_W).astype(F32)
    ang = jnp.concatenate([row[:, None] * inv_freq[None], col[:, None] * inv_freq[None]], axis=-1)
    cos = jnp.concatenate([jnp.cos(ang), jnp.cos(ang)], axis=-1)
    sin = jnp.concatenate([-jnp.sin(ang), jnp.sin(ang)], axis=-1)
    return jnp.concatenate([cos * g if part == 0 else sin * jnp.roll(g, HEAD_DIM // 2)
                            for g in (gain_q, gain_k) for part in (0, 1)], axis=-1)


_W_IN_SEGMENTS = dict(q_b=QKV_A_WIDTH, k_b=QKV_A_WIDTH + WIDTH_B, v_b=QKV_A_WIDTH + 2 * WIDTH_B,
                      g_a=QKV_A_WIDTH + 3 * WIDTH_B, g_b=QKV_A_WIDTH + 3 * WIDTH_B + D_MODEL)
_PROJ_LAYOUT = (("g_a", D_MODEL), ("g_b", D_MODEL), ("q_b", WIDTH_B), ("k_b", WIDTH_B), ("v_b", WIDTH_B))
_PROJ_SRC_COLS = tuple((_W_IN_SEGMENTS[name] + off) // PROJ_TN
                       for name, width in _PROJ_LAYOUT for off in range(0, width, PROJ_TN))
assert all(start % PROJ_TN == 0 for start in _W_IN_SEGMENTS.values())
_GA_COL, _GB_COL = 0, 1
_QB_COL, _KB_COL, _VB_COL = 4, 5, 6
assert 2 * D_MODEL == _QB_COL * WIDTH_B


def _prepare_weights(ffn1_norm, ffn1_w_gate, ffn1_w_up, ffn1_w_down, mix_norm, w_in, q_norm, k_norm,
                     w_branch_a, w_branch_b, w_out, ffn2_norm, ffn2_w_gate, ffn2_w_up, ffn2_w_down,
                     ple_norm, w_ple_gate, w_ple_proj, final_norm, layer):
    i = layer
    half = np.concatenate([np.arange(0, HEAD_DIM, 2), np.arange(1, HEAD_DIM, 2)])
    w_in_bf16 = _to_bf16(w_in, i)
    w_rot = _rotary_weight_layout(w_in_bf16)
    w_gate_up = _pack_gate_up([(ffn1_w_gate, ffn1_w_up), (ffn2_w_gate, ffn2_w_up)], i, tf=FFN_TF)
    q_scale = ATTN_SCALE * LOG2_E
    proj_scale = jnp.concatenate([jnp.full((width,), q_scale if name == "q_b" else 1.0, F32)
                                  for name, width in _PROJ_LAYOUT])[None]
    rot_gains = ((q_norm[i] * q_scale)[half], k_norm[i][half])
    gqa_score_bound = (HEAD_DIM * q_scale * BF16_NORM_MARGIN
                       * jnp.max(jnp.abs(q_norm[i])) * jnp.max(jnp.abs(k_norm[i])))
    return dict(
        ffn1=(ffn1_norm[i][None], w_gate_up, 0, _to_bf16(ffn1_w_down, i, FFN_RESIDUAL_SCALE), mix_norm[i][None]),
        w_rot=w_rot, rot_gains=rot_gains, w_in=w_in_bf16, proj_scale=proj_scale, gqa_score_bound=gqa_score_bound,
        merge=(_to_bf16(w_branch_a, i), _to_bf16(w_branch_b, i), _to_bf16(w_out, i)),
        ffn2=(ffn2_norm[i][None], w_gate_up, 1, _to_bf16(ffn2_w_down, i, FFN_RESIDUAL_SCALE), ple_norm[i][None]),
        ple=(_to_bf16(w_ple_gate, i), _to_bf16(w_ple_proj, i), final_norm[None]),
    )


def _encoder(xs, ps, wts, rot, col_bias):
    s, d = xs[0].shape[1:]
    assert all(x.shape[1:] == (s, d) for x in xs)
    b = sum(x.shape[0] for x in xs)
    t = b * s
    h1, u = _ffn([x.reshape(-1, d) for x in xs], *wts["ffn1"])
    q_a, k_a, v_a = _qkv_a(u, wts["w_rot"], wts["w_in"], rot, s)
    z = _proj(u, wts["w_in"], _PROJ_SRC_COLS, wts["proj_scale"])
    qkv = (q_a.reshape(b, s, -1), k_a.reshape(b, s, -1), v_a.reshape(b, s, -1))
    o_a = lax.cond(wts["gqa_score_bound"] <= GQA_SAFE_LOG2_SCORE,
                   functools.partial(_gqa, bounded=True), functools.partial(_gqa, bounded=False), *qkv)
    o_b = _nbr(z.reshape(b, s, -1), col_bias, _QB_COL, _KB_COL, _VB_COL)
    h2 = _merge(o_a.reshape(t, -1), o_b.reshape(t, -1), z, _GA_COL, _GB_COL, h1, *wts["merge"])
    h3, n = _ffn([h2], *wts["ffn2"])
    row_starts = np.cumsum([0] + [x.shape[0] * s for x in xs])
    return tuple(_ple(h3, n, int(row0), p.reshape(-1, p.shape[-1]), *wts["ple"]).reshape(x.shape)
                 for row0, p, x in zip(row_starts, ps, xs))


def kernel(x_prompt, x_sample, p_prompt, p_sample, ffn1_norm, ffn1_w_gate, ffn1_w_up, ffn1_w_down, mix_norm, w_in, q_norm, k_norm, nat_rpb, w_branch_a, w_branch_b, w_out, ffn2_norm, ffn2_w_gate, ffn2_w_up, ffn2_w_down, ple_norm, w_ple_gate, w_ple_proj, final_norm):
    assert ffn1_norm.shape[0] == 1, "single-layer encoder"
    wts = _prepare_weights(ffn1_norm, ffn1_w_gate, ffn1_w_up, ffn1_w_down, mix_norm, w_in, q_norm, k_norm,
                           w_branch_a, w_branch_b, w_out, ffn2_norm, ffn2_w_gate, ffn2_w_up, ffn2_w_down,
                           ple_norm, w_ple_gate, w_ple_proj, final_norm, 0)
    col_bias = _nbr_col_bias(nat_rpb[0])
    groups = ((x_prompt, p_prompt[0]), (x_sample, p_sample[0]))
    outs = {}
    for seq in sorted({x.shape[1] for x, _ in groups}):
        members = [k for k, (x, _) in enumerate(groups) if x.shape[1] == seq]
        rot = _rotary_tables(seq, *wts["rot_gains"])
        ys = _encoder([groups[k][0] for k in members], [groups[k][1] for k in members], wts, rot, col_bias)
        outs.update(zip(members, ys))
    return tuple(outs[k] for k in range(len(groups)))
```

```python
import functools
import math

import numpy as np
import jax
import jax.numpy as jnp
from jax import lax
from jax.experimental import pallas as pl
from jax.experimental.pallas import tpu as pltpu

F32 = jnp.float32
BF16 = jnp.bfloat16

D_MODEL = 2048
HEAD_DIM = 128
N_HEADS_A = 8
N_KV_A = 2
GQA_GROUP = N_HEADS_A // N_KV_A
N_HEADS_B = 8
WIDTH_A = N_HEADS_A * HEAD_DIM
KV_WIDTH_A = N_KV_A * HEAD_DIM
WIDTH_B = N_HEADS_B * HEAD_DIM
ROT_WIDTH = WIDTH_A + KV_WIDTH_A
QKV_A_WIDTH = ROT_WIDTH + KV_WIDTH_A
D_FF = 5632
D_PLE = 256
GRID_W = 64
WIN_ROWS = 8
WIN_COLS = 16
ROPE_THETA = 10000.0
EPS = 1e-6
ATTN_SCALE = HEAD_DIM ** -0.5
LOG2_E = math.log2(math.e)
MASKED = -1e30
GQA_SAFE_LOG2_SCORE = 48.0
BF16_NORM_MARGIN = 1.02

NBR_Q_ROWS = 4
NBR_KEY_ROWS = 12
NBR_Q = NBR_Q_ROWS * GRID_W
NBR_KEYS = NBR_KEY_ROWS * GRID_W
NBR_DR_PAD = (NBR_KEY_ROWS - NBR_Q_ROWS) - (WIN_ROWS - 1) + (NBR_Q_ROWS - 1)
NBR_DR_SLOTS = NBR_DR_PAD + (WIN_ROWS - 1) + (NBR_KEY_ROWS - 2) + 1

PROJ_TN = 1024

V7X_VMEM_BYTES = 64 * 1024 * 1024
MIB = 1024 * 1024


def _params(semantics, vmem_mib):
    assert vmem_mib * MIB < V7X_VMEM_BYTES
    return pltpu.CompilerParams(dimension_semantics=semantics, vmem_limit_bytes=vmem_mib * MIB)


def _resident(shape):
    zeros = (0,) * len(shape)
    return pl.BlockSpec(shape, lambda *_: zeros, pipeline_mode=pl.Buffered(1))


def _rms(x):
    return x * lax.rsqrt(jnp.mean(x * x, axis=-1, keepdims=True) + EPS)


FFN_RESIDUAL_SCALE = 0.5
FFN_TF = 512
ROW_CHUNK = 16
NORM_ROWS = 128
ROW_UNROLL_EXIT = 16

def _ffn_kernel(*refs, tile_starts):
    n_src = len(tile_starts) - 1
    x_srcs = refs[:n_src]
    g1_ref, wgu_ref, wd_ref, g2_ref, h_ref, n_ref, x_buf, xn_ref, x_sem = refs[n_src:]
    i = pl.program_id(0)
    j = pl.program_id(1)
    tm = x_buf.shape[0]

    def x_copy(tile, wait):
        for s, x_hbm in enumerate(x_srcs):
            @pl.when((tile >= tile_starts[s]) & (tile < tile_starts[s + 1]))
            def _(s=s, x_hbm=x_hbm):
                row = pl.multiple_of((tile - tile_starts[s]) * tm, tm)
                copy = pltpu.make_async_copy(x_hbm.at[pl.ds(row, tm), :], x_buf, x_sem)
                copy.wait() if wait else copy.start()

    cur = i % 2
    has_next = i + 1 < pl.num_programs(0)
    n_chunks = tm // NORM_ROWS

    def norm_rows(slot, row0, n_rows):
        for r in range(0, n_rows, ROW_CHUNK):
            rows = pl.ds(row0 + r, ROW_CHUNK)
            xn_ref[slot, rows, :] = (_rms(x_buf[rows, :]) * g1_ref[...]).astype(BF16)

    @pl.when((i == 0) & (j == 0))
    def _():
        x_copy(0, wait=False)
        x_copy(0, wait=True)

        def first(c, carry):
            norm_rows(0, pl.multiple_of(c * NORM_ROWS, NORM_ROWS), NORM_ROWS)
            return carry

        lax.fori_loop(0, n_chunks, first, 0)

    @pl.when((j == 1) & has_next)
    def _():
        x_copy(i + 1, wait=False)

    @pl.when((j == 2) & has_next)
    def _():
        x_copy(i + 1, wait=True)

    tf = wd_ref.shape[0]

    def step(first_step, prepare_next):
        gu = jnp.dot(xn_ref[cur], wgu_ref[...], preferred_element_type=F32)
        g, u = gu[:, :tf], gu[:, tf:]
        a = (g * jax.nn.sigmoid(g) * u).astype(BF16)
        down = jnp.dot(a, wd_ref[...], preferred_element_type=F32)
        h_ref[...] = (x_buf[...] if first_step else h_ref[...]) + down
        if prepare_next:
            chunk = jnp.minimum(j - 2, n_chunks - 1)
            norm_rows(1 - cur, pl.multiple_of(chunk * NORM_ROWS, NORM_ROWS), NORM_ROWS)

    pl.when(j == 0)(functools.partial(step, True, False))
    pl.when(j == 1)(functools.partial(step, False, False))
    pl.when(j >= 2)(functools.partial(step, False, True))

    @pl.when(j == pl.num_programs(1) - 1)
    def _():
        def last(r, carry):
            rows = pl.ds(pl.multiple_of(r * ROW_CHUNK, ROW_CHUNK), ROW_CHUNK)
            n_ref[rows, :] = (_rms(h_ref[rows, :]) * g2_ref[...]).astype(BF16)
            return carry

        lax.fori_loop(0, tm // ROW_CHUNK, last, 0, unroll=ROW_UNROLL_EXIT)


def _ffn(xs, g1, w_gate_up, slot, wd, g2, *, tm=1024):
    d = xs[0].shape[1]
    dff, tf = wd.shape[0], FFN_TF
    assert all(x.shape[0] % tm == 0 and x.shape[1] == d for x in xs)
    assert w_gate_up.shape[1:] == (dff // tf, d, 2 * tf)
    assert tm % NORM_ROWS == 0 and tm // NORM_ROWS <= dff // tf - 2
    tile_starts = tuple(int(n) for n in np.cumsum([0] + [x.shape[0] // tm for x in xs]))
    t = tile_starts[-1] * tm
    return pl.pallas_call(
        functools.partial(_ffn_kernel, tile_starts=tile_starts),
        out_shape=(jax.ShapeDtypeStruct((t, d), F32), jax.ShapeDtypeStruct((t, d), BF16)),
        grid=(t // tm, dff // tf),
        in_specs=[pl.BlockSpec(memory_space=pl.ANY)] * len(xs) + [
            pl.BlockSpec((1, d), lambda i, j: (0, 0)),
            pl.BlockSpec((None, None, d, 2 * tf), lambda i, j: (slot, j, 0, 0)),
            pl.BlockSpec((tf, d), lambda i, j: (j, 0)),
            pl.BlockSpec((1, d), lambda i, j: (0, 0)),
        ],
        out_specs=(
            pl.BlockSpec((tm, d), lambda i, j: (i, 0)),
            pl.BlockSpec((tm, d), lambda i, j: (i, 0)),
        ),
        scratch_shapes=[pltpu.VMEM((tm, d), F32), pltpu.VMEM((2, tm, d), BF16), pltpu.SemaphoreType.DMA(())],
        compiler_params=_params(("arbitrary", "arbitrary"), 60),
        name="ffn",
    )(*xs, g1, w_gate_up, wd, g2)


def _permute_heads_kernel(w_ref, perm_ref, o_ref):
    o_ref[...] = jnp.dot(w_ref[...], perm_ref[...], preferred_element_type=F32).astype(BF16)


def _rotary_weight_layout(w_in):
    d = w_in.shape[0]
    half = np.concatenate([np.arange(0, HEAD_DIM, 2), np.arange(1, HEAD_DIM, 2)])
    perm = jnp.asarray(np.arange(HEAD_DIM)[:, None] == half[None, :], BF16)
    return pl.pallas_call(
        _permute_heads_kernel,
        out_shape=jax.ShapeDtypeStruct((d, ROT_WIDTH), BF16),
        grid=(ROT_WIDTH // HEAD_DIM,),
        in_specs=[pl.BlockSpec((d, HEAD_DIM), lambda h: (0, h)), _resident((HEAD_DIM, HEAD_DIM))],
        out_specs=pl.BlockSpec((d, HEAD_DIM), lambda h: (0, h)),
        compiler_params=_params(("parallel",), 16),
        name="rotary_weight_layout",
    )(w_in, perm)


def _qkv_a_kernel(u_ref, wr_ref, wv_ref, rot_ref, q_ref, k_ref, v_ref, z_buf, v_buf, *, n_tiles):
    i = pl.program_id(0)

    def project(slot):
        u = u_ref[...]
        z_buf[slot] = jnp.dot(u, wr_ref[...], preferred_element_type=F32)
        v_buf[slot] = jnp.dot(u, wv_ref[...], preferred_element_type=F32).astype(BF16)

    def finish(slot):
        for hh in range(N_HEADS_A + N_KV_A):
            zh = z_buf[slot, :, hh * HEAD_DIM:(hh + 1) * HEAD_DIM]
            tab = 0 if hh < N_HEADS_A else 2
            cos_g = rot_ref[:, tab * HEAD_DIM:(tab + 1) * HEAD_DIM]
            sin_g = rot_ref[:, (tab + 1) * HEAD_DIM:(tab + 2) * HEAD_DIM]
            r = lax.rsqrt(jnp.mean(zh * zh, axis=-1, keepdims=True) + EPS)
            out = ((zh * cos_g + pltpu.roll(zh, HEAD_DIM // 2, 1) * sin_g) * r).astype(BF16)
            if hh < N_HEADS_A:
                q_ref[:, hh * HEAD_DIM:(hh + 1) * HEAD_DIM] = out
            else:
                kk = hh - N_HEADS_A
                k_ref[:, kk * HEAD_DIM:(kk + 1) * HEAD_DIM] = out
        v_ref[...] = v_buf[slot]

    @pl.when(i == 0)
    def _():
        project(0)

    for parity in range(2):
        @pl.when((i > 0) & (i < n_tiles) & (i % 2 == parity))
        def _(parity=parity):
            project(parity)
            finish(1 - parity)

    @pl.when(i == n_tiles)
    def _():
        finish((n_tiles - 1) % 2)


def _qkv_a(u, w_rot, w_in, rot, seq, *, tm=512):
    t, d = u.shape
    assert t % tm == 0 and seq % tm == 0 and ROT_WIDTH % KV_WIDTH_A == 0
    n_tiles, seq_blocks = t // tm, seq // tm

    def lagged(i):
        return jnp.maximum(i - 1, 0)

    return pl.pallas_call(
        functools.partial(_qkv_a_kernel, n_tiles=n_tiles),
        out_shape=(
            jax.ShapeDtypeStruct((t, WIDTH_A), BF16),
            jax.ShapeDtypeStruct((t, KV_WIDTH_A), BF16),
            jax.ShapeDtypeStruct((t, KV_WIDTH_A), BF16),
        ),
        grid=(n_tiles + 1,),
        in_specs=[
            pl.BlockSpec((tm, d), lambda i: (jnp.minimum(i, n_tiles - 1), 0)),
            _resident((d, ROT_WIDTH)),
            pl.BlockSpec((d, KV_WIDTH_A), lambda i: (0, ROT_WIDTH // KV_WIDTH_A), pipeline_mode=pl.Buffered(1)),
            pl.BlockSpec((tm, 4 * HEAD_DIM), lambda i: (lagged(i) % seq_blocks, 0)),
        ],
        out_specs=(
            pl.BlockSpec((tm, WIDTH_A), lambda i: (lagged(i), 0)),
            pl.BlockSpec((tm, KV_WIDTH_A), lambda i: (lagged(i), 0)),
            pl.BlockSpec((tm, KV_WIDTH_A), lambda i: (lagged(i), 0)),
        ),
        scratch_shapes=[pltpu.VMEM((2, tm, ROT_WIDTH), F32), pltpu.VMEM((2, tm, KV_WIDTH_A), BF16)],
        compiler_params=_params(("arbitrary",), 40),
        name="qkv_a",
    )(u, w_rot, w_in, rot)


def _proj_kernel(u_ref, w_ref, scale_ref, z_ref):
    z = jnp.dot(u_ref[...], w_ref[...], preferred_element_type=F32)
    z_ref[...] = (z * scale_ref[...]).astype(BF16)


def _proj(u, w, col_scale, *, tm=2048):
    t, d = u.shape
    n_chunks, _, tn = w.shape
    assert t % tm == 0 and col_scale.shape == (1, n_chunks * tn)
    return pl.pallas_call(
        _proj_kernel,
        out_shape=jax.ShapeDtypeStruct((t, n_chunks * tn), BF16),
        grid=(t // tm, n_chunks),
        in_specs=[
            pl.BlockSpec((tm, d), lambda i, j: (i, 0)),
            pl.BlockSpec((None, d, tn), lambda i, j: (j, 0, 0)),
            pl.BlockSpec((1, tn), lambda i, j: (0, j)),
        ],
        out_specs=pl.BlockSpec((tm, tn), lambda i, j: (i, j)),
        compiler_params=_params(("parallel", "arbitrary"), 48),
        name="proj",
    )(u, w, col_scale)


def _softmax_pv(s, v):
    m = jnp.max(s, axis=-1, keepdims=True)
    p = jnp.exp2(s - m)
    l = jnp.sum(p, axis=-1, keepdims=True)
    return jnp.dot(p.astype(BF16), v, preferred_element_type=F32) / l


_NT = (((1,), (1,)), ((), ()))


def _gqa_kernel(q_ref, k_ref, v_ref, o_ref, *, bounded):
    k = k_ref[0]
    v = v_ref[0]

    def score(g):
        return lax.dot_general(q_ref[0, :, g * HEAD_DIM:(g + 1) * HEAD_DIM], k, _NT, preferred_element_type=F32)

    s_next = score(0)
    for g in range(GQA_GROUP):
        s = s_next
        if g + 1 < GQA_GROUP:
            s_next = score(g + 1)
        if bounded:
            p = jnp.exp2(s)
            l = jnp.sum(p, axis=-1, keepdims=True)
            o = jnp.dot(p.astype(BF16), v, preferred_element_type=F32) / l
        else:
            o = _softmax_pv(s, v)
        o_ref[0, :, g * HEAD_DIM:(g + 1) * HEAD_DIM] = o.astype(BF16)


def _gqa(q, k, v, *, bounded, tq=512):
    b, s, _ = q.shape
    assert s % tq == 0
    gw = GQA_GROUP * HEAD_DIM
    return pl.pallas_call(
        functools.partial(_gqa_kernel, bounded=bounded),
        out_shape=jax.ShapeDtypeStruct((b, s, WIDTH_A), BF16),
        grid=(b, N_KV_A, s // tq),
        in_specs=[
            pl.BlockSpec((1, tq, gw), lambda bi, kh, qi: (bi, qi, kh)),
            pl.BlockSpec((1, s, HEAD_DIM), lambda bi, kh, qi: (bi, 0, kh)),
            pl.BlockSpec((1, s, HEAD_DIM), lambda bi, kh, qi: (bi, 0, kh)),
        ],
        out_specs=pl.BlockSpec((1, tq, gw), lambda bi, kh, qi: (bi, qi, kh)),
        compiler_params=_params(("parallel", "parallel", "arbitrary"), 40),
        name="gqa_bounded" if bounded else "gqa",
    )(q, k, v)


def _nbr_window_start(j, rows):
    return jnp.clip(NBR_Q_ROWS * j - WIN_ROWS // 2, 0, rows - NBR_KEY_ROWS)


def _nbr_block_plans(rows):
    plans = {}
    for j in range(rows // NBR_Q_ROWS):
        ws = int(np.clip(NBR_Q_ROWS * j - WIN_ROWS // 2, 0, rows - NBR_KEY_ROWS))
        lo = tuple(int(np.clip(NBR_Q_ROWS * j + qr - WIN_ROWS // 2, 0, rows - WIN_ROWS)) - ws
                   for qr in range(NBR_Q_ROWS))
        d0 = ws - NBR_Q_ROWS * j + (WIN_ROWS - 1) + NBR_DR_PAD
        assert min(lo) >= 0 and max(lo) + WIN_ROWS <= NBR_KEY_ROWS
        plans.setdefault((lo, d0), []).append(j)
    return plans


def _nbr_block(q_ref, k_ref, v_ref, cb_ref, o_ref, ws, lo, d0):
    pair_w = 2 * GRID_W
    pa = min(lo) // 2
    pb = (max(lo) + WIN_ROWS - 1) // 2 + 1
    n_keys = (pb - pa) * pair_w
    start = pl.multiple_of(ws * GRID_W + pa * pair_w, pair_w)
    lane = lax.broadcasted_iota(jnp.int32, (GRID_W, pair_w), 1)
    zeros = jnp.zeros((GRID_W, pair_w), BF16)
    heads = [slice(h * HEAD_DIM, (h + 1) * HEAD_DIM) for h in range(N_HEADS_B)]

    def score(cols):
        return lax.dot_general(q_ref[0, :, cols], k_ref[0, pl.ds(start, n_keys), cols], _NT,
                               preferred_element_type=F32)

    lookahead = len(set(lo)) > 1
    s_next = score(heads[0])
    for h, cols in enumerate(heads):
        s = s_next if lookahead or h == 0 else score(cols)
        p_rows, l_rows = [], []
        for qr in range(NBR_Q_ROWS):
            first, last = lo[qr], lo[qr] + WIN_ROWS - 1
            tiles = []
            for a in range(first // 2, last // 2 + 1):
                t = (s[qr * GRID_W:(qr + 1) * GRID_W, (a - pa) * pair_w:(a - pa + 1) * pair_w]
                     + cb_ref[h, d0 + 2 * a - qr])
                if 2 * a < first:
                    t = jnp.where(lane < GRID_W, MASKED, t)
                if 2 * a + 1 > last:
                    t = jnp.where(lane >= GRID_W, MASKED, t)
                tiles.append(t)
            m = jnp.max(functools.reduce(jnp.maximum, tiles), axis=-1, keepdims=True)
            ps = [jnp.exp2(t - m) for t in tiles]
            l_rows.append(jnp.sum(functools.reduce(jnp.add, ps), axis=-1, keepdims=True))
            p_rows.append(jnp.concatenate(
                [zeros] * (first // 2 - pa) + [p.astype(BF16) for p in ps] + [zeros] * (pb - 1 - last // 2),
                axis=1))
        p = jnp.concatenate(p_rows, axis=0)
        l = jnp.concatenate(l_rows, axis=0)
        if lookahead and h + 1 < N_HEADS_B:
            s_next = score(heads[h + 1])
        v = v_ref[0, pl.ds(start, n_keys), cols]
        o_ref[0, :, cols] = (jnp.dot(p, v, preferred_element_type=F32) / l).astype(BF16)


def _nbr_kernel(q_ref, k_ref, v_ref, cb_ref, o_ref, *, rows):
    j = pl.program_id(1)
    ws = _nbr_window_start(j, rows)
    for (lo, d0), members in _nbr_block_plans(rows).items():
        member = functools.reduce(jnp.logical_or, [j == m for m in members])
        pl.when(member)(functools.partial(_nbr_block, q_ref, k_ref, v_ref, cb_ref, o_ref, ws, lo, d0))


def _nbr(z, col_bias, q_col, k_col, v_col):
    b, s, _ = z.shape
    rows = s // GRID_W
    assert rows % NBR_Q_ROWS == 0 and rows >= NBR_KEY_ROWS
    return pl.pallas_call(
        functools.partial(_nbr_kernel, rows=rows),
        out_shape=jax.ShapeDtypeStruct((b, s, WIDTH_B), BF16),
        grid=(b, rows // NBR_Q_ROWS),
        in_specs=[
            pl.BlockSpec((1, NBR_Q, WIDTH_B), lambda bi, j: (bi, j, q_col)),
            pl.BlockSpec((1, s, WIDTH_B), lambda bi, j: (bi, 0, k_col)),
            pl.BlockSpec((1, s, WIDTH_B), lambda bi, j: (bi, 0, v_col)),
            _resident(col_bias.shape),
        ],
        out_specs=pl.BlockSpec((1, NBR_Q, WIDTH_B), lambda bi, j: (bi, j, 0)),
        compiler_params=_params(("parallel", "arbitrary"), 48),
        name="nbr",
    )(z, z, z, col_bias)


def _nbr_col_bias(rpb):
    n_dr, n_dc = 2 * WIN_ROWS - 1, 2 * WIN_COLS - 1
    c = np.arange(GRID_W)[:, None]
    kc = np.arange(GRID_W)[None, :]
    cs = np.clip(c - WIN_COLS // 2, 0, GRID_W - WIN_COLS)
    col_valid = (kc >= cs) & (kc < cs + WIN_COLS)
    onehot = ((kc - c + (WIN_COLS - 1))[None] == np.arange(n_dc)[:, None, None]) & col_valid[None]
    onehot = jnp.asarray(onehot.reshape(n_dc, GRID_W * GRID_W), F32)
    tm = jnp.einsum("hrd,dn->hrn", rpb * LOG2_E, onehot, precision=lax.Precision.HIGHEST)
    tm = jnp.where(col_valid.reshape(-1), tm, MASKED).reshape(-1, n_dr, GRID_W, GRID_W)
    hi_pad = NBR_DR_SLOTS + 1 - NBR_DR_PAD - n_dr
    tm = jnp.pad(tm, ((0, 0), (NBR_DR_PAD, hi_pad), (0, 0), (0, 0)), constant_values=MASKED)
    return jnp.concatenate([tm[:, :-1], tm[:, 1:]], axis=-1)


MERGE_TN = 512


def _merge_kernel(oa_ref, ob_ref, ga_ref, gb_ref, h_ref, wa_ref, wb_ref, wo_ref, out_ref, m_ref):
    oa = oa_ref[...]
    ob = ob_ref[...]
    for c in range(D_MODEL // MERGE_TN):
        cols = slice(c * MERGE_TN, (c + 1) * MERGE_TN)
        a = jnp.dot(oa, wa_ref[:, cols], preferred_element_type=F32)
        b = jnp.dot(ob, wb_ref[:, cols], preferred_element_type=F32)
        merged = (jax.nn.sigmoid(ga_ref[:, cols].astype(F32)) * a
                  + jax.nn.sigmoid(gb_ref[:, cols].astype(F32)) * b)
        m_ref[:, cols] = merged.astype(BF16)
    out_ref[...] = h_ref[...] + jnp.dot(m_ref[...], wo_ref[...], preferred_element_type=F32)


def _merge(oa, ob, z, ga_col, gb_col, h, wa, wb, wo, *, tm=512):
    t, d = h.shape
    assert t % tm == 0 and d == D_MODEL
    return pl.pallas_call(
        _merge_kernel,
        out_shape=jax.ShapeDtypeStruct((t, d), F32),
        grid=(t // tm,),
        in_specs=[
            pl.BlockSpec((tm, WIDTH_A), lambda i: (i, 0)),
            pl.BlockSpec((tm, WIDTH_B), lambda i: (i, 0)),
            pl.BlockSpec((tm, d), lambda i: (i, ga_col)),
            pl.BlockSpec((tm, d), lambda i: (i, gb_col)),
            pl.BlockSpec((tm, d), lambda i: (i, 0)),
            _resident((WIDTH_A, d)),
            _resident((WIDTH_B, d)),
            _resident((d, d)),
        ],
        out_specs=pl.BlockSpec((tm, d), lambda i: (i, 0)),
        scratch_shapes=[pltpu.VMEM((tm, d), BF16)],
        compiler_params=_params(("parallel",), 56),
        name="merge",
    )(oa, ob, z, z, h, wa, wb, wo)


def _ple_kernel(h_ref, n_ref, p_ref, wg_ref, wp_ref, gf_ref, y_ref):
    gate = jax.nn.sigmoid(jnp.dot(n_ref[...], wg_ref[...], preferred_element_type=F32))
    emb = jnp.dot(p_ref[...].astype(BF16), wp_ref[...], preferred_element_type=F32)
    y_ref[...] = _rms(h_ref[...] + gate * emb) * gf_ref[...]


def _ple(h, n, row0, p, wg, wp, gf, *, tm=512):
    t, d = p.shape[0], h.shape[1]
    assert t % tm == 0 and row0 % tm == 0
    tile0 = row0 // tm
    return pl.pallas_call(
        _ple_kernel,
        out_shape=jax.ShapeDtypeStruct((t, d), F32),
        grid=(t // tm,),
        in_specs=[
            pl.BlockSpec((tm, d), lambda i: (i + tile0, 0)),
            pl.BlockSpec((tm, d), lambda i: (i + tile0, 0)),
            pl.BlockSpec((tm, D_PLE), lambda i: (i, 0)),
            _resident((d, d)),
            _resident((D_PLE, d)),
            _resident((1, d)),
        ],
        out_specs=pl.BlockSpec((tm, d), lambda i: (i, 0)),
        compiler_params=_params(("parallel",), 48),
        name="ple",
    )(h, n, p, wg, wp, gf)


CAST_BLOCK_BYTES = 4 * MIB


def _cast_kernel(w_ref, o_ref, *, scale):
    w = w_ref[...]
    o_ref[...] = (w if scale == 1.0 else w * scale).astype(BF16)


def _to_bf16(w, layer, scale=1.0):
    _, r, c = w.shape
    target = max(16, CAST_BLOCK_BYTES // (4 * c))
    rows = next(n for n in range(min(r, target) // 16 * 16, 0, -16) if r % n == 0)
    return pl.pallas_call(
        functools.partial(_cast_kernel, scale=scale),
        out_shape=jax.ShapeDtypeStruct((r, c), BF16),
        grid=(r // rows,),
        in_specs=[pl.BlockSpec((None, rows, c), lambda i: (layer, i, 0))],
        out_specs=pl.BlockSpec((rows, c), lambda i: (i, 0)),
        compiler_params=_params(("parallel",), 32),
        name="to_bf16",
    )(w)


def _pack_w_in_kernel(w_ref, qkv_ref, proj_ref, *, qkv_width, chunk_starts, tn):
    qkv_ref[...] = w_ref[:, :qkv_width].astype(BF16)
    for c, start in enumerate(chunk_starts):
        proj_ref[c] = w_ref[:, start:start + tn].astype(BF16)


def _pack_w_in(w_in, layer, qkv_width, chunk_starts, tn, *, rows=64):
    _, d, n = w_in.shape
    assert d % rows == 0 and all(s % HEAD_DIM == 0 and s + tn <= n for s in chunk_starts)
    return pl.pallas_call(
        functools.partial(_pack_w_in_kernel, qkv_width=qkv_width, chunk_starts=chunk_starts, tn=tn),
        out_shape=(jax.ShapeDtypeStruct((d, qkv_width), BF16),
                   jax.ShapeDtypeStruct((len(chunk_starts), d, tn), BF16)),
        grid=(d // rows,),
        in_specs=[pl.BlockSpec((None, rows, n), lambda i: (layer, i, 0))],
        out_specs=(pl.BlockSpec((rows, qkv_width), lambda i: (i, 0)),
                   pl.BlockSpec((len(chunk_starts), rows, tn), lambda i: (0, i, 0))),
        compiler_params=_params(("parallel",), 32),
        name="pack_w_in",
    )(w_in)


def _pack_gate_up_kernel(*refs, tf):
    o_ref = refs[-1]
    for slot in range(len(refs) // 2):
        @pl.when(pl.program_id(0) == slot)
        def _(g_ref=refs[2 * slot], u_ref=refs[2 * slot + 1]):
            for j in range(o_ref.shape[0]):
                o_ref[j, :, :tf] = g_ref[:, j * tf:(j + 1) * tf].astype(BF16)
                o_ref[j, :, tf:] = u_ref[:, j * tf:(j + 1) * tf].astype(BF16)


def _pack_gate_up(gate_up_pairs, layer, *, tf, rows=64):
    _, r, c = gate_up_pairs[0][0].shape
    assert r % rows == 0 and c % tf == 0
    n_chunks, n_blocks = c // tf, r // rows

    def in_spec(slot):
        return pl.BlockSpec((None, rows, c), lambda s, i: (layer, jnp.where(s == slot, i, n_blocks - 1), 0))

    weights = [w for pair in gate_up_pairs for w in pair]
    assert all(w.shape == weights[0].shape for w in weights)
    return pl.pallas_call(
        functools.partial(_pack_gate_up_kernel, tf=tf),
        out_shape=jax.ShapeDtypeStruct((len(gate_up_pairs), n_chunks, r, 2 * tf), BF16),
        grid=(len(gate_up_pairs), n_blocks),
        in_specs=[in_spec(slot) for slot in range(len(gate_up_pairs)) for _ in range(2)],
        out_specs=pl.BlockSpec((None, n_chunks, rows, 2 * tf), lambda s, i: (s, 0, i, 0)),
        compiler_params=_params(("arbitrary", "arbitrary"), 32),
        name="pack_gate_up",
    )(*weights)


def _rotary_tables(seq, gain_q, gain_k):
    n_freq = HEAD_DIM // 4
    inv_freq = ROPE_THETA ** (-jnp.arange(n_freq, dtype=F32) / n_freq)
    t = jnp.arange(seq)
    row = (t // GRID_W).astype(F32)
    col = (t % GRID_W).astype(F32)
    ang = jnp.concatenate([row[:, None] * inv_freq[None], col[:, None] * inv_freq[None]], axis=-1)
    cos = jnp.concatenate([jnp.cos(ang), jnp.cos(ang)], axis=-1)
    sin = jnp.concatenate([-jnp.sin(ang), jnp.sin(ang)], axis=-1)
    return jnp.concatenate([cos * g if part == 0 else sin * jnp.roll(g, HEAD_DIM // 2)
                            for g in (gain_q, gain_k) for part in (0, 1)], axis=-1)


_W_IN_SEGMENTS = dict(q_b=QKV_A_WIDTH, k_b=QKV_A_WIDTH + WIDTH_B, v_b=QKV_A_WIDTH + 2 * WIDTH_B,
                      g_a=QKV_A_WIDTH + 3 * WIDTH_B, g_b=QKV_A_WIDTH + 3 * WIDTH_B + D_MODEL)
_PROJ_LAYOUT = (("g_a", D_MODEL), ("g_b", D_MODEL), ("q_b", WIDTH_B), ("k_b", WIDTH_B), ("v_b", WIDTH_B))
_PROJ_CHUNK_STARTS = tuple(_W_IN_SEGMENTS[name] + off
                           for name, width in _PROJ_LAYOUT for off in range(0, width, PROJ_TN))
assert all(width % PROJ_TN == 0 for _, width in _PROJ_LAYOUT)
_GA_COL, _GB_COL = 0, 1
_QB_COL, _KB_COL, _VB_COL = 4, 5, 6
assert 2 * D_MODEL == _QB_COL * WIDTH_B


def _prepare_weights(ffn1_norm, ffn1_w_gate, ffn1_w_up, ffn1_w_down, mix_norm, w_in, q_norm, k_norm,
                     w_branch_a, w_branch_b, w_out, ffn2_norm, ffn2_w_gate, ffn2_w_up, ffn2_w_down,
                     ple_norm, w_ple_gate, w_ple_proj, final_norm, layer):
    i = layer
    half = np.concatenate([np.arange(0, HEAD_DIM, 2), np.arange(1, HEAD_DIM, 2)])
    w_qkv_a, w_proj = _pack_w_in(w_in, i, QKV_A_WIDTH, _PROJ_CHUNK_STARTS, PROJ_TN)
    w_rot = _rotary_weight_layout(w_qkv_a)
    w_gate_up = _pack_gate_up([(ffn1_w_gate, ffn1_w_up), (ffn2_w_gate, ffn2_w_up)], i, tf=FFN_TF)
    q_scale = ATTN_SCALE * LOG2_E
    proj_scale = jnp.concatenate([jnp.full((width,), q_scale if name == "q_b" else 1.0, F32)
                                  for name, width in _PROJ_LAYOUT])[None]
    rot_gains = ((q_norm[i] * q_scale)[half], k_norm[i][half])
    gqa_score_bound = (HEAD_DIM * q_scale * BF16_NORM_MARGIN
                       * jnp.max(jnp.abs(q_norm[i])) * jnp.max(jnp.abs(k_norm[i])))
    return dict(
        ffn1=(ffn1_norm[i][None], w_gate_up, 0, _to_bf16(ffn1_w_down, i, FFN_RESIDUAL_SCALE), mix_norm[i][None]),
        w_rot=w_rot, rot_gains=rot_gains, w_qkv_a=w_qkv_a, w_proj=w_proj, proj_scale=proj_scale, gqa_score_bound=gqa_score_bound,
        merge=(_to_bf16(w_branch_a, i), _to_bf16(w_branch_b, i), _to_bf16(w_out, i)),
        ffn2=(ffn2_norm[i][None], w_gate_up, 1, _to_bf16(ffn2_w_down, i, FFN_RESIDUAL_SCALE), ple_norm[i][None]),
        ple=(_to_bf16(w_ple_gate, i), _to_bf16(w_ple_proj, i), final_norm[None]),
    )


def _encoder(xs, ps, wts, rot, col_bias):
    s, d = xs[0].shape[1:]
    assert all(x.shape[1:] == (s, d) for x in xs)
    b = sum(x.shape[0] for x in xs)
    t = b * s
    h1, u = _ffn([x.reshape(-1, d) for x in xs], *wts["ffn1"])
    q_a, k_a, v_a = _qkv_a(u, wts["w_rot"], wts["w_qkv_a"], rot, s)
    z = _proj(u, wts["w_proj"], wts["proj_scale"])
    qkv = (q_a.reshape(b, s, -1), k_a.reshape(b, s, -1), v_a.reshape(b, s, -1))
    o_a = lax.cond(wts["gqa_score_bound"] <= GQA_SAFE_LOG2_SCORE,
                   functools.partial(_gqa, bounded=True), functools.partial(_gqa, bounded=False), *qkv)
    o_b = _nbr(z.reshape(b, s, -1), col_bias, _QB_COL, _KB_COL, _VB_COL)
    h2 = _merge(o_a.reshape(t, -1), o_b.reshape(t, -1), z, _GA_COL, _GB_COL, h1, *wts["merge"])
    h3, n = _ffn([h2], *wts["ffn2"])
    row_starts = np.cumsum([0] + [x.shape[0] * s for x in xs])
    return tuple(_ple(h3, n, int(row0), p.reshape(-1, p.shape[-1]), *wts["ple"]).reshape(x.shape)
                 for row0, p, x in zip(row_starts, ps, xs))


def kernel(x_prompt, x_sample, p_prompt, p_sample, ffn1_norm, ffn1_w_gate, ffn1_w_up, ffn1_w_down, mix_norm, w_in, q_norm, k_norm, nat_rpb, w_branch_a, w_branch_b, w_out, ffn2_norm, ffn2_w_gate, ffn2_w_up, ffn2_w_down, ple_norm, w_ple_gate, w_ple_proj, final_norm):
    assert ffn1_norm.shape[0] == 1, "single-layer encoder"
    wts = _prepare_weights(ffn1_norm, ffn1_w_gate, ffn1_w_up, ffn1_w_down, mix_norm, w_in, q_norm, k_norm,
                           w_branch_a, w_branch_b, w_out, ffn2_norm, ffn2_w_gate, ffn2_w_up, ffn2_w_down,
                           ple_norm, w_ple_gate, w_ple_proj, final_norm, 0)
    col_bias = _nbr_col_bias(nat_rpb[0])
    groups = ((x_prompt, p_prompt[0]), (x_sample, p_sample[0]))
    outs = {}
    for seq in sorted({x.shape[1] for x, _ in groups}):
        members = [k for k, (x, _) in enumerate(groups) if x.shape[1] == seq]
        rot = _rotary_tables(seq, *wts["rot_gains"])
        ys = _encoder([groups[k][0] for k in members], [groups[k][1] for k in members], wts, rot, col_bias)
        outs.update(zip(members, ys))
    return tuple(outs[k] for k in range(len(groups)))
```

```python
import functools
import math

import numpy as np
import jax
import jax.numpy as jnp
from jax import lax
from jax.experimental import pallas as pl
from jax.experimental.pallas import tpu as pltpu

F32 = jnp.float32
BF16 = jnp.bfloat16

D_MODEL = 2048
HEAD_DIM = 128
N_HEADS_A = 8
N_KV_A = 2
GQA_GROUP = N_HEADS_A // N_KV_A
N_HEADS_B = 8
WIDTH_A = N_HEADS_A * HEAD_DIM
KV_WIDTH_A = N_KV_A * HEAD_DIM
WIDTH_B = N_HEADS_B * HEAD_DIM
ROT_WIDTH = WIDTH_A + KV_WIDTH_A
QKV_A_WIDTH = ROT_WIDTH + KV_WIDTH_A
D_FF = 5632
D_PLE = 256
GRID_W = 64
WIN_ROWS = 8
WIN_COLS = 16
ROPE_THETA = 10000.0
EPS = 1e-6
ATTN_SCALE = HEAD_DIM ** -0.5
LOG2_E = math.log2(math.e)
MASKED = -1e30
GQA_SAFE_LOG2_SCORE = 48.0
BF16_NORM_MARGIN = 1.02

NBR_Q_ROWS = 4
NBR_KEY_ROWS = 12
NBR_Q = NBR_Q_ROWS * GRID_W
NBR_KEYS = NBR_KEY_ROWS * GRID_W
NBR_DR_PAD = (NBR_KEY_ROWS - NBR_Q_ROWS) - (WIN_ROWS - 1) + (NBR_Q_ROWS - 1)
NBR_DR_SLOTS = NBR_DR_PAD + (WIN_ROWS - 1) + (NBR_KEY_ROWS - 2) + 1

PROJ_TN = 1024

V7X_VMEM_BYTES = 64 * 1024 * 1024
MIB = 1024 * 1024


def _params(semantics, vmem_mib):
    assert vmem_mib * MIB < V7X_VMEM_BYTES
    return pltpu.CompilerParams(dimension_semantics=semantics, vmem_limit_bytes=vmem_mib * MIB)


def _resident(shape):
    zeros = (0,) * len(shape)
    return pl.BlockSpec(shape, lambda *_: zeros, pipeline_mode=pl.Buffered(1))


def _rms(x):
    return x * lax.rsqrt(jnp.mean(x * x, axis=-1, keepdims=True) + EPS)


FFN_RESIDUAL_SCALE = 0.5
FFN_TF = 512
ROW_CHUNK = 16
NORM_ROWS = 128
ROW_UNROLL_EXIT = 16

def _ffn_kernel(*refs, tile_starts):
    n_src = len(tile_starts) - 1
    x_srcs = refs[:n_src]
    g1_ref, wgu_ref, wd_ref, g2_ref, h_ref, n_ref, x_buf, xn_ref, x_sem = refs[n_src:]
    i = pl.program_id(0)
    j = pl.program_id(1)
    tm = x_buf.shape[0]

    def x_copy(tile, wait):
        for s, x_hbm in enumerate(x_srcs):
            @pl.when((tile >= tile_starts[s]) & (tile < tile_starts[s + 1]))
            def _(s=s, x_hbm=x_hbm):
                row = pl.multiple_of((tile - tile_starts[s]) * tm, tm)
                copy = pltpu.make_async_copy(x_hbm.at[pl.ds(row, tm), :], x_buf, x_sem)
                copy.wait() if wait else copy.start()

    cur = i % 2
    has_next = i + 1 < pl.num_programs(0)
    n_chunks = tm // NORM_ROWS

    def norm_rows(slot, row0, n_rows):
        for r in range(0, n_rows, ROW_CHUNK):
            rows = pl.ds(row0 + r, ROW_CHUNK)
            xn_ref[slot, rows, :] = (_rms(x_buf[rows, :]) * g1_ref[...]).astype(BF16)

    @pl.when((i == 0) & (j == 0))
    def _():
        x_copy(0, wait=False)
        x_copy(0, wait=True)

        def first(c, carry):
            norm_rows(0, pl.multiple_of(c * NORM_ROWS, NORM_ROWS), NORM_ROWS)
            return carry

        lax.fori_loop(0, n_chunks, first, 0)

    @pl.when((j == 1) & has_next)
    def _():
        x_copy(i + 1, wait=False)

    @pl.when((j == 2) & has_next)
    def _():
        x_copy(i + 1, wait=True)

    tf = wd_ref.shape[0]

    def step(first_step, prepare_next):
        gu = jnp.dot(xn_ref[cur], wgu_ref[...], preferred_element_type=F32)
        g, u = gu[:, :tf], gu[:, tf:]
        a = (g * jax.nn.sigmoid(g) * u).astype(BF16)
        down = jnp.dot(a, wd_ref[...], preferred_element_type=F32)
        h_ref[...] = (x_buf[...] if first_step else h_ref[...]) + down
        if prepare_next:
            chunk = jnp.minimum(j - 2, n_chunks - 1)
            norm_rows(1 - cur, pl.multiple_of(chunk * NORM_ROWS, NORM_ROWS), NORM_ROWS)

    pl.when(j == 0)(functools.partial(step, True, False))
    pl.when(j == 1)(functools.partial(step, False, False))
    pl.when(j >= 2)(functools.partial(step, False, True))

    @pl.when(j == pl.num_programs(1) - 1)
    def _():
        def last(r, carry):
            rows = pl.ds(pl.multiple_of(r * ROW_CHUNK, ROW_CHUNK), ROW_CHUNK)
            n_ref[rows, :] = (_rms(h_ref[rows, :]) * g2_ref[...]).astype(BF16)
            return carry

        lax.fori_loop(0, tm // ROW_CHUNK, last, 0, unroll=ROW_UNROLL_EXIT)


def _ffn(xs, g1, w_gate_up, slot, wd, g2, *, tm=1024):
    d = xs[0].shape[1]
    dff, tf = wd.shape[0], FFN_TF
    assert all(x.shape[0] % tm == 0 and x.shape[1] == d for x in xs)
    assert w_gate_up.shape[1:] == (dff // tf, d, 2 * tf)
    assert tm % NORM_ROWS == 0 and tm // NORM_ROWS <= dff // tf - 2
    tile_starts = tuple(int(n) for n in np.cumsum([0] + [x.shape[0] // tm for x in xs]))
    t = tile_starts[-1] * tm
    return pl.pallas_call(
        functools.partial(_ffn_kernel, tile_starts=tile_starts),
        out_shape=(jax.ShapeDtypeStruct((t, d), F32), jax.ShapeDtypeStruct((t, d), BF16)),
        grid=(t // tm, dff // tf),
        in_specs=[pl.BlockSpec(memory_space=pl.ANY)] * len(xs) + [
            pl.BlockSpec((1, d), lambda i, j: (0, 0)),
            pl.BlockSpec((None, None, d, 2 * tf), lambda i, j: (slot, j, 0, 0)),
            pl.BlockSpec((tf, d), lambda i, j: (j, 0)),
            pl.BlockSpec((1, d), lambda i, j: (0, 0)),
        ],
        out_specs=(
            pl.BlockSpec((tm, d), lambda i, j: (i, 0)),
            pl.BlockSpec((tm, d), lambda i, j: (i, 0)),
        ),
        scratch_shapes=[pltpu.VMEM((tm, d), F32), pltpu.VMEM((2, tm, d), BF16), pltpu.SemaphoreType.DMA(())],
        compiler_params=_params(("arbitrary", "arbitrary"), 60),
        name="ffn",
    )(*xs, g1, w_gate_up, wd, g2)


def _permute_heads_kernel(w_ref, perm_ref, o_ref):
    o_ref[...] = jnp.dot(w_ref[...], perm_ref[...], preferred_element_type=F32).astype(BF16)


def _rotary_weight_layout(w_in):
    d = w_in.shape[0]
    half = np.concatenate([np.arange(0, HEAD_DIM, 2), np.arange(1, HEAD_DIM, 2)])
    perm = jnp.asarray(np.arange(HEAD_DIM)[:, None] == half[None, :], BF16)
    return pl.pallas_call(
        _permute_heads_kernel,
        out_shape=jax.ShapeDtypeStruct((d, ROT_WIDTH), BF16),
        grid=(ROT_WIDTH // HEAD_DIM,),
        in_specs=[pl.BlockSpec((d, HEAD_DIM), lambda h: (0, h)), _resident((HEAD_DIM, HEAD_DIM))],
        out_specs=pl.BlockSpec((d, HEAD_DIM), lambda h: (0, h)),
        compiler_params=_params(("parallel",), 16),
        name="rotary_weight_layout",
    )(w_in, perm)


def _qkv_a_kernel(u_ref, wr_ref, wv_ref, rot_ref, q_ref, k_ref, v_ref, z_buf, v_buf, *, n_tiles):
    i = pl.program_id(0)

    def project(slot):
        u = u_ref[...]
        z_buf[slot] = jnp.dot(u, wr_ref[...], preferred_element_type=F32)
        v_buf[slot] = jnp.dot(u, wv_ref[...], preferred_element_type=F32).astype(BF16)

    def finish(slot):
        for hh in range(N_HEADS_A + N_KV_A):
            zh = z_buf[slot, :, hh * HEAD_DIM:(hh + 1) * HEAD_DIM]
            tab = 0 if hh < N_HEADS_A else 2
            cos_g = rot_ref[:, tab * HEAD_DIM:(tab + 1) * HEAD_DIM]
            sin_g = rot_ref[:, (tab + 1) * HEAD_DIM:(tab + 2) * HEAD_DIM]
            r = lax.rsqrt(jnp.mean(zh * zh, axis=-1, keepdims=True) + EPS)
            out = ((zh * cos_g + pltpu.roll(zh, HEAD_DIM // 2, 1) * sin_g) * r).astype(BF16)
            if hh < N_HEADS_A:
                q_ref[:, hh * HEAD_DIM:(hh + 1) * HEAD_DIM] = out
            else:
                kk = hh - N_HEADS_A
                k_ref[:, kk * HEAD_DIM:(kk + 1) * HEAD_DIM] = out
        v_ref[...] = v_buf[slot]

    @pl.when(i == 0)
    def _():
        project(0)

    for parity in range(2):
        @pl.when((i > 0) & (i < n_tiles) & (i % 2 == parity))
        def _(parity=parity):
            project(parity)
            finish(1 - parity)

    @pl.when(i == n_tiles)
    def _():
        finish((n_tiles - 1) % 2)


def _qkv_a(u, w_rot, w_in, rot, seq, *, tm=512):
    t, d = u.shape
    assert t % tm == 0 and seq % tm == 0 and ROT_WIDTH % KV_WIDTH_A == 0
    n_tiles, seq_blocks = t // tm, seq // tm

    def lagged(i):
        return jnp.maximum(i - 1, 0)

    return pl.pallas_call(
        functools.partial(_qkv_a_kernel, n_tiles=n_tiles),
        out_shape=(
            jax.ShapeDtypeStruct((t, WIDTH_A), BF16),
            jax.ShapeDtypeStruct((t, KV_WIDTH_A), BF16),
            jax.ShapeDtypeStruct((t, KV_WIDTH_A), BF16),
        ),
        grid=(n_tiles + 1,),
        in_specs=[
            pl.BlockSpec((tm, d), lambda i: (jnp.minimum(i, n_tiles - 1), 0)),
            _resident((d, ROT_WIDTH)),
            pl.BlockSpec((d, KV_WIDTH_A), lambda i: (0, ROT_WIDTH // KV_WIDTH_A), pipeline_mode=pl.Buffered(1)),
            pl.BlockSpec((tm, 4 * HEAD_DIM), lambda i: (lagged(i) % seq_blocks, 0)),
        ],
        out_specs=(
            pl.BlockSpec((tm, WIDTH_A), lambda i: (lagged(i), 0)),
            pl.BlockSpec((tm, KV_WIDTH_A), lambda i: (lagged(i), 0)),
            pl.BlockSpec((tm, KV_WIDTH_A), lambda i: (lagged(i), 0)),
        ),
        scratch_shapes=[pltpu.VMEM((2, tm, ROT_WIDTH), F32), pltpu.VMEM((2, tm, KV_WIDTH_A), BF16)],
        compiler_params=_params(("arbitrary",), 40),
        name="qkv_a",
    )(u, w_rot, w_in, rot)


def _proj_kernel(u_ref, w_ref, scale_ref, z_ref):
    z = jnp.dot(u_ref[...], w_ref[...], preferred_element_type=F32)
    z_ref[...] = (z * scale_ref[...]).astype(BF16)


def _proj(u, w, col_scale, *, tm=2048):
    t, d = u.shape
    n_chunks, _, tn = w.shape
    assert t % tm == 0 and col_scale.shape == (1, n_chunks * tn)
    return pl.pallas_call(
        _proj_kernel,
        out_shape=jax.ShapeDtypeStruct((t, n_chunks * tn), BF16),
        grid=(t // tm, n_chunks),
        in_specs=[
            pl.BlockSpec((tm, d), lambda i, j: (i, 0)),
            pl.BlockSpec((None, d, tn), lambda i, j: (j, 0, 0)),
            pl.BlockSpec((1, tn), lambda i, j: (0, j)),
        ],
        out_specs=pl.BlockSpec((tm, tn), lambda i, j: (i, j)),
        compiler_params=_params(("parallel", "arbitrary"), 48),
        name="proj",
    )(u, w, col_scale)


def _softmax_pv(s, v):
    m = jnp.max(s, axis=-1, keepdims=True)
    p = jnp.exp2(s - m)
    l = jnp.sum(p, axis=-1, keepdims=True)
    return jnp.dot(p.astype(BF16), v, preferred_element_type=F32) / l


_NT = (((1,), (1,)), ((), ()))


def _gqa_kernel(q_ref, k_ref, v_ref, o_ref, *, bounded):
    k = k_ref[0]
    v = v_ref[0]

    def score(g):
        return lax.dot_general(q_ref[0, :, g * HEAD_DIM:(g + 1) * HEAD_DIM], k, _NT, preferred_element_type=F32)

    s_next = score(0)
    for g in range(GQA_GROUP):
        s = s_next
        if g + 1 < GQA_GROUP:
            s_next = score(g + 1)
        if bounded:
            p = jnp.exp2(s)
            l = jnp.sum(p, axis=-1, keepdims=True)
            o = jnp.dot(p.astype(BF16), v, preferred_element_type=F32) / l
        else:
            o = _softmax_pv(s, v)
        o_ref[0, :, g * HEAD_DIM:(g + 1) * HEAD_DIM] = o.astype(BF16)


def _gqa(q, k, v, *, bounded, tq=1024):
    b, s, _ = q.shape
    assert s % tq == 0
    gw = GQA_GROUP * HEAD_DIM
    return pl.pallas_call(
        functools.partial(_gqa_kernel, bounded=bounded),
        out_shape=jax.ShapeDtypeStruct((b, s, WIDTH_A), BF16),
        grid=(b, N_KV_A, s // tq),
        in_specs=[
            pl.BlockSpec((1, tq, gw), lambda bi, kh, qi: (bi, qi, kh)),
            pl.BlockSpec((1, s, HEAD_DIM), lambda bi, kh, qi: (bi, 0, kh)),
            pl.BlockSpec((1, s, HEAD_DIM), lambda bi, kh, qi: (bi, 0, kh)),
        ],
        out_specs=pl.BlockSpec((1, tq, gw), lambda bi, kh, qi: (bi, qi, kh)),
        compiler_params=_params(("parallel", "parallel", "arbitrary"), 40),
        name="gqa_bounded" if bounded else "gqa",
    )(q, k, v)


def _nbr_window_start(j, rows):
    return jnp.clip(NBR_Q_ROWS * j - WIN_ROWS // 2, 0, rows - NBR_KEY_ROWS)


def _nbr_block_plans(rows):
    plans = {}
    for j in range(rows // NBR_Q_ROWS):
        ws = int(np.clip(NBR_Q_ROWS * j - WIN_ROWS // 2, 0, rows - NBR_KEY_ROWS))
        lo = tuple(int(np.clip(NBR_Q_ROWS * j + qr - WIN_ROWS // 2, 0, rows - WIN_ROWS)) - ws
                   for qr in range(NBR_Q_ROWS))
        d0 = ws - NBR_Q_ROWS * j + (WIN_ROWS - 1) + NBR_DR_PAD
        assert min(lo) >= 0 and max(lo) + WIN_ROWS <= NBR_KEY_ROWS
        plans.setdefault((lo, d0), []).append(j)
    return plans


def _nbr_block(q_ref, k_ref, v_ref, cb_ref, o_ref, ws, lo, d0):
    pair_w = 2 * GRID_W
    pa = min(lo) // 2
    pb = (max(lo) + WIN_ROWS - 1) // 2 + 1
    n_keys = (pb - pa) * pair_w
    start = pl.multiple_of(ws * GRID_W + pa * pair_w, pair_w)
    lane = lax.broadcasted_iota(jnp.int32, (GRID_W, pair_w), 1)
    zeros = jnp.zeros((GRID_W, pair_w), BF16)
    heads = [slice(h * HEAD_DIM, (h + 1) * HEAD_DIM) for h in range(N_HEADS_B)]

    def score(cols):
        return lax.dot_general(q_ref[0, :, cols], k_ref[0, pl.ds(start, n_keys), cols], _NT,
                               preferred_element_type=F32)

    lookahead = len(set(lo)) > 1
    s_next = score(heads[0])
    for h, cols in enumerate(heads):
        s = s_next if lookahead or h == 0 else score(cols)
        p_rows, l_rows = [], []
        for qr in range(NBR_Q_ROWS):
            first, last = lo[qr], lo[qr] + WIN_ROWS - 1
            tiles = []
            for a in range(first // 2, last // 2 + 1):
                t = (s[qr * GRID_W:(qr + 1) * GRID_W, (a - pa) * pair_w:(a - pa + 1) * pair_w]
                     + cb_ref[h, d0 + 2 * a - qr])
                if 2 * a < first:
                    t = jnp.where(lane < GRID_W, MASKED, t)
                if 2 * a + 1 > last:
                    t = jnp.where(lane >= GRID_W, MASKED, t)
                tiles.append(t)
            m = jnp.max(functools.reduce(jnp.maximum, tiles), axis=-1, keepdims=True)
            ps = [jnp.exp2(t - m) for t in tiles]
            l_rows.append(jnp.sum(functools.reduce(jnp.add, ps), axis=-1, keepdims=True))
            p_rows.append(jnp.concatenate(
                [zeros] * (first // 2 - pa) + [p.astype(BF16) for p in ps] + [zeros] * (pb - 1 - last // 2),
                axis=1))
        p = jnp.concatenate(p_rows, axis=0)
        l = jnp.concatenate(l_rows, axis=0)
        if lookahead and h + 1 < N_HEADS_B:
            s_next = score(heads[h + 1])
        v = v_ref[0, pl.ds(start, n_keys), cols]
        o_ref[0, :, cols] = (jnp.dot(p, v, preferred_element_type=F32) / l).astype(BF16)


def _nbr_kernel(q_ref, k_ref, v_ref, cb_ref, o_ref, *, rows):
    j = pl.program_id(1)
    ws = _nbr_window_start(j, rows)
    for (lo, d0), members in _nbr_block_plans(rows).items():
        member = functools.reduce(jnp.logical_or, [j == m for m in members])
        pl.when(member)(functools.partial(_nbr_block, q_ref, k_ref, v_ref, cb_ref, o_ref, ws, lo, d0))


def _nbr(z, col_bias, q_col, k_col, v_col):
    b, s, _ = z.shape
    rows = s // GRID_W
    assert rows % NBR_Q_ROWS == 0 and rows >= NBR_KEY_ROWS
    return pl.pallas_call(
        functools.partial(_nbr_kernel, rows=rows),
        out_shape=jax.ShapeDtypeStruct((b, s, WIDTH_B), BF16),
        grid=(b, rows // NBR_Q_ROWS),
        in_specs=[
            pl.BlockSpec((1, NBR_Q, WIDTH_B), lambda bi, j: (bi, j, q_col)),
            pl.BlockSpec((1, s, WIDTH_B), lambda bi, j: (bi, 0, k_col)),
            pl.BlockSpec((1, s, WIDTH_B), lambda bi, j: (bi, 0, v_col)),
            _resident(col_bias.shape),
        ],
        out_specs=pl.BlockSpec((1, NBR_Q, WIDTH_B), lambda bi, j: (bi, j, 0)),
        compiler_params=_params(("parallel", "arbitrary"), 48),
        name="nbr",
    )(z, z, z, col_bias)


def _nbr_col_bias(rpb):
    n_dr, n_dc = 2 * WIN_ROWS - 1, 2 * WIN_COLS - 1
    c = np.arange(GRID_W)[:, None]
    kc = np.arange(GRID_W)[None, :]
    cs = np.clip(c - WIN_COLS // 2, 0, GRID_W - WIN_COLS)
    col_valid = (kc >= cs) & (kc < cs + WIN_COLS)
    onehot = ((kc - c + (WIN_COLS - 1))[None] == np.arange(n_dc)[:, None, None]) & col_valid[None]
    onehot = jnp.asarray(onehot.reshape(n_dc, GRID_W * GRID_W), F32)
    tm = jnp.einsum("hrd,dn->hrn", rpb * LOG2_E, onehot, precision=lax.Precision.HIGHEST)
    tm = jnp.where(col_valid.reshape(-1), tm, MASKED).reshape(-1, n_dr, GRID_W, GRID_W)
    hi_pad = NBR_DR_SLOTS + 1 - NBR_DR_PAD - n_dr
    tm = jnp.pad(tm, ((0, 0), (NBR_DR_PAD, hi_pad), (0, 0), (0, 0)), constant_values=MASKED)
    return jnp.concatenate([tm[:, :-1], tm[:, 1:]], axis=-1)


MERGE_TN = 512


def _merge_kernel(oa_ref, ob_ref, ga_ref, gb_ref, h_ref, wa_ref, wb_ref, wo_ref, out_ref, m_ref):
    oa = oa_ref[...]
    ob = ob_ref[...]
    for c in range(D_MODEL // MERGE_TN):
        cols = slice(c * MERGE_TN, (c + 1) * MERGE_TN)
        a = jnp.dot(oa, wa_ref[:, cols], preferred_element_type=F32)
        b = jnp.dot(ob, wb_ref[:, cols], preferred_element_type=F32)
        merged = (jax.nn.sigmoid(ga_ref[:, cols].astype(F32)) * a
                  + jax.nn.sigmoid(gb_ref[:, cols].astype(F32)) * b)
        m_ref[:, cols] = merged.astype(BF16)
    out_ref[...] = h_ref[...] + jnp.dot(m_ref[...], wo_ref[...], preferred_element_type=F32)


def _merge(oa, ob, z, ga_col, gb_col, h, wa, wb, wo, *, tm=512):
    t, d = h.shape
    assert t % tm == 0 and d == D_MODEL
    return pl.pallas_call(
        _merge_kernel,
        out_shape=jax.ShapeDtypeStruct((t, d), F32),
        grid=(t // tm,),
        in_specs=[
            pl.BlockSpec((tm, WIDTH_A), lambda i: (i, 0)),
            pl.BlockSpec((tm, WIDTH_B), lambda i: (i, 0)),
            pl.BlockSpec((tm, d), lambda i: (i, ga_col)),
            pl.BlockSpec((tm, d), lambda i: (i, gb_col)),
            pl.BlockSpec((tm, d), lambda i: (i, 0)),
            _resident((WIDTH_A, d)),
            _resident((WIDTH_B, d)),
            _resident((d, d)),
        ],
        out_specs=pl.BlockSpec((tm, d), lambda i: (i, 0)),
        scratch_shapes=[pltpu.VMEM((tm, d), BF16)],
        compiler_params=_params(("parallel",), 56),
        name="merge",
    )(oa, ob, z, z, h, wa, wb, wo)


def _ple_kernel(h_ref, n_ref, p_ref, wg_ref, wp_ref, gf_ref, y_ref):
    gate = jax.nn.sigmoid(jnp.dot(n_ref[...], wg_ref[...], preferred_element_type=F32))
    emb = jnp.dot(p_ref[...].astype(BF16), wp_ref[...], preferred_element_type=F32)
    y_ref[...] = _rms(h_ref[...] + gate * emb) * gf_ref[...]


def _ple(h, n, row0, p, wg, wp, gf, *, tm=512):
    t, d = p.shape[0], h.shape[1]
    assert t % tm == 0 and row0 % tm == 0
    tile0 = row0 // tm
    return pl.pallas_call(
        _ple_kernel,
        out_shape=jax.ShapeDtypeStruct((t, d), F32),
        grid=(t // tm,),
        in_specs=[
            pl.BlockSpec((tm, d), lambda i: (i + tile0, 0)),
            pl.BlockSpec((tm, d), lambda i: (i + tile0, 0)),
            pl.BlockSpec((tm, D_PLE), lambda i: (i, 0)),
            _resident((d, d)),
            _resident((D_PLE, d)),
            _resident((1, d)),
        ],
        out_specs=pl.BlockSpec((tm, d), lambda i: (i, 0)),
        compiler_params=_params(("parallel",), 48),
        name="ple",
    )(h, n, p, wg, wp, gf)


CAST_BLOCK_BYTES = 4 * MIB


def _cast_kernel(w_ref, o_ref, *, scale):
    w = w_ref[...]
    o_ref[...] = (w if scale == 1.0 else w * scale).astype(BF16)


def _to_bf16(w, layer, scale=1.0):
    _, r, c = w.shape
    target = max(16, CAST_BLOCK_BYTES // (4 * c))
    rows = next(n for n in range(min(r, target) // 16 * 16, 0, -16) if r % n == 0)
    return pl.pallas_call(
        functools.partial(_cast_kernel, scale=scale),
        out_shape=jax.ShapeDtypeStruct((r, c), BF16),
        grid=(r // rows,),
        in_specs=[pl.BlockSpec((None, rows, c), lambda i: (layer, i, 0))],
        out_specs=pl.BlockSpec((rows, c), lambda i: (i, 0)),
        compiler_params=_params(("parallel",), 32),
        name="to_bf16",
    )(w)


def _pack_w_in_kernel(w_ref, qkv_ref, proj_ref, *, qkv_width, chunk_starts, tn):
    qkv_ref[...] = w_ref[:, :qkv_width].astype(BF16)
    for c, start in enumerate(chunk_starts):
        proj_ref[c] = w_ref[:, start:start + tn].astype(BF16)


def _pack_w_in(w_in, layer, qkv_width, chunk_starts, tn, *, rows=64):
    _, d, n = w_in.shape
    assert d % rows == 0 and all(s % HEAD_DIM == 0 and s + tn <= n for s in chunk_starts)
    return pl.pallas_call(
        functools.partial(_pack_w_in_kernel, qkv_width=qkv_width, chunk_starts=chunk_starts, tn=tn),
        out_shape=(jax.ShapeDtypeStruct((d, qkv_width), BF16),
                   jax.ShapeDtypeStruct((len(chunk_starts), d, tn), BF16)),
        grid=(d // rows,),
        in_specs=[pl.BlockSpec((None, rows, n), lambda i: (layer, i, 0))],
        out_specs=(pl.BlockSpec((rows, qkv_width), lambda i: (i, 0)),
                   pl.BlockSpec((len(chunk_starts), rows, tn), lambda i: (0, i, 0))),
        compiler_params=_params(("parallel",), 32),
        name="pack_w_in",
    )(w_in)


def _pack_gate_up_kernel(*refs, tf):
    o_ref = refs[-1]
    for slot in range(len(refs) // 2):
        @pl.when(pl.program_id(0) == slot)
        def _(g_ref=refs[2 * slot], u_ref=refs[2 * slot + 1]):
            for j in range(o_ref.shape[0]):
                o_ref[j, :, :tf] = g_ref[:, j * tf:(j + 1) * tf].astype(BF16)
                o_ref[j, :, tf:] = u_ref[:, j * tf:(j + 1) * tf].astype(BF16)


def _pack_gate_up(gate_up_pairs, layer, *, tf, rows=64):
    _, r, c = gate_up_pairs[0][0].shape
    assert r % rows == 0 and c % tf == 0
    n_chunks, n_blocks = c // tf, r // rows

    def in_spec(slot):
        return pl.BlockSpec((None, rows, c), lambda s, i: (layer, jnp.where(s == slot, i, n_blocks - 1), 0))

    weights = [w for pair in gate_up_pairs for w in pair]
    assert all(w.shape == weights[0].shape for w in weights)
    return pl.pallas_call(
        functools.partial(_pack_gate_up_kernel, tf=tf),
        out_shape=jax.ShapeDtypeStruct((len(gate_up_pairs), n_chunks, r, 2 * tf), BF16),
        grid=(len(gate_up_pairs), n_blocks),
        in_specs=[in_spec(slot) for slot in range(len(gate_up_pairs)) for _ in range(2)],
        out_specs=pl.BlockSpec((None, n_chunks, rows, 2 * tf), lambda s, i: (s, 0, i, 0)),
        compiler_params=_params(("arbitrary", "arbitrary"), 32),
        name="pack_gate_up",
    )(*weights)


def _rotary_tables(seq, gain_q, gain_k):
    n_freq = HEAD_DIM // 4
    inv_freq = ROPE_THETA ** (-jnp.arange(n_freq, dtype=F32) / n_freq)
    t = jnp.arange(seq)
    row = (t // GRID_W).astype(F32)
    col = (t % GRID_W).astype(F32)
    ang = jnp.concatenate([row[:, None] * inv_freq[None], col[:, None] * inv_freq[None]], axis=-1)
    cos = jnp.concatenate([jnp.cos(ang), jnp.cos(ang)], axis=-1)
    sin = jnp.concatenate([-jnp.sin(ang), jnp.sin(ang)], axis=-1)
    return jnp.concatenate([cos * g if part == 0 else sin * jnp.roll(g, HEAD_DIM // 2)
                            for g in (gain_q, gain_k) for part in (0, 1)], axis=-1)


_W_IN_SEGMENTS = dict(q_b=QKV_A_WIDTH, k_b=QKV_A_WIDTH + WIDTH_B, v_b=QKV_A_WIDTH + 2 * WIDTH_B,
                      g_a=QKV_A_WIDTH + 3 * WIDTH_B, g_b=QKV_A_WIDTH + 3 * WIDTH_B + D_MODEL)
_PROJ_LAYOUT = (("g_a", D_MODEL), ("g_b", D_MODEL), ("q_b", WIDTH_B), ("k_b", WIDTH_B), ("v_b", WIDTH_B))
_PROJ_CHUNK_STARTS = tuple(_W_IN_SEGMENTS[name] + off
                           for name, width in _PROJ_LAYOUT for off in range(0, width, PROJ_TN))
assert all(width % PROJ_TN == 0 for _, width in _PROJ_LAYOUT)
_GA_COL, _GB_COL = 0, 1
_QB_COL, _KB_COL, _VB_COL = 4, 5, 6
assert 2 * D_MODEL == _QB_COL * WIDTH_B


def _prepare_weights(ffn1_norm, ffn1_w_gate, ffn1_w_up, ffn1_w_down, mix_norm, w_in, q_norm, k_norm,
                     w_branch_a, w_branch_b, w_out, ffn2_norm, ffn2_w_gate, ffn2_w_up, ffn2_w_down,
                     ple_norm, w_ple_gate, w_ple_proj, final_norm, layer):
    i = layer
    half = np.concatenate([np.arange(0, HEAD_DIM, 2), np.arange(1, HEAD_DIM, 2)])
    w_qkv_a, w_proj = _pack_w_in(w_in, i, QKV_A_WIDTH, _PROJ_CHUNK_STARTS, PROJ_TN)
    w_rot = _rotary_weight_layout(w_qkv_a)
    w_gate_up = _pack_gate_up([(ffn1_w_gate, ffn1_w_up), (ffn2_w_gate, ffn2_w_up)], i, tf=FFN_TF)
    q_scale = ATTN_SCALE * LOG2_E
    proj_scale = jnp.concatenate([jnp.full((width,), q_scale if name == "q_b" else 1.0, F32)
                                  for name, width in _PROJ_LAYOUT])[None]
    rot_gains = ((q_norm[i] * q_scale)[half], k_norm[i][half])
    gqa_score_bound = (HEAD_DIM * q_scale * BF16_NORM_MARGIN
                       * jnp.max(jnp.abs(q_norm[i])) * jnp.max(jnp.abs(k_norm[i])))
    return dict(
        ffn1=(ffn1_norm[i][None], w_gate_up, 0, _to_bf16(ffn1_w_down, i, FFN_RESIDUAL_SCALE), mix_norm[i][None]),
        w_rot=w_rot, rot_gains=rot_gains, w_qkv_a=w_qkv_a, w_proj=w_proj, proj_scale=proj_scale, gqa_score_bound=gqa_score_bound,
        merge=(_to_bf16(w_branch_a, i), _to_bf16(w_branch_b, i), _to_bf16(w_out, i)),
        ffn2=(ffn2_norm[i][None], w_gate_up, 1, _to_bf16(ffn2_w_down, i, FFN_RESIDUAL_SCALE), ple_norm[i][None]),
        ple=(_to_bf16(w_ple_gate, i), _to_bf16(w_ple_proj, i), final_norm[None]),
    )


def _encoder(xs, ps, wts, rot, col_bias):
    s, d = xs[0].shape[1:]
    assert all(x.shape[1:] == (s, d) for x in xs)
    b = sum(x.shape[0] for x in xs)
    t = b * s
    h1, u = _ffn([x.reshape(-1, d) for x in xs], *wts["ffn1"])
    q_a, k_a, v_a = _qkv_a(u, wts["w_rot"], wts["w_qkv_a"], rot, s)
    z = _proj(u, wts["w_proj"], wts["proj_scale"])
    qkv = (q_a.reshape(b, s, -1), k_a.reshape(b, s, -1), v_a.reshape(b, s, -1))
    o_a = lax.cond(wts["gqa_score_bound"] <= GQA_SAFE_LOG2_SCORE,
                   functools.partial(_gqa, bounded=True), functools.partial(_gqa, bounded=False), *qkv)
    o_b = _nbr(z.reshape(b, s, -1), col_bias, _QB_COL, _KB_COL, _VB_COL)
    h2 = _merge(o_a.reshape(t, -1), o_b.reshape(t, -1), z, _GA_COL, _GB_COL, h1, *wts["merge"])
    h3, n = _ffn([h2], *wts["ffn2"])
    row_starts = np.cumsum([0] + [x.shape[0] * s for x in xs])
    return tuple(_ple(h3, n, int(row0), p.reshape(-1, p.shape[-1]), *wts["ple"]).reshape(x.shape)
                 for row0, p, x in zip(row_starts, ps, xs))


def kernel(x_prompt, x_sample, p_prompt, p_sample, ffn1_norm, ffn1_w_gate, ffn1_w_up, ffn1_w_down, mix_norm, w_in, q_norm, k_norm, nat_rpb, w_branch_a, w_branch_b, w_out, ffn2_norm, ffn2_w_gate, ffn2_w_up, ffn2_w_down, ple_norm, w_ple_gate, w_ple_proj, final_norm):
    assert ffn1_norm.shape[0] == 1, "single-layer encoder"
    wts = _prepare_weights(ffn1_norm, ffn1_w_gate, ffn1_w_up, ffn1_w_down, mix_norm, w_in, q_norm, k_norm,
                           w_branch_a, w_branch_b, w_out, ffn2_norm, ffn2_w_gate, ffn2_w_up, ffn2_w_down,
                           ple_norm, w_ple_gate, w_ple_proj, final_norm, 0)
    col_bias = _nbr_col_bias(nat_rpb[0])
    groups = ((x_prompt, p_prompt[0]), (x_sample, p_sample[0]))
    outs = {}
    for seq in sorted({x.shape[1] for x, _ in groups}):
        members = [k for k, (x, _) in enumerate(groups) if x.shape[1] == seq]
        rot = _rotary_tables(seq, *wts["rot_gains"])
        ys = _encoder([groups[k][0] for k in members], [groups[k][1] for k in members], wts, rot, col_bias)
        outs.update(zip(members, ys))
    return tuple(outs[k] for k in range(len(groups)))
```

```python
import functools
import math

import numpy as np
import jax
import jax.numpy as jnp
from jax import lax
from jax.experimental import pallas as pl
from jax.experimental.pallas import tpu as pltpu

F32 = jnp.float32
BF16 = jnp.bfloat16

D_MODEL = 2048
HEAD_DIM = 128
N_HEADS_A = 8
N_KV_A = 2
GQA_GROUP = N_HEADS_A // N_KV_A
N_HEADS_B = 8
WIDTH_A = N_HEADS_A * HEAD_DIM
KV_WIDTH_A = N_KV_A * HEAD_DIM
WIDTH_B = N_HEADS_B * HEAD_DIM
ROT_WIDTH = WIDTH_A + KV_WIDTH_A
QKV_A_WIDTH = ROT_WIDTH + KV_WIDTH_A
D_FF = 5632
D_PLE = 256
GRID_W = 64
WIN_ROWS = 8
WIN_COLS = 16
ROPE_THETA = 10000.0
EPS = 1e-6
ATTN_SCALE = HEAD_DIM ** -0.5
LOG2_E = math.log2(math.e)
MASKED = -1e30
GQA_SAFE_LOG2_SCORE = 48.0
BF16_NORM_MARGIN = 1.02

NBR_Q_ROWS = 4
NBR_KEY_ROWS = 12
NBR_Q = NBR_Q_ROWS * GRID_W
NBR_KEYS = NBR_KEY_ROWS * GRID_W
NBR_DR_PAD = (NBR_KEY_ROWS - NBR_Q_ROWS) - (WIN_ROWS - 1) + (NBR_Q_ROWS - 1)
NBR_DR_SLOTS = NBR_DR_PAD + (WIN_ROWS - 1) + (NBR_KEY_ROWS - 2) + 1

PROJ_TN = 1024

V7X_VMEM_BYTES = 64 * 1024 * 1024
MIB = 1024 * 1024


def _params(semantics, vmem_mib):
    assert vmem_mib * MIB < V7X_VMEM_BYTES
    return pltpu.CompilerParams(dimension_semantics=semantics, vmem_limit_bytes=vmem_mib * MIB)


def _resident(shape):
    zeros = (0,) * len(shape)
    return pl.BlockSpec(shape, lambda *_: zeros, pipeline_mode=pl.Buffered(1))


def _rms(x):
    return x * lax.rsqrt(jnp.mean(x * x, axis=-1, keepdims=True) + EPS)


FFN_RESIDUAL_SCALE = 0.5
FFN_TF = 512
ROW_CHUNK = 16
NORM_ROWS = 128
ROW_UNROLL_EXIT = 16

def _ffn_kernel(*refs, tile_starts):
    n_src = len(tile_starts) - 1
    x_srcs = refs[:n_src]
    g1_ref, wgu_ref, wd_ref, g2_ref, h_ref, n_ref, x_buf, xn_ref, x_sem = refs[n_src:]
    i = pl.program_id(0)
    j = pl.program_id(1)
    tm = x_buf.shape[0]

    def x_copy(tile, wait):
        for s, x_hbm in enumerate(x_srcs):
            @pl.when((tile >= tile_starts[s]) & (tile < tile_starts[s + 1]))
            def _(s=s, x_hbm=x_hbm):
                row = pl.multiple_of((tile - tile_starts[s]) * tm, tm)
                copy = pltpu.make_async_copy(x_hbm.at[pl.ds(row, tm), :], x_buf, x_sem)
                copy.wait() if wait else copy.start()

    cur = i % 2
    has_next = i + 1 < pl.num_programs(0)
    n_chunks = tm // NORM_ROWS

    def norm_rows(slot, row0, n_rows):
        for r in range(0, n_rows, ROW_CHUNK):
            rows = pl.ds(row0 + r, ROW_CHUNK)
            xn_ref[slot, rows, :] = (_rms(x_buf[rows, :]) * g1_ref[...]).astype(BF16)

    @pl.when((i == 0) & (j == 0))
    def _():
        x_copy(0, wait=False)
        x_copy(0, wait=True)

        def first(c, carry):
            norm_rows(0, pl.multiple_of(c * NORM_ROWS, NORM_ROWS), NORM_ROWS)
            return carry

        lax.fori_loop(0, n_chunks, first, 0)

    @pl.when((j == 1) & has_next)
    def _():
        x_copy(i + 1, wait=False)

    @pl.when((j == 2) & has_next)
    def _():
        x_copy(i + 1, wait=True)

    tf = wd_ref.shape[0]

    def step(first_step, prepare_next):
        gu = jnp.dot(xn_ref[cur], wgu_ref[...], preferred_element_type=F32)
        g, u = gu[:, :tf], gu[:, tf:]
        a = (g * jax.nn.sigmoid(g) * u).astype(BF16)
        down = jnp.dot(a, wd_ref[...], preferred_element_type=F32)
        h_ref[...] = (x_buf[...] if first_step else h_ref[...]) + down
        if prepare_next:
            chunk = jnp.minimum(j - 2, n_chunks - 1)
            norm_rows(1 - cur, pl.multiple_of(chunk * NORM_ROWS, NORM_ROWS), NORM_ROWS)

    pl.when(j == 0)(functools.partial(step, True, False))
    pl.when(j == 1)(functools.partial(step, False, False))
    pl.when(j >= 2)(functools.partial(step, False, True))

    @pl.when(j == pl.num_programs(1) - 1)
    def _():
        def last(r, carry):
            rows = pl.ds(pl.multiple_of(r * ROW_CHUNK, ROW_CHUNK), ROW_CHUNK)
            n_ref[rows, :] = (_rms(h_ref[rows, :]) * g2_ref[...]).astype(BF16)
            return carry

        lax.fori_loop(0, tm // ROW_CHUNK, last, 0, unroll=ROW_UNROLL_EXIT)


def _ffn(xs, g1, w_gate_up, slot, wd, g2, *, tm=1024):
    d = xs[0].shape[1]
    dff, tf = wd.shape[0], FFN_TF
    assert all(x.shape[0] % tm == 0 and x.shape[1] == d for x in xs)
    assert w_gate_up.shape[1:] == (dff // tf, d, 2 * tf)
    assert tm % NORM_ROWS == 0 and tm // NORM_ROWS <= dff // tf - 2
    tile_starts = tuple(int(n) for n in np.cumsum([0] + [x.shape[0] // tm for x in xs]))
    t = tile_starts[-1] * tm
    return pl.pallas_call(
        functools.partial(_ffn_kernel, tile_starts=tile_starts),
        out_shape=(jax.ShapeDtypeStruct((t, d), F32), jax.ShapeDtypeStruct((t, d), BF16)),
        grid=(t // tm, dff // tf),
        in_specs=[pl.BlockSpec(memory_space=pl.ANY)] * len(xs) + [
            pl.BlockSpec((1, d), lambda i, j: (0, 0)),
            pl.BlockSpec((None, None, d, 2 * tf), lambda i, j: (slot, j, 0, 0)),
            pl.BlockSpec((tf, d), lambda i, j: (j, 0)),
            pl.BlockSpec((1, d), lambda i, j: (0, 0)),
        ],
        out_specs=(
            pl.BlockSpec((tm, d), lambda i, j: (i, 0)),
            pl.BlockSpec((tm, d), lambda i, j: (i, 0)),
        ),
        scratch_shapes=[pltpu.VMEM((tm, d), F32), pltpu.VMEM((2, tm, d), BF16), pltpu.SemaphoreType.DMA(())],
        compiler_params=_params(("arbitrary", "arbitrary"), 60),
        name="ffn",
    )(*xs, g1, w_gate_up, wd, g2)


def _permute_heads_kernel(w_ref, perm_ref, o_ref):
    o_ref[...] = jnp.dot(w_ref[...], perm_ref[...], preferred_element_type=F32).astype(BF16)


def _rotary_weight_layout(w_in):
    d = w_in.shape[0]
    half = np.concatenate([np.arange(0, HEAD_DIM, 2), np.arange(1, HEAD_DIM, 2)])
    perm = jnp.asarray(np.arange(HEAD_DIM)[:, None] == half[None, :], BF16)
    return pl.pallas_call(
        _permute_heads_kernel,
        out_shape=jax.ShapeDtypeStruct((d, ROT_WIDTH), BF16),
        grid=(ROT_WIDTH // HEAD_DIM,),
        in_specs=[pl.BlockSpec((d, HEAD_DIM), lambda h: (0, h)), _resident((HEAD_DIM, HEAD_DIM))],
        out_specs=pl.BlockSpec((d, HEAD_DIM), lambda h: (0, h)),
        compiler_params=_params(("parallel",), 16),
        name="rotary_weight_layout",
    )(w_in, perm)


def _qkv_a_kernel(u_ref, wr_ref, wv_ref, rot_ref, q_ref, k_ref, v_ref, z_buf, v_buf, *, n_tiles):
    i = pl.program_id(0)

    def project(slot):
        u = u_ref[...]
        z_buf[slot] = jnp.dot(u, wr_ref[...], preferred_element_type=F32)
        v_buf[slot] = jnp.dot(u, wv_ref[...], preferred_element_type=F32).astype(BF16)

    def finish(slot):
        for hh in range(N_HEADS_A + N_KV_A):
            zh = z_buf[slot, :, hh * HEAD_DIM:(hh + 1) * HEAD_DIM]
            tab = 0 if hh < N_HEADS_A else 2
            cos_g = rot_ref[:, tab * HEAD_DIM:(tab + 1) * HEAD_DIM]
            sin_g = rot_ref[:, (tab + 1) * HEAD_DIM:(tab + 2) * HEAD_DIM]
            r = lax.rsqrt(jnp.mean(zh * zh, axis=-1, keepdims=True) + EPS)
            out = ((zh * cos_g + pltpu.roll(zh, HEAD_DIM // 2, 1) * sin_g) * r).astype(BF16)
            if hh < N_HEADS_A:
                q_ref[:, hh * HEAD_DIM:(hh + 1) * HEAD_DIM] = out
            else:
                kk = hh - N_HEADS_A
                k_ref[:, kk * HEAD_DIM:(kk + 1) * HEAD_DIM] = out
        v_ref[...] = v_buf[slot]

    @pl.when(i == 0)
    def _():
        project(0)

    for parity in range(2):
        @pl.when((i > 0) & (i < n_tiles) & (i % 2 == parity))
        def _(parity=parity):
            project(parity)
            finish(1 - parity)

    @pl.when(i == n_tiles)
    def _():
        finish((n_tiles - 1) % 2)


def _qkv_a(u, w_rot, w_in, rot, seq, *, tm=512):
    t, d = u.shape
    assert t % tm == 0 and seq % tm == 0 and ROT_WIDTH % KV_WIDTH_A == 0
    n_tiles, seq_blocks = t // tm, seq // tm

    def lagged(i):
        return jnp.maximum(i - 1, 0)

    return pl.pallas_call(
        functools.partial(_qkv_a_kernel, n_tiles=n_tiles),
        out_shape=(
            jax.ShapeDtypeStruct((t, WIDTH_A), BF16),
            jax.ShapeDtypeStruct((t, KV_WIDTH_A), BF16),
            jax.ShapeDtypeStruct((t, KV_WIDTH_A), BF16),
        ),
        grid=(n_tiles + 1,),
        in_specs=[
            pl.BlockSpec((tm, d), lambda i: (jnp.minimum(i, n_tiles - 1), 0)),
            _resident((d, ROT_WIDTH)),
            pl.BlockSpec((d, KV_WIDTH_A), lambda i: (0, ROT_WIDTH // KV_WIDTH_A), pipeline_mode=pl.Buffered(1)),
            pl.BlockSpec((tm, 4 * HEAD_DIM), lambda i: (lagged(i) % seq_blocks, 0)),
        ],
        out_specs=(
            pl.BlockSpec((tm, WIDTH_A), lambda i: (lagged(i), 0)),
            pl.BlockSpec((tm, KV_WIDTH_A), lambda i: (lagged(i), 0)),
            pl.BlockSpec((tm, KV_WIDTH_A), lambda i: (lagged(i), 0)),
        ),
        scratch_shapes=[pltpu.VMEM((2, tm, ROT_WIDTH), F32), pltpu.VMEM((2, tm, KV_WIDTH_A), BF16)],
        compiler_params=_params(("arbitrary",), 40),
        name="qkv_a",
    )(u, w_rot, w_in, rot)


def _proj_kernel(u_ref, w_ref, scale_ref, z_ref):
    z = jnp.dot(u_ref[...], w_ref[...], preferred_element_type=F32)
    z_ref[...] = (z * scale_ref[...]).astype(BF16)


def _proj(u, w, col_scale, *, tm=2048):
    t, d = u.shape
    n_chunks, _, tn = w.shape
    assert t % tm == 0 and col_scale.shape == (1, n_chunks * tn)
    return pl.pallas_call(
        _proj_kernel,
        out_shape=jax.ShapeDtypeStruct((t, n_chunks * tn), BF16),
        grid=(t // tm, n_chunks),
        in_specs=[
            pl.BlockSpec((tm, d), lambda i, j: (i, 0)),
            pl.BlockSpec((None, d, tn), lambda i, j: (j, 0, 0)),
            pl.BlockSpec((1, tn), lambda i, j: (0, j)),
        ],
        out_specs=pl.BlockSpec((tm, tn), lambda i, j: (i, j)),
        compiler_params=_params(("parallel", "arbitrary"), 48),
        name="proj",
    )(u, w, col_scale)


def _softmax_pv(s, v):
    m = jnp.max(s, axis=-1, keepdims=True)
    p = jnp.exp2(s - m)
    l = jnp.sum(p, axis=-1, keepdims=True)
    return jnp.dot(p.astype(BF16), v, preferred_element_type=F32) / l


_NT = (((1,), (1,)), ((), ()))


def _gqa_kernel(q_ref, k_ref, v_ref, o_ref, *, bounded):
    k = k_ref[0]
    v = v_ref[0]

    def score(g):
        return lax.dot_general(q_ref[0, :, g * HEAD_DIM:(g + 1) * HEAD_DIM], k, _NT, preferred_element_type=F32)

    s_next = score(0)
    for g in range(GQA_GROUP):
        s = s_next
        if g + 1 < GQA_GROUP:
            s_next = score(g + 1)
        if bounded:
            p = jnp.exp2(s)
            l = jnp.sum(p, axis=-1, keepdims=True)
            o = jnp.dot(p.astype(BF16), v, preferred_element_type=F32) / l
        else:
            o = _softmax_pv(s, v)
        o_ref[0, :, g * HEAD_DIM:(g + 1) * HEAD_DIM] = o.astype(BF16)


def _gqa(q, k, v, *, bounded, tq=1024):
    b, s, _ = q.shape
    assert s % tq == 0
    gw = GQA_GROUP * HEAD_DIM
    return pl.pallas_call(
        functools.partial(_gqa_kernel, bounded=bounded),
        out_shape=jax.ShapeDtypeStruct((b, s, WIDTH_A), BF16),
        grid=(b, N_KV_A, s // tq),
        in_specs=[
            pl.BlockSpec((1, tq, gw), lambda bi, kh, qi: (bi, qi, kh)),
            pl.BlockSpec((1, s, HEAD_DIM), lambda bi, kh, qi: (bi, 0, kh)),
            pl.BlockSpec((1, s, HEAD_DIM), lambda bi, kh, qi: (bi, 0, kh)),
        ],
        out_specs=pl.BlockSpec((1, tq, gw), lambda bi, kh, qi: (bi, qi, kh)),
        compiler_params=_params(("parallel", "parallel", "arbitrary"), 40),
        name="gqa_bounded" if bounded else "gqa",
    )(q, k, v)


def _nbr_window_start(j, rows):
    return jnp.clip(NBR_Q_ROWS * j - WIN_ROWS // 2, 0, rows - NBR_KEY_ROWS)


def _nbr_block_plans(rows):
    plans = {}
    for j in range(rows // NBR_Q_ROWS):
        ws = int(np.clip(NBR_Q_ROWS * j - WIN_ROWS // 2, 0, rows - NBR_KEY_ROWS))
        lo = tuple(int(np.clip(NBR_Q_ROWS * j + qr - WIN_ROWS // 2, 0, rows - WIN_ROWS)) - ws
                   for qr in range(NBR_Q_ROWS))
        d0 = ws - NBR_Q_ROWS * j + (WIN_ROWS - 1) + NBR_DR_PAD
        assert min(lo) >= 0 and max(lo) + WIN_ROWS <= NBR_KEY_ROWS
        plans.setdefault((lo, d0), []).append(j)
    return plans


def _nbr_block(q_ref, k_ref, v_ref, cb_ref, o_ref, ws, lo, d0):
    pair_w = 2 * GRID_W
    pa = min(lo) // 2
    pb = (max(lo) + WIN_ROWS - 1) // 2 + 1
    n_keys = (pb - pa) * pair_w
    start = pl.multiple_of(ws * GRID_W + pa * pair_w, pair_w)
    lane = lax.broadcasted_iota(jnp.int32, (GRID_W, pair_w), 1)
    zeros = jnp.zeros((GRID_W, pair_w), BF16)
    heads = [slice(h * HEAD_DIM, (h + 1) * HEAD_DIM) for h in range(N_HEADS_B)]

    def score(cols):
        return lax.dot_general(q_ref[0, :, cols], k_ref[pl.ds(start, n_keys), cols], _NT,
                               preferred_element_type=F32)

    lookahead = len(set(lo)) > 1
    s_next = score(heads[0])
    for h, cols in enumerate(heads):
        s = s_next if lookahead or h == 0 else score(cols)
        p_rows, l_rows = [], []
        for qr in range(NBR_Q_ROWS):
            first, last = lo[qr], lo[qr] + WIN_ROWS - 1
            tiles = []
            for a in range(first // 2, last // 2 + 1):
                t = (s[qr * GRID_W:(qr + 1) * GRID_W, (a - pa) * pair_w:(a - pa + 1) * pair_w]
                     + cb_ref[h, d0 + 2 * a - qr])
                if 2 * a < first:
                    t = jnp.where(lane < GRID_W, MASKED, t)
                if 2 * a + 1 > last:
                    t = jnp.where(lane >= GRID_W, MASKED, t)
                tiles.append(t)
            m = jnp.max(functools.reduce(jnp.maximum, tiles), axis=-1, keepdims=True)
            ps = [jnp.exp2(t - m) for t in tiles]
            l_rows.append(jnp.sum(functools.reduce(jnp.add, ps), axis=-1, keepdims=True))
            p_rows.append(jnp.concatenate(
                [zeros] * (first // 2 - pa) + [p.astype(BF16) for p in ps] + [zeros] * (pb - 1 - last // 2),
                axis=1))
        p = jnp.concatenate(p_rows, axis=0)
        l = jnp.concatenate(l_rows, axis=0)
        if lookahead and h + 1 < N_HEADS_B:
            s_next = score(heads[h + 1])
        v = v_ref[pl.ds(start, n_keys), cols]
        o_ref[0, :, cols] = (jnp.dot(p, v, preferred_element_type=F32) / l).astype(BF16)


def _nbr_kernel(q_ref, z_hbm, cb_ref, o_ref, kv_buf, kv_sem, *, rows, kv_cols):
    b = pl.program_id(0)
    j = pl.program_id(1)
    slot = b % 2

    def kv_copies(seq, dst_slot):
        return [pltpu.make_async_copy(z_hbm.at[seq, :, pl.ds(col * WIDTH_B, WIDTH_B)],
                                      kv_buf.at[dst_slot, which], kv_sem.at[which])
                for which, col in enumerate(kv_cols)]

    @pl.when(j == 0)
    def _():
        @pl.when(b == 0)
        def _():
            for copy in kv_copies(0, 0):
                copy.start()

        for copy in kv_copies(b, slot):
            copy.wait()

        @pl.when(b + 1 < pl.num_programs(0))
        def _():
            for copy in kv_copies(b + 1, 1 - slot):
                copy.start()

    k_ref, v_ref = kv_buf.at[slot, 0], kv_buf.at[slot, 1]
    ws = _nbr_window_start(j, rows)
    for (lo, d0), members in _nbr_block_plans(rows).items():
        member = functools.reduce(jnp.logical_or, [j == m for m in members])
        pl.when(member)(functools.partial(_nbr_block, q_ref, k_ref, v_ref, cb_ref, o_ref, ws, lo, d0))


def _nbr(z, col_bias, q_col, k_col, v_col):
    b, s, _ = z.shape
    rows = s // GRID_W
    assert rows % NBR_Q_ROWS == 0 and rows >= NBR_KEY_ROWS
    return pl.pallas_call(
        functools.partial(_nbr_kernel, rows=rows, kv_cols=(k_col, v_col)),
        out_shape=jax.ShapeDtypeStruct((b, s, WIDTH_B), BF16),
        grid=(b, rows // NBR_Q_ROWS),
        in_specs=[
            pl.BlockSpec((1, NBR_Q, WIDTH_B), lambda bi, j: (bi, j, q_col)),
            pl.BlockSpec(memory_space=pl.ANY),
            _resident(col_bias.shape),
        ],
        out_specs=pl.BlockSpec((1, NBR_Q, WIDTH_B), lambda bi, j: (bi, j, 0)),
        scratch_shapes=[pltpu.VMEM((2, 2, s, WIDTH_B), BF16), pltpu.SemaphoreType.DMA((2,))],
        compiler_params=_params(("arbitrary", "arbitrary"), 48),
        name="nbr",
    )(z, z, col_bias)


def _nbr_col_bias(rpb):
    n_dr, n_dc = 2 * WIN_ROWS - 1, 2 * WIN_COLS - 1
    c = np.arange(GRID_W)[:, None]
    kc = np.arange(GRID_W)[None, :]
    cs = np.clip(c - WIN_COLS // 2, 0, GRID_W - WIN_COLS)
    col_valid = (kc >= cs) & (kc < cs + WIN_COLS)
    onehot = ((kc - c + (WIN_COLS - 1))[None] == np.arange(n_dc)[:, None, None]) & col_valid[None]
    onehot = jnp.asarray(onehot.reshape(n_dc, GRID_W * GRID_W), F32)
    tm = jnp.einsum("hrd,dn->hrn", rpb * LOG2_E, onehot, precision=lax.Precision.HIGHEST)
    tm = jnp.where(col_valid.reshape(-1), tm, MASKED).reshape(-1, n_dr, GRID_W, GRID_W)
    hi_pad = NBR_DR_SLOTS + 1 - NBR_DR_PAD - n_dr
    tm = jnp.pad(tm, ((0, 0), (NBR_DR_PAD, hi_pad), (0, 0), (0, 0)), constant_values=MASKED)
    return jnp.concatenate([tm[:, :-1], tm[:, 1:]], axis=-1)


MERGE_TN = 512


def _merge_kernel(oa_ref, ob_ref, ga_ref, gb_ref, h_ref, wa_ref, wb_ref, wo_ref, out_ref, m_ref):
    oa = oa_ref[...]
    ob = ob_ref[...]
    for c in range(D_MODEL // MERGE_TN):
        cols = slice(c * MERGE_TN, (c + 1) * MERGE_TN)
        a = jnp.dot(oa, wa_ref[:, cols], preferred_element_type=F32)
        b = jnp.dot(ob, wb_ref[:, cols], preferred_element_type=F32)
        merged = (jax.nn.sigmoid(ga_ref[:, cols].astype(F32)) * a
                  + jax.nn.sigmoid(gb_ref[:, cols].astype(F32)) * b)
        m_ref[:, cols] = merged.astype(BF16)
    out_ref[...] = h_ref[...] + jnp.dot(m_ref[...], wo_ref[...], preferred_element_type=F32)


def _merge(oa, ob, z, ga_col, gb_col, h, wa, wb, wo, *, tm=512):
    t, d = h.shape
    assert t % tm == 0 and d == D_MODEL
    return pl.pallas_call(
        _merge_kernel,
        out_shape=jax.ShapeDtypeStruct((t, d), F32),
        grid=(t // tm,),
        in_specs=[
            pl.BlockSpec((tm, WIDTH_A), lambda i: (i, 0)),
            pl.BlockSpec((tm, WIDTH_B), lambda i: (i, 0)),
            pl.BlockSpec((tm, d), lambda i: (i, ga_col)),
            pl.BlockSpec((tm, d), lambda i: (i, gb_col)),
            pl.BlockSpec((tm, d), lambda i: (i, 0)),
            _resident((WIDTH_A, d)),
            _resident((WIDTH_B, d)),
            _resident((d, d)),
        ],
        out_specs=pl.BlockSpec((tm, d), lambda i: (i, 0)),
        scratch_shapes=[pltpu.VMEM((tm, d), BF16)],
        compiler_params=_params(("parallel",), 56),
        name="merge",
    )(oa, ob, z, z, h, wa, wb, wo)


def _ple_kernel(h_ref, n_ref, p_ref, wg_ref, wp_ref, gf_ref, y_ref):
    gate = jax.nn.sigmoid(jnp.dot(n_ref[...], wg_ref[...], preferred_element_type=F32))
    emb = jnp.dot(p_ref[...].astype(BF16), wp_ref[...], preferred_element_type=F32)
    y_ref[...] = _rms(h_ref[...] + gate * emb) * gf_ref[...]


def _ple(h, n, row0, p, wg, wp, gf, *, tm=512):
    t, d = p.shape[0], h.shape[1]
    assert t % tm == 0 and row0 % tm == 0
    tile0 = row0 // tm
    return pl.pallas_call(
        _ple_kernel,
        out_shape=jax.ShapeDtypeStruct((t, d), F32),
        grid=(t // tm,),
        in_specs=[
            pl.BlockSpec((tm, d), lambda i: (i + tile0, 0)),
            pl.BlockSpec((tm, d), lambda i: (i + tile0, 0)),
            pl.BlockSpec((tm, D_PLE), lambda i: (i, 0)),
            _resident((d, d)),
            _resident((D_PLE, d)),
            _resident((1, d)),
        ],
        out_specs=pl.BlockSpec((tm, d), lambda i: (i, 0)),
        compiler_params=_params(("parallel",), 48),
        name="ple",
    )(h, n, p, wg, wp, gf)


CAST_BLOCK_BYTES = 4 * MIB


def _cast_kernel(w_ref, o_ref, *, scale):
    w = w_ref[...]
    o_ref[...] = (w if scale == 1.0 else w * scale).astype(BF16)


def _to_bf16(w, layer, scale=1.0):
    _, r, c = w.shape
    target = max(16, CAST_BLOCK_BYTES // (4 * c))
    rows = next(n for n in range(min(r, target) // 16 * 16, 0, -16) if r % n == 0)
    return pl.pallas_call(
        functools.partial(_cast_kernel, scale=scale),
        out_shape=jax.ShapeDtypeStruct((r, c), BF16),
        grid=(r // rows,),
        in_specs=[pl.BlockSpec((None, rows, c), lambda i: (layer, i, 0))],
        out_specs=pl.BlockSpec((rows, c), lambda i: (i, 0)),
        compiler_params=_params(("parallel",), 32),
        name="to_bf16",
    )(w)


def _pack_w_in_kernel(w_ref, qkv_ref, proj_ref, *, qkv_width, chunk_starts, tn):
    qkv_ref[...] = w_ref[:, :qkv_width].astype(BF16)
    for c, start in enumerate(chunk_starts):
        proj_ref[c] = w_ref[:, start:start + tn].astype(BF16)


def _pack_w_in(w_in, layer, qkv_width, chunk_starts, tn, *, rows=64):
    _, d, n = w_in.shape
    assert d % rows == 0 and all(s % HEAD_DIM == 0 and s + tn <= n for s in chunk_starts)
    return pl.pallas_call(
        functools.partial(_pack_w_in_kernel, qkv_width=qkv_width, chunk_starts=chunk_starts, tn=tn),
        out_shape=(jax.ShapeDtypeStruct((d, qkv_width), BF16),
                   jax.ShapeDtypeStruct((len(chunk_starts), d, tn), BF16)),
        grid=(d // rows,),
        in_specs=[pl.BlockSpec((None, rows, n), lambda i: (layer, i, 0))],
        out_specs=(pl.BlockSpec((rows, qkv_width), lambda i: (i, 0)),
                   pl.BlockSpec((len(chunk_starts), rows, tn), lambda i: (0, i, 0))),
        compiler_params=_params(("parallel",), 32),
        name="pack_w_in",
    )(w_in)


def _pack_gate_up_kernel(*refs, tf):
    o_ref = refs[-1]
    for slot in range(len(refs) // 2):
        @pl.when(pl.program_id(0) == slot)
        def _(g_ref=refs[2 * slot], u_ref=refs[2 * slot + 1]):
            for j in range(o_ref.shape[0]):
                o_ref[j, :, :tf] = g_ref[:, j * tf:(j + 1) * tf].astype(BF16)
                o_ref[j, :, tf:] = u_ref[:, j * tf:(j + 1) * tf].astype(BF16)


def _pack_gate_up(gate_up_pairs, layer, *, tf, rows=64):
    _, r, c = gate_up_pairs[0][0].shape
    assert r % rows == 0 and c % tf == 0
    n_chunks, n_blocks = c // tf, r // rows

    def in_spec(slot):
        return pl.BlockSpec((None, rows, c), lambda s, i: (layer, jnp.where(s == slot, i, n_blocks - 1), 0))

    weights = [w for pair in gate_up_pairs for w in pair]
    assert all(w.shape == weights[0].shape for w in weights)
    return pl.pallas_call(
        functools.partial(_pack_gate_up_kernel, tf=tf),
        out_shape=jax.ShapeDtypeStruct((len(gate_up_pairs), n_chunks, r, 2 * tf), BF16),
        grid=(len(gate_up_pairs), n_blocks),
        in_specs=[in_spec(slot) for slot in range(len(gate_up_pairs)) for _ in range(2)],
        out_specs=pl.BlockSpec((None, n_chunks, rows, 2 * tf), lambda s, i: (s, 0, i, 0)),
        compiler_params=_params(("arbitrary", "arbitrary"), 32),
        name="pack_gate_up",
    )(*weights)


def _rotary_tables(seq, gain_q, gain_k):
    n_freq = HEAD_DIM // 4
    inv_freq = ROPE_THETA ** (-jnp.arange(n_freq, dtype=F32) / n_freq)
    t = jnp.arange(seq)
    row = (t // GRID_W).astype(F32)
    col = (t % GRID_W).astype(F32)
    ang = jnp.concatenate([row[:, None] * inv_freq[None], col[:, None] * inv_freq[None]], axis=-1)
    cos = jnp.concatenate([jnp.cos(ang), jnp.cos(ang)], axis=-1)
    sin = jnp.concatenate([-jnp.sin(ang), jnp.sin(ang)], axis=-1)
    return jnp.concatenate([cos * g if part == 0 else sin * jnp.roll(g, HEAD_DIM // 2)
                            for g in (gain_q, gain_k) for part in (0, 1)], axis=-1)


_W_IN_SEGMENTS = dict(q_b=QKV_A_WIDTH, k_b=QKV_A_WIDTH + WIDTH_B, v_b=QKV_A_WIDTH + 2 * WIDTH_B,
                      g_a=QKV_A_WIDTH + 3 * WIDTH_B, g_b=QKV_A_WIDTH + 3 * WIDTH_B + D_MODEL)
_PROJ_LAYOUT = (("g_a", D_MODEL), ("g_b", D_MODEL), ("q_b", WIDTH_B), ("k_b", WIDTH_B), ("v_b", WIDTH_B))
_PROJ_CHUNK_STARTS = tuple(_W_IN_SEGMENTS[name] + off
                           for name, width in _PROJ_LAYOUT for off in range(0, width, PROJ_TN))
assert all(width % PROJ_TN == 0 for _, width in _PROJ_LAYOUT)
_GA_COL, _GB_COL = 0, 1
_QB_COL, _KB_COL, _VB_COL = 4, 5, 6
assert 2 * D_MODEL == _QB_COL * WIDTH_B


def _prepare_weights(ffn1_norm, ffn1_w_gate, ffn1_w_up, ffn1_w_down, mix_norm, w_in, q_norm, k_norm,
                     w_branch_a, w_branch_b, w_out, ffn2_norm, ffn2_w_gate, ffn2_w_up, ffn2_w_down,
                     ple_norm, w_ple_gate, w_ple_proj, final_norm, layer):
    i = layer
    half = np.concatenate([np.arange(0, HEAD_DIM, 2), np.arange(1, HEAD_DIM, 2)])
    w_qkv_a, w_proj = _pack_w_in(w_in, i, QKV_A_WIDTH, _PROJ_CHUNK_STARTS, PROJ_TN)
    w_rot = _rotary_weight_layout(w_qkv_a)
    w_gate_up = _pack_gate_up([(ffn1_w_gate, ffn1_w_up), (ffn2_w_gate, ffn2_w_up)], i, tf=FFN_TF)
    q_scale = ATTN_SCALE * LOG2_E
    proj_scale = jnp.concatenate([jnp.full((width,), q_scale if name == "q_b" else 1.0, F32)
                                  for name, width in _PROJ_LAYOUT])[None]
    rot_gains = ((q_norm[i] * q_scale)[half], k_norm[i][half])
    gqa_score_bound = (HEAD_DIM * q_scale * BF16_NORM_MARGIN
                       * jnp.max(jnp.abs(q_norm[i])) * jnp.max(jnp.abs(k_norm[i])))
    return dict(
        ffn1=(ffn1_norm[i][None], w_gate_up, 0, _to_bf16(ffn1_w_down, i, FFN_RESIDUAL_SCALE), mix_norm[i][None]),
        w_rot=w_rot, rot_gains=rot_gains, w_qkv_a=w_qkv_a, w_proj=w_proj, proj_scale=proj_scale, gqa_score_bound=gqa_score_bound,
        merge=(_to_bf16(w_branch_a, i), _to_bf16(w_branch_b, i), _to_bf16(w_out, i)),
        ffn2=(ffn2_norm[i][None], w_gate_up, 1, _to_bf16(ffn2_w_down, i, FFN_RESIDUAL_SCALE), ple_norm[i][None]),
        ple=(_to_bf16(w_ple_gate, i), _to_bf16(w_ple_proj, i), final_norm[None]),
    )


def _encoder(xs, ps, wts, rot, col_bias):
    s, d = xs[0].shape[1:]
    assert all(x.shape[1:] == (s, d) for x in xs)
    b = sum(x.shape[0] for x in xs)
    t = b * s
    h1, u = _ffn([x.reshape(-1, d) for x in xs], *wts["ffn1"])
    q_a, k_a, v_a = _qkv_a(u, wts["w_rot"], wts["w_qkv_a"], rot, s)
    z = _proj(u, wts["w_proj"], wts["proj_scale"])
    qkv = (q_a.reshape(b, s, -1), k_a.reshape(b, s, -1), v_a.reshape(b, s, -1))
    o_a = lax.cond(wts["gqa_score_bound"] <= GQA_SAFE_LOG2_SCORE,
                   functools.partial(_gqa, bounded=True), functools.partial(_gqa, bounded=False), *qkv)
    o_b = _nbr(z.reshape(b, s, -1), col_bias, _QB_COL, _KB_COL, _VB_COL)
    h2 = _merge(o_a.reshape(t, -1), o_b.reshape(t, -1), z, _GA_COL, _GB_COL, h1, *wts["merge"])
    h3, n = _ffn([h2], *wts["ffn2"])
    row_starts = np.cumsum([0] + [x.shape[0] * s for x in xs])
    return tuple(_ple(h3, n, int(row0), p.reshape(-1, p.shape[-1]), *wts["ple"]).reshape(x.shape)
                 for row0, p, x in zip(row_starts, ps, xs))


def kernel(x_prompt, x_sample, p_prompt, p_sample, ffn1_norm, ffn1_w_gate, ffn1_w_up, ffn1_w_down, mix_norm, w_in, q_norm, k_norm, nat_rpb, w_branch_a, w_branch_b, w_out, ffn2_norm, ffn2_w_gate, ffn2_w_up, ffn2_w_down, ple_norm, w_ple_gate, w_ple_proj, final_norm):
    assert ffn1_norm.shape[0] == 1, "single-layer encoder"
    wts = _prepare_weights(ffn1_norm, ffn1_w_gate, ffn1_w_up, ffn1_w_down, mix_norm, w_in, q_norm, k_norm,
                           w_branch_a, w_branch_b, w_out, ffn2_norm, ffn2_w_gate, ffn2_w_up, ffn2_w_down,
                           ple_norm, w_ple_gate, w_ple_proj, final_norm, 0)
    col_bias = _nbr_col_bias(nat_rpb[0])
    groups = ((x_prompt, p_prompt[0]), (x_sample, p_sample[0]))
    outs = {}
    for seq in sorted({x.shape[1] for x, _ in groups}):
        members = [k for k, (x, _) in enumerate(groups) if x.shape[1] == seq]
        rot = _rotary_tables(seq, *wts["rot_gains"])
        ys = _encoder([groups[k][0] for k in members], [groups[k][1] for k in members], wts, rot, col_bias)
        outs.update(zip(members, ys))
    return tuple(outs[k] for k in range(len(groups)))
```

```python
import functools
import math

import numpy as np
import jax
import jax.numpy as jnp
from jax import lax
from jax.experimental import pallas as pl
from jax.experimental.pallas import tpu as pltpu

F32 = jnp.float32
BF16 = jnp.bfloat16

D_MODEL = 2048
HEAD_DIM = 128
N_HEADS_A = 8
N_KV_A = 2
GQA_GROUP = N_HEADS_A // N_KV_A
N_HEADS_B = 8
WIDTH_A = N_HEADS_A * HEAD_DIM
KV_WIDTH_A = N_KV_A * HEAD_DIM
WIDTH_B = N_HEADS_B * HEAD_DIM
ROT_WIDTH = WIDTH_A + KV_WIDTH_A
QKV_A_WIDTH = ROT_WIDTH + KV_WIDTH_A
D_FF = 5632
D_PLE = 256
GRID_W = 64
WIN_ROWS = 8
WIN_COLS = 16
ROPE_THETA = 10000.0
EPS = 1e-6
ATTN_SCALE = HEAD_DIM ** -0.5
LOG2_E = math.log2(math.e)
MASKED = -1e30
GQA_SAFE_LOG2_SCORE = 48.0
BF16_NORM_MARGIN = 1.02

NBR_Q_ROWS = 4
NBR_KEY_ROWS = 12
NBR_Q = NBR_Q_ROWS * GRID_W
NBR_KEYS = NBR_KEY_ROWS * GRID_W
NBR_DR_PAD = (NBR_KEY_ROWS - NBR_Q_ROWS) - (WIN_ROWS - 1) + (NBR_Q_ROWS - 1)
NBR_DR_SLOTS = NBR_DR_PAD + (WIN_ROWS - 1) + (NBR_KEY_ROWS - 2) + 1

PROJ_TN = 1024

V7X_VMEM_BYTES = 64 * 1024 * 1024
MIB = 1024 * 1024


def _params(semantics, vmem_mib):
    assert vmem_mib * MIB < V7X_VMEM_BYTES
    return pltpu.CompilerParams(dimension_semantics=semantics, vmem_limit_bytes=vmem_mib * MIB)


def _resident(shape):
    zeros = (0,) * len(shape)
    return pl.BlockSpec(shape, lambda *_: zeros, pipeline_mode=pl.Buffered(1))


def _rms(x):
    return x * lax.rsqrt(jnp.mean(x * x, axis=-1, keepdims=True) + EPS)


FFN_RESIDUAL_SCALE = 0.5
FFN_TF = 512
ROW_CHUNK = 16
NORM_ROWS = 128
ROW_UNROLL_EXIT = 16

def _ffn_kernel(*refs, tile_starts):
    n_src = len(tile_starts) - 1
    x_srcs = refs[:n_src]
    g1_ref, wgu_ref, wd_ref, g2_ref, h_ref, n_ref, x_buf, xn_ref, x_sem = refs[n_src:]
    i = pl.program_id(0)
    j = pl.program_id(1)
    tm = x_buf.shape[0]

    def x_copy(tile, wait):
        for s, x_hbm in enumerate(x_srcs):
            @pl.when((tile >= tile_starts[s]) & (tile < tile_starts[s + 1]))
            def _(s=s, x_hbm=x_hbm):
                row = pl.multiple_of((tile - tile_starts[s]) * tm, tm)
                copy = pltpu.make_async_copy(x_hbm.at[pl.ds(row, tm), :], x_buf, x_sem)
                copy.wait() if wait else copy.start()

    cur = i % 2
    has_next = i + 1 < pl.num_programs(0)
    n_chunks = tm // NORM_ROWS

    def norm_rows(slot, row0, n_rows):
        for r in range(0, n_rows, ROW_CHUNK):
            rows = pl.ds(row0 + r, ROW_CHUNK)
            xn_ref[slot, rows, :] = (_rms(x_buf[rows, :]) * g1_ref[...]).astype(BF16)

    @pl.when((i == 0) & (j == 0))
    def _():
        x_copy(0, wait=False)
        x_copy(0, wait=True)

        def first(c, carry):
            norm_rows(0, pl.multiple_of(c * NORM_ROWS, NORM_ROWS), NORM_ROWS)
            return carry

        lax.fori_loop(0, n_chunks, first, 0)

    @pl.when((j == 1) & has_next)
    def _():
        x_copy(i + 1, wait=False)

    @pl.when((j == 2) & has_next)
    def _():
        x_copy(i + 1, wait=True)

    tf = wd_ref.shape[0]

    def step(first_step, prepare_next):
        gu = jnp.dot(xn_ref[cur], wgu_ref[...], preferred_element_type=F32)
        g, u = gu[:, :tf], gu[:, tf:]
        a = (g * jax.nn.sigmoid(g) * u).astype(BF16)
        down = jnp.dot(a, wd_ref[...], preferred_element_type=F32)
        h_ref[...] = (x_buf[...] if first_step else h_ref[...]) + down
        if prepare_next:
            chunk = jnp.minimum(j - 2, n_chunks - 1)
            norm_rows(1 - cur, pl.multiple_of(chunk * NORM_ROWS, NORM_ROWS), NORM_ROWS)

    pl.when(j == 0)(functools.partial(step, True, False))
    pl.when(j == 1)(functools.partial(step, False, False))
    pl.when(j >= 2)(functools.partial(step, False, True))

    @pl.when(j == pl.num_programs(1) - 1)
    def _():
        def last(r, carry):
            rows = pl.ds(pl.multiple_of(r * ROW_CHUNK, ROW_CHUNK), ROW_CHUNK)
            n_ref[rows, :] = (_rms(h_ref[rows, :]) * g2_ref[...]).astype(BF16)
            return carry

        lax.fori_loop(0, tm // ROW_CHUNK, last, 0, unroll=ROW_UNROLL_EXIT)


def _ffn(xs, g1, w_gate_up, slot, wd, g2, *, tm=1024):
    d = xs[0].shape[1]
    dff, tf = wd.shape[1], FFN_TF
    assert all(x.shape[0] % tm == 0 and x.shape[1] == d for x in xs)
    assert w_gate_up.shape[1:] == (dff // tf, d, 2 * tf)
    assert tm % NORM_ROWS == 0 and tm // NORM_ROWS <= dff // tf - 2
    tile_starts = tuple(int(n) for n in np.cumsum([0] + [x.shape[0] // tm for x in xs]))
    t = tile_starts[-1] * tm
    return pl.pallas_call(
        functools.partial(_ffn_kernel, tile_starts=tile_starts),
        out_shape=(jax.ShapeDtypeStruct((t, d), F32), jax.ShapeDtypeStruct((t, d), BF16)),
        grid=(t // tm, dff // tf),
        in_specs=[pl.BlockSpec(memory_space=pl.ANY)] * len(xs) + [
            pl.BlockSpec((1, d), lambda i, j: (0, 0)),
            pl.BlockSpec((None, None, d, 2 * tf), lambda i, j: (slot, j, 0, 0)),
            pl.BlockSpec((None, tf, d), lambda i, j: (slot, j, 0)),
            pl.BlockSpec((1, d), lambda i, j: (0, 0)),
        ],
        out_specs=(
            pl.BlockSpec((tm, d), lambda i, j: (i, 0)),
            pl.BlockSpec((tm, d), lambda i, j: (i, 0)),
        ),
        scratch_shapes=[pltpu.VMEM((tm, d), F32), pltpu.VMEM((2, tm, d), BF16), pltpu.SemaphoreType.DMA(())],
        compiler_params=_params(("arbitrary", "arbitrary"), 60),
        name="ffn",
    )(*xs, g1, w_gate_up, wd, g2)


def _permute_heads_kernel(w_ref, perm_ref, o_ref):
    o_ref[...] = jnp.dot(w_ref[...], perm_ref[...], preferred_element_type=F32).astype(BF16)


def _rotary_weight_layout(w_in):
    d = w_in.shape[0]
    half = np.concatenate([np.arange(0, HEAD_DIM, 2), np.arange(1, HEAD_DIM, 2)])
    perm = jnp.asarray(np.arange(HEAD_DIM)[:, None] == half[None, :], BF16)
    return pl.pallas_call(
        _permute_heads_kernel,
        out_shape=jax.ShapeDtypeStruct((d, ROT_WIDTH), BF16),
        grid=(ROT_WIDTH // HEAD_DIM,),
        in_specs=[pl.BlockSpec((d, HEAD_DIM), lambda h: (0, h)), _resident((HEAD_DIM, HEAD_DIM))],
        out_specs=pl.BlockSpec((d, HEAD_DIM), lambda h: (0, h)),
        compiler_params=_params(("parallel",), 16),
        name="rotary_weight_layout",
    )(w_in, perm)


def _qkv_a_kernel(u_ref, wr_ref, wv_ref, rot_ref, q_ref, k_ref, v_ref, z_buf, v_buf, *, n_tiles):
    i = pl.program_id(0)

    def project(slot):
        u = u_ref[...]
        z_buf[slot] = jnp.dot(u, wr_ref[...], preferred_element_type=F32)
        v_buf[slot] = jnp.dot(u, wv_ref[...], preferred_element_type=F32).astype(BF16)

    def finish(slot):
        for hh in range(N_HEADS_A + N_KV_A):
            zh = z_buf[slot, :, hh * HEAD_DIM:(hh + 1) * HEAD_DIM]
            tab = 0 if hh < N_HEADS_A else 2
            cos_g = rot_ref[:, tab * HEAD_DIM:(tab + 1) * HEAD_DIM]
            sin_g = rot_ref[:, (tab + 1) * HEAD_DIM:(tab + 2) * HEAD_DIM]
            r = lax.rsqrt(jnp.mean(zh * zh, axis=-1, keepdims=True) + EPS)
            out = ((zh * cos_g + pltpu.roll(zh, HEAD_DIM // 2, 1) * sin_g) * r).astype(BF16)
            if hh < N_HEADS_A:
                q_ref[:, hh * HEAD_DIM:(hh + 1) * HEAD_DIM] = out
            else:
                kk = hh - N_HEADS_A
                k_ref[:, kk * HEAD_DIM:(kk + 1) * HEAD_DIM] = out
        v_ref[...] = v_buf[slot]

    @pl.when(i == 0)
    def _():
        project(0)

    for parity in range(2):
        @pl.when((i > 0) & (i < n_tiles) & (i % 2 == parity))
        def _(parity=parity):
            project(parity)
            finish(1 - parity)

    @pl.when(i == n_tiles)
    def _():
        finish((n_tiles - 1) % 2)


def _qkv_a(u, w_rot, w_in, rot, seq, *, tm=512):
    t, d = u.shape
    assert t % tm == 0 and seq % tm == 0 and ROT_WIDTH % KV_WIDTH_A == 0
    n_tiles, seq_blocks = t // tm, seq // tm

    def lagged(i):
        return jnp.maximum(i - 1, 0)

    return pl.pallas_call(
        functools.partial(_qkv_a_kernel, n_tiles=n_tiles),
        out_shape=(
            jax.ShapeDtypeStruct((t, WIDTH_A), BF16),
            jax.ShapeDtypeStruct((t, KV_WIDTH_A), BF16),
            jax.ShapeDtypeStruct((t, KV_WIDTH_A), BF16),
        ),
        grid=(n_tiles + 1,),
        in_specs=[
            pl.BlockSpec((tm, d), lambda i: (jnp.minimum(i, n_tiles - 1), 0)),
            _resident((d, ROT_WIDTH)),
            pl.BlockSpec((d, KV_WIDTH_A), lambda i: (0, ROT_WIDTH // KV_WIDTH_A), pipeline_mode=pl.Buffered(1)),
            pl.BlockSpec((tm, 4 * HEAD_DIM), lambda i: (lagged(i) % seq_blocks, 0)),
        ],
        out_specs=(
            pl.BlockSpec((tm, WIDTH_A), lambda i: (lagged(i), 0)),
            pl.BlockSpec((tm, KV_WIDTH_A), lambda i: (lagged(i), 0)),
            pl.BlockSpec((tm, KV_WIDTH_A), lambda i: (lagged(i), 0)),
        ),
        scratch_shapes=[pltpu.VMEM((2, tm, ROT_WIDTH), F32), pltpu.VMEM((2, tm, KV_WIDTH_A), BF16)],
        compiler_params=_params(("arbitrary",), 40),
        name="qkv_a",
    )(u, w_rot, w_in, rot)


def _proj_kernel(u_ref, w_ref, scale_ref, z_ref):
    z = jnp.dot(u_ref[...], w_ref[...], preferred_element_type=F32)
    z_ref[...] = (z * scale_ref[...]).astype(BF16)


def _proj(u, w, col_scale, *, tm=2048):
    t, d = u.shape
    n_chunks, _, tn = w.shape
    assert t % tm == 0 and col_scale.shape == (1, n_chunks * tn)
    return pl.pallas_call(
        _proj_kernel,
        out_shape=jax.ShapeDtypeStruct((t, n_chunks * tn), BF16),
        grid=(t // tm, n_chunks),
        in_specs=[
            pl.BlockSpec((tm, d), lambda i, j: (i, 0)),
            pl.BlockSpec((None, d, tn), lambda i, j: (j, 0, 0)),
            pl.BlockSpec((1, tn), lambda i, j: (0, j)),
        ],
        out_specs=pl.BlockSpec((tm, tn), lambda i, j: (i, j)),
        compiler_params=_params(("parallel", "arbitrary"), 48),
        name="proj",
    )(u, w, col_scale)


def _softmax_pv(s, v):
    m = jnp.max(s, axis=-1, keepdims=True)
    p = jnp.exp2(s - m)
    l = jnp.sum(p, axis=-1, keepdims=True)
    return jnp.dot(p.astype(BF16), v, preferred_element_type=F32) / l


_NT = (((1,), (1,)), ((), ()))


def _gqa_kernel(q_ref, k_ref, v_ref, o_ref, *, bounded):
    k = k_ref[0]
    v = v_ref[0]

    def score(g):
        return lax.dot_general(q_ref[0, :, g * HEAD_DIM:(g + 1) * HEAD_DIM], k, _NT, preferred_element_type=F32)

    s_next = score(0)
    for g in range(GQA_GROUP):
        s = s_next
        if g + 1 < GQA_GROUP:
            s_next = score(g + 1)
        if bounded:
            p = jnp.exp2(s)
            l = jnp.sum(p, axis=-1, keepdims=True)
            o = jnp.dot(p.astype(BF16), v, preferred_element_type=F32) / l
        else:
            o = _softmax_pv(s, v)
        o_ref[0, :, g * HEAD_DIM:(g + 1) * HEAD_DIM] = o.astype(BF16)


def _gqa(q, k, v, *, bounded, tq=1024):
    b, s, _ = q.shape
    assert s % tq == 0
    gw = GQA_GROUP * HEAD_DIM
    return pl.pallas_call(
        functools.partial(_gqa_kernel, bounded=bounded),
        out_shape=jax.ShapeDtypeStruct((b, s, WIDTH_A), BF16),
        grid=(b, N_KV_A, s // tq),
        in_specs=[
            pl.BlockSpec((1, tq, gw), lambda bi, kh, qi: (bi, qi, kh)),
            pl.BlockSpec((1, s, HEAD_DIM), lambda bi, kh, qi: (bi, 0, kh)),
            pl.BlockSpec((1, s, HEAD_DIM), lambda bi, kh, qi: (bi, 0, kh)),
        ],
        out_specs=pl.BlockSpec((1, tq, gw), lambda bi, kh, qi: (bi, qi, kh)),
        compiler_params=_params(("parallel", "parallel", "arbitrary"), 40),
        name="gqa_bounded" if bounded else "gqa",
    )(q, k, v)


def _nbr_window_start(j, rows):
    return jnp.clip(NBR_Q_ROWS * j - WIN_ROWS // 2, 0, rows - NBR_KEY_ROWS)


def _nbr_block_plans(rows):
    plans = {}
    for j in range(rows // NBR_Q_ROWS):
        ws = int(np.clip(NBR_Q_ROWS * j - WIN_ROWS // 2, 0, rows - NBR_KEY_ROWS))
        lo = tuple(int(np.clip(NBR_Q_ROWS * j + qr - WIN_ROWS // 2, 0, rows - WIN_ROWS)) - ws
                   for qr in range(NBR_Q_ROWS))
        d0 = ws - NBR_Q_ROWS * j + (WIN_ROWS - 1) + NBR_DR_PAD
        assert min(lo) >= 0 and max(lo) + WIN_ROWS <= NBR_KEY_ROWS
        plans.setdefault((lo, d0), []).append(j)
    return plans


def _nbr_block(q_ref, k_ref, v_ref, cb_ref, o_ref, ws, lo, d0):
    pair_w = 2 * GRID_W
    pa = min(lo) // 2
    pb = (max(lo) + WIN_ROWS - 1) // 2 + 1
    n_keys = (pb - pa) * pair_w
    start = pl.multiple_of(ws * GRID_W + pa * pair_w, pair_w)
    lane = lax.broadcasted_iota(jnp.int32, (GRID_W, pair_w), 1)
    zeros = jnp.zeros((GRID_W, pair_w), BF16)
    heads = [slice(h * HEAD_DIM, (h + 1) * HEAD_DIM) for h in range(N_HEADS_B)]

    def score(cols):
        return lax.dot_general(q_ref[0, :, cols], k_ref[pl.ds(start, n_keys), cols], _NT,
                               preferred_element_type=F32)

    lookahead = len(set(lo)) > 1
    s_next = score(heads[0])
    for h, cols in enumerate(heads):
        s = s_next if lookahead or h == 0 else score(cols)
        p_rows, l_rows = [], []
        for qr in range(NBR_Q_ROWS):
            first, last = lo[qr], lo[qr] + WIN_ROWS - 1
            tiles = []
            for a in range(first // 2, last // 2 + 1):
                t = (s[qr * GRID_W:(qr + 1) * GRID_W, (a - pa) * pair_w:(a - pa + 1) * pair_w]
                     + cb_ref[h, d0 + 2 * a - qr])
                if 2 * a < first:
                    t = jnp.where(lane < GRID_W, MASKED, t)
                if 2 * a + 1 > last:
                    t = jnp.where(lane >= GRID_W, MASKED, t)
                tiles.append(t)
            m = jnp.max(functools.reduce(jnp.maximum, tiles), axis=-1, keepdims=True)
            ps = [jnp.exp2(t - m) for t in tiles]
            l_rows.append(jnp.sum(functools.reduce(jnp.add, ps), axis=-1, keepdims=True))
            p_rows.append(jnp.concatenate(
                [zeros] * (first // 2 - pa) + [p.astype(BF16) for p in ps] + [zeros] * (pb - 1 - last // 2),
                axis=1))
        p = jnp.concatenate(p_rows, axis=0)
        l = jnp.concatenate(l_rows, axis=0)
        if lookahead and h + 1 < N_HEADS_B:
            s_next = score(heads[h + 1])
        v = v_ref[pl.ds(start, n_keys), cols]
        o_ref[0, :, cols] = (jnp.dot(p, v, preferred_element_type=F32) / l).astype(BF16)


def _nbr_kernel(q_ref, z_hbm, cb_ref, o_ref, kv_buf, kv_sem, *, rows, kv_cols):
    b = pl.program_id(0)
    j = pl.program_id(1)
    slot = b % 2

    def kv_copies(seq, dst_slot):
        return [pltpu.make_async_copy(z_hbm.at[seq, :, pl.ds(col * WIDTH_B, WIDTH_B)],
                                      kv_buf.at[dst_slot, which], kv_sem.at[which])
                for which, col in enumerate(kv_cols)]

    @pl.when(j == 0)
    def _():
        @pl.when(b == 0)
        def _():
            for copy in kv_copies(0, 0):
                copy.start()

        for copy in kv_copies(b, slot):
            copy.wait()

        @pl.when(b + 1 < pl.num_programs(0))
        def _():
            for copy in kv_copies(b + 1, 1 - slot):
                copy.start()

    k_ref, v_ref = kv_buf.at[slot, 0], kv_buf.at[slot, 1]
    ws = _nbr_window_start(j, rows)
    for (lo, d0), members in _nbr_block_plans(rows).items():
        member = functools.reduce(jnp.logical_or, [j == m for m in members])
        pl.when(member)(functools.partial(_nbr_block, q_ref, k_ref, v_ref, cb_ref, o_ref, ws, lo, d0))


def _nbr(z, col_bias, q_col, k_col, v_col):
    b, s, _ = z.shape
    rows = s // GRID_W
    assert rows % NBR_Q_ROWS == 0 and rows >= NBR_KEY_ROWS
    return pl.pallas_call(
        functools.partial(_nbr_kernel, rows=rows, kv_cols=(k_col, v_col)),
        out_shape=jax.ShapeDtypeStruct((b, s, WIDTH_B), BF16),
        grid=(b, rows // NBR_Q_ROWS),
        in_specs=[
            pl.BlockSpec((1, NBR_Q, WIDTH_B), lambda bi, j: (bi, j, q_col)),
            pl.BlockSpec(memory_space=pl.ANY),
            _resident(col_bias.shape),
        ],
        out_specs=pl.BlockSpec((1, NBR_Q, WIDTH_B), lambda bi, j: (bi, j, 0)),
        scratch_shapes=[pltpu.VMEM((2, 2, s, WIDTH_B), BF16), pltpu.SemaphoreType.DMA((2,))],
        compiler_params=_params(("arbitrary", "arbitrary"), 48),
        name="nbr",
    )(z, z, col_bias)


def _nbr_col_bias(rpb):
    n_dr, n_dc = 2 * WIN_ROWS - 1, 2 * WIN_COLS - 1
    c = np.arange(GRID_W)[:, None]
    kc = np.arange(GRID_W)[None, :]
    cs = np.clip(c - WIN_COLS // 2, 0, GRID_W - WIN_COLS)
    col_valid = (kc >= cs) & (kc < cs + WIN_COLS)
    onehot = ((kc - c + (WIN_COLS - 1))[None] == np.arange(n_dc)[:, None, None]) & col_valid[None]
    onehot = jnp.asarray(onehot.reshape(n_dc, GRID_W * GRID_W), F32)
    tm = jnp.einsum("hrd,dn->hrn", rpb * LOG2_E, onehot, precision=lax.Precision.HIGHEST)
    tm = jnp.where(col_valid.reshape(-1), tm, MASKED).reshape(-1, n_dr, GRID_W, GRID_W)
    hi_pad = NBR_DR_SLOTS + 1 - NBR_DR_PAD - n_dr
    tm = jnp.pad(tm, ((0, 0), (NBR_DR_PAD, hi_pad), (0, 0), (0, 0)), constant_values=MASKED)
    return jnp.concatenate([tm[:, :-1], tm[:, 1:]], axis=-1)


MERGE_TN = 512


def _merge_kernel(oa_ref, ob_ref, ga_ref, gb_ref, h_ref, wa_ref, wb_ref, wo_ref, out_ref, m_ref):
    oa = oa_ref[...]
    ob = ob_ref[...]
    for c in range(D_MODEL // MERGE_TN):
        cols = slice(c * MERGE_TN, (c + 1) * MERGE_TN)
        a = jnp.dot(oa, wa_ref[:, cols], preferred_element_type=F32)
        b = jnp.dot(ob, wb_ref[:, cols], preferred_element_type=F32)
        merged = (jax.nn.sigmoid(ga_ref[:, cols].astype(F32)) * a
                  + jax.nn.sigmoid(gb_ref[:, cols].astype(F32)) * b)
        m_ref[:, cols] = merged.astype(BF16)
    out_ref[...] = h_ref[...] + jnp.dot(m_ref[...], wo_ref[...], preferred_element_type=F32)


def _stacked(shape, k):
    zeros = (0,) * len(shape)
    return pl.BlockSpec((None,) + shape, lambda *_: (k,) + zeros, pipeline_mode=pl.Buffered(1))


def _merge(oa, ob, z, ga_col, gb_col, h, w_branches, w_square, *, tm=512):
    t, d = h.shape
    assert t % tm == 0 and d == D_MODEL and WIDTH_A == WIDTH_B
    return pl.pallas_call(
        _merge_kernel,
        out_shape=jax.ShapeDtypeStruct((t, d), F32),
        grid=(t // tm,),
        in_specs=[
            pl.BlockSpec((tm, WIDTH_A), lambda i: (i, 0)),
            pl.BlockSpec((tm, WIDTH_B), lambda i: (i, 0)),
            pl.BlockSpec((tm, d), lambda i: (i, ga_col)),
            pl.BlockSpec((tm, d), lambda i: (i, gb_col)),
            pl.BlockSpec((tm, d), lambda i: (i, 0)),
            _stacked((WIDTH_A, d), 0),
            _stacked((WIDTH_B, d), 1),
            _stacked((d, d), 0),
        ],
        out_specs=pl.BlockSpec((tm, d), lambda i: (i, 0)),
        scratch_shapes=[pltpu.VMEM((tm, d), BF16)],
        compiler_params=_params(("parallel",), 56),
        name="merge",
    )(oa, ob, z, z, h, w_branches, w_branches, w_square)


def _ple_kernel(h_ref, n_ref, p_ref, wg_ref, wp_ref, gf_ref, y_ref):
    gate = jax.nn.sigmoid(jnp.dot(n_ref[...], wg_ref[...], preferred_element_type=F32))
    emb = jnp.dot(p_ref[...].astype(BF16), wp_ref[...], preferred_element_type=F32)
    y_ref[...] = _rms(h_ref[...] + gate * emb) * gf_ref[...]


def _ple(h, n, row0, p, w_square, wp, gf, *, tm=512):
    t, d = p.shape[0], h.shape[1]
    assert t % tm == 0 and row0 % tm == 0
    tile0 = row0 // tm
    return pl.pallas_call(
        _ple_kernel,
        out_shape=jax.ShapeDtypeStruct((t, d), F32),
        grid=(t // tm,),
        in_specs=[
            pl.BlockSpec((tm, d), lambda i: (i + tile0, 0)),
            pl.BlockSpec((tm, d), lambda i: (i + tile0, 0)),
            pl.BlockSpec((tm, D_PLE), lambda i: (i, 0)),
            _stacked((d, d), 1),
            _resident((D_PLE, d)),
            _resident((1, d)),
        ],
        out_specs=pl.BlockSpec((tm, d), lambda i: (i, 0)),
        compiler_params=_params(("parallel",), 48),
        name="ple",
    )(h, n, p, w_square, wp, gf)


CAST_BLOCK_BYTES = 4 * MIB


def _cast_kernel(w_ref, o_ref, *, scale):
    w = w_ref[...]
    o_ref[...] = (w if scale == 1.0 else w * scale).astype(BF16)


def _to_bf16(w, layer, scale=1.0):
    _, r, c = w.shape
    target = max(16, CAST_BLOCK_BYTES // (4 * c))
    rows = next(n for n in range(min(r, target) // 16 * 16, 0, -16) if r % n == 0)
    return pl.pallas_call(
        functools.partial(_cast_kernel, scale=scale),
        out_shape=jax.ShapeDtypeStruct((r, c), BF16),
        grid=(r // rows,),
        in_specs=[pl.BlockSpec((None, rows, c), lambda i: (layer, i, 0))],
        out_specs=pl.BlockSpec((rows, c), lambda i: (i, 0)),
        compiler_params=_params(("parallel",), 32),
        name="to_bf16",
    )(w)


def _stack_kernel(*refs, scale):
    o_ref = refs[-1]
    for slot, w_ref in enumerate(refs[:-1]):
        @pl.when(pl.program_id(0) == slot)
        def _(w_ref=w_ref):
            w = w_ref[...]
            o_ref[...] = (w if scale == 1.0 else w * scale).astype(BF16)


def _stack_bf16(ws, layer, scale=1.0):
    _, r, c = ws[0].shape
    assert all(w.shape == ws[0].shape for w in ws)
    target = max(16, CAST_BLOCK_BYTES // (4 * c))
    rows = next(n for n in range(min(r, target) // 16 * 16, 0, -16) if r % n == 0)
    n_blocks = r // rows

    def in_spec(slot):
        return pl.BlockSpec((None, rows, c), lambda s, i: (layer, jnp.where(s == slot, i, n_blocks - 1), 0))

    return pl.pallas_call(
        functools.partial(_stack_kernel, scale=scale),
        out_shape=jax.ShapeDtypeStruct((len(ws), r, c), BF16),
        grid=(len(ws), n_blocks),
        in_specs=[in_spec(slot) for slot in range(len(ws))],
        out_specs=pl.BlockSpec((None, rows, c), lambda s, i: (s, i, 0)),
        compiler_params=_params(("arbitrary", "arbitrary"), 32),
        name="stack_bf16",
    )(*ws)


def _pack_w_in_kernel(w_ref, qkv_ref, proj_ref, *, qkv_width, chunk_starts, tn):
    qkv_ref[...] = w_ref[:, :qkv_width].astype(BF16)
    for c, start in enumerate(chunk_starts):
        proj_ref[c] = w_ref[:, start:start + tn].astype(BF16)


def _pack_w_in(w_in, layer, qkv_width, chunk_starts, tn, *, rows=64):
    _, d, n = w_in.shape
    assert d % rows == 0 and all(s % HEAD_DIM == 0 and s + tn <= n for s in chunk_starts)
    return pl.pallas_call(
        functools.partial(_pack_w_in_kernel, qkv_width=qkv_width, chunk_starts=chunk_starts, tn=tn),
        out_shape=(jax.ShapeDtypeStruct((d, qkv_width), BF16),
                   jax.ShapeDtypeStruct((len(chunk_starts), d, tn), BF16)),
        grid=(d // rows,),
        in_specs=[pl.BlockSpec((None, rows, n), lambda i: (layer, i, 0))],
        out_specs=(pl.BlockSpec((rows, qkv_width), lambda i: (i, 0)),
                   pl.BlockSpec((len(chunk_starts), rows, tn), lambda i: (0, i, 0))),
        compiler_params=_params(("parallel",), 32),
        name="pack_w_in",
    )(w_in)


def _pack_gate_up_kernel(*refs, tf):
    o_ref = refs[-1]
    for slot in range(len(refs) // 2):
        @pl.when(pl.program_id(0) == slot)
        def _(g_ref=refs[2 * slot], u_ref=refs[2 * slot + 1]):
            for j in range(o_ref.shape[0]):
                o_ref[j, :, :tf] = g_ref[:, j * tf:(j + 1) * tf].astype(BF16)
                o_ref[j, :, tf:] = u_ref[:, j * tf:(j + 1) * tf].astype(BF16)


def _pack_gate_up(gate_up_pairs, layer, *, tf, rows=64):
    _, r, c = gate_up_pairs[0][0].shape
    assert r % rows == 0 and c % tf == 0
    n_chunks, n_blocks = c // tf, r // rows

    def in_spec(slot):
        return pl.BlockSpec((None, rows, c), lambda s, i: (layer, jnp.where(s == slot, i, n_blocks - 1), 0))

    weights = [w for pair in gate_up_pairs for w in pair]
    assert all(w.shape == weights[0].shape for w in weights)
    return pl.pallas_call(
        functools.partial(_pack_gate_up_kernel, tf=tf),
        out_shape=jax.ShapeDtypeStruct((len(gate_up_pairs), n_chunks, r, 2 * tf), BF16),
        grid=(len(gate_up_pairs), n_blocks),
        in_specs=[in_spec(slot) for slot in range(len(gate_up_pairs)) for _ in range(2)],
        out_specs=pl.BlockSpec((None, n_chunks, rows, 2 * tf), lambda s, i: (s, 0, i, 0)),
        compiler_params=_params(("arbitrary", "arbitrary"), 32),
        name="pack_gate_up",
    )(*weights)


def _rotary_tables(seq, gain_q, gain_k):
    n_freq = HEAD_DIM // 4
    inv_freq = ROPE_THETA ** (-jnp.arange(n_freq, dtype=F32) / n_freq)
    t = jnp.arange(seq)
    row = (t // GRID_W).astype(F32)
    col = (t % GRID_W).astype(F32)
    ang = jnp.concatenate([row[:, None] * inv_freq[None], col[:, None] * inv_freq[None]], axis=-1)
    cos = jnp.concatenate([jnp.cos(ang), jnp.cos(ang)], axis=-1)
    sin = jnp.concatenate([-jnp.sin(ang), jnp.sin(ang)], axis=-1)
    return jnp.concatenate([cos * g if part == 0 else sin * jnp.roll(g, HEAD_DIM // 2)
                            for g in (gain_q, gain_k) for part in (0, 1)], axis=-1)


_W_IN_SEGMENTS = dict(q_b=QKV_A_WIDTH, k_b=QKV_A_WIDTH + WIDTH_B, v_b=QKV_A_WIDTH + 2 * WIDTH_B,
                      g_a=QKV_A_WIDTH + 3 * WIDTH_B, g_b=QKV_A_WIDTH + 3 * WIDTH_B + D_MODEL)
_PROJ_LAYOUT = (("g_a", D_MODEL), ("g_b", D_MODEL), ("q_b", WIDTH_B), ("k_b", WIDTH_B), ("v_b", WIDTH_B))
_PROJ_CHUNK_STARTS = tuple(_W_IN_SEGMENTS[name] + off
                           for name, width in _PROJ_LAYOUT for off in range(0, width, PROJ_TN))
assert all(width % PROJ_TN == 0 for _, width in _PROJ_LAYOUT)
_GA_COL, _GB_COL = 0, 1
_QB_COL, _KB_COL, _VB_COL = 4, 5, 6
assert 2 * D_MODEL == _QB_COL * WIDTH_B


def _prepare_weights(ffn1_norm, ffn1_w_gate, ffn1_w_up, ffn1_w_down, mix_norm, w_in, q_norm, k_norm,
                     w_branch_a, w_branch_b, w_out, ffn2_norm, ffn2_w_gate, ffn2_w_up, ffn2_w_down,
                     ple_norm, w_ple_gate, w_ple_proj, final_norm, layer):
    i = layer
    half = np.concatenate([np.arange(0, HEAD_DIM, 2), np.arange(1, HEAD_DIM, 2)])
    w_qkv_a, w_proj = _pack_w_in(w_in, i, QKV_A_WIDTH, _PROJ_CHUNK_STARTS, PROJ_TN)
    w_rot = _rotary_weight_layout(w_qkv_a)
    w_gate_up = _pack_gate_up([(ffn1_w_gate, ffn1_w_up), (ffn2_w_gate, ffn2_w_up)], i, tf=FFN_TF)
    w_down = _stack_bf16([ffn1_w_down, ffn2_w_down], i, FFN_RESIDUAL_SCALE)
    w_square = _stack_bf16([w_out, w_ple_gate], i)
    q_scale = ATTN_SCALE * LOG2_E
    proj_scale = jnp.concatenate([jnp.full((width,), q_scale if name == "q_b" else 1.0, F32)
                                  for name, width in _PROJ_LAYOUT])[None]
    rot_gains = ((q_norm[i] * q_scale)[half], k_norm[i][half])
    gqa_score_bound = (HEAD_DIM * q_scale * BF16_NORM_MARGIN
                       * jnp.max(jnp.abs(q_norm[i])) * jnp.max(jnp.abs(k_norm[i])))
    return dict(
        ffn1=(ffn1_norm[i][None], w_gate_up, 0, w_down, mix_norm[i][None]),
        w_rot=w_rot, rot_gains=rot_gains, w_qkv_a=w_qkv_a, w_proj=w_proj, proj_scale=proj_scale, gqa_score_bound=gqa_score_bound,
        merge=(_stack_bf16([w_branch_a, w_branch_b], i), w_square),
        ffn2=(ffn2_norm[i][None], w_gate_up, 1, w_down, ple_norm[i][None]),
        ple=(w_square, _to_bf16(w_ple_proj, i), final_norm[None]),
    )


def _encoder(xs, ps, wts, rot, col_bias):
    s, d = xs[0].shape[1:]
    assert all(x.shape[1:] == (s, d) for x in xs)
    b = sum(x.shape[0] for x in xs)
    t = b * s
    h1, u = _ffn([x.reshape(-1, d) for x in xs], *wts["ffn1"])
    q_a, k_a, v_a = _qkv_a(u, wts["w_rot"], wts["w_qkv_a"], rot, s)
    z = _proj(u, wts["w_proj"], wts["proj_scale"])
    qkv = (q_a.reshape(b, s, -1), k_a.reshape(b, s, -1), v_a.reshape(b, s, -1))
    o_a = lax.cond(wts["gqa_score_bound"] <= GQA_SAFE_LOG2_SCORE,
                   functools.partial(_gqa, bounded=True), functools.partial(_gqa, bounded=False), *qkv)
    o_b = _nbr(z.reshape(b, s, -1), col_bias, _QB_COL, _KB_COL, _VB_COL)
    h2 = _merge(o_a.reshape(t, -1), o_b.reshape(t, -1), z, _GA_COL, _GB_COL, h1, *wts["merge"])
    h3, n = _ffn([h2], *wts["ffn2"])
    row_starts = np.cumsum([0] + [x.shape[0] * s for x in xs])
    return tuple(_ple(h3, n, int(row0), p.reshape(-1, p.shape[-1]), *wts["ple"]).reshape(x.shape)
                 for row0, p, x in zip(row_starts, ps, xs))


def kernel(x_prompt, x_sample, p_prompt, p_sample, ffn1_norm, ffn1_w_gate, ffn1_w_up, ffn1_w_down, mix_norm, w_in, q_norm, k_norm, nat_rpb, w_branch_a, w_branch_b, w_out, ffn2_norm, ffn2_w_gate, ffn2_w_up, ffn2_w_down, ple_norm, w_ple_gate, w_ple_proj, final_norm):
    assert ffn1_norm.shape[0] == 1, "single-layer encoder"
    wts = _prepare_weights(ffn1_norm, ffn1_w_gate, ffn1_w_up, ffn1_w_down, mix_norm, w_in, q_norm, k_norm,
                           w_branch_a, w_branch_b, w_out, ffn2_norm, ffn2_w_gate, ffn2_w_up, ffn2_w_down,
                           ple_norm, w_ple_gate, w_ple_proj, final_norm, 0)
    col_bias = _nbr_col_bias(nat_rpb[0])
    groups = ((x_prompt, p_prompt[0]), (x_sample, p_sample[0]))
    outs = {}
    for seq in sorted({x.shape[1] for x, _ in groups}):
        members = [k for k, (x, _) in enumerate(groups) if x.shape[1] == seq]
        rot = _rotary_tables(seq, *wts["rot_gains"])
        ys = _encoder([groups[k][0] for k in members], [groups[k][1] for k in members], wts, rot, col_bias)
        outs.update(zip(members, ys))
    return tuple(outs[k] for k in range(len(groups)))
```

```python
import functools
import math

import numpy as np
import jax
import jax.numpy as jnp
from jax import lax
from jax.experimental import pallas as pl
from jax.experimental.pallas import tpu as pltpu

F32 = jnp.float32
BF16 = jnp.bfloat16

D_MODEL = 2048
HEAD_DIM = 128
N_HEADS_A = 8
N_KV_A = 2
GQA_GROUP = N_HEADS_A // N_KV_A
N_HEADS_B = 8
WIDTH_A = N_HEADS_A * HEAD_DIM
KV_WIDTH_A = N_KV_A * HEAD_DIM
WIDTH_B = N_HEADS_B * HEAD_DIM
ROT_WIDTH = WIDTH_A + KV_WIDTH_A
QKV_A_WIDTH = ROT_WIDTH + KV_WIDTH_A
D_FF = 5632
D_PLE = 256
GRID_W = 64
WIN_ROWS = 8
WIN_COLS = 16
ROPE_THETA = 10000.0
EPS = 1e-6
ATTN_SCALE = HEAD_DIM ** -0.5
LOG2_E = math.log2(math.e)
MASKED = -1e30
GQA_SAFE_LOG2_SCORE = 48.0
BF16_NORM_MARGIN = 1.02

NBR_Q_ROWS = 4
NBR_KEY_ROWS = 12
NBR_Q = NBR_Q_ROWS * GRID_W
NBR_LOOKAHEAD = 2
NBR_DR_PAD = (NBR_KEY_ROWS - NBR_Q_ROWS) - (WIN_ROWS - 1) + (NBR_Q_ROWS - 1)
NBR_DR_SLOTS = NBR_DR_PAD + (WIN_ROWS - 1) + (NBR_KEY_ROWS - 2) + 1

PROJ_TN = 1024

V7X_VMEM_BYTES = 64 * 1024 * 1024
MIB = 1024 * 1024


def _params(semantics, vmem_mib):
    assert vmem_mib * MIB < V7X_VMEM_BYTES
    return pltpu.CompilerParams(dimension_semantics=semantics, vmem_limit_bytes=vmem_mib * MIB)


def _resident(shape):
    zeros = (0,) * len(shape)
    return pl.BlockSpec(shape, lambda *_: zeros, pipeline_mode=pl.Buffered(1))


def _rms(x):
    return x * lax.rsqrt(jnp.mean(x * x, axis=-1, keepdims=True) + EPS)


FFN_RESIDUAL_SCALE = 0.5
FFN_TF = 512
ROW_CHUNK = 16
NORM_ROWS = 128
ROW_UNROLL_EXIT = 16

def _ffn_kernel(*refs, tile_starts):
    n_src = len(tile_starts) - 1
    x_srcs = refs[:n_src]
    g1_ref, wgu_ref, wd_ref, g2_ref, h_ref, n_ref, x_buf, xn_ref, x_sem = refs[n_src:]
    i = pl.program_id(0)
    j = pl.program_id(1)
    tm = x_buf.shape[0]

    def x_copy(tile, wait):
        for s, x_hbm in enumerate(x_srcs):
            @pl.when((tile >= tile_starts[s]) & (tile < tile_starts[s + 1]))
            def _(s=s, x_hbm=x_hbm):
                row = pl.multiple_of((tile - tile_starts[s]) * tm, tm)
                copy = pltpu.make_async_copy(x_hbm.at[pl.ds(row, tm), :], x_buf, x_sem)
                copy.wait() if wait else copy.start()

    cur = i % 2
    has_next = i + 1 < pl.num_programs(0)
    n_chunks = tm // NORM_ROWS

    def norm_rows(slot, row0, n_rows):
        for r in range(0, n_rows, ROW_CHUNK):
            rows = pl.ds(row0 + r, ROW_CHUNK)
            xn_ref[slot, rows, :] = (_rms(x_buf[rows, :]) * g1_ref[...]).astype(BF16)

    @pl.when((i == 0) & (j == 0))
    def _():
        x_copy(0, wait=False)
        x_copy(0, wait=True)

        def first(c, carry):
            norm_rows(0, pl.multiple_of(c * NORM_ROWS, NORM_ROWS), NORM_ROWS)
            return carry

        lax.fori_loop(0, n_chunks, first, 0)

    @pl.when((j == 1) & has_next)
    def _():
        x_copy(i + 1, wait=False)

    @pl.when((j == 2) & has_next)
    def _():
        x_copy(i + 1, wait=True)

    tf = wd_ref.shape[0]

    def step(first_step, prepare_next):
        gu = jnp.dot(xn_ref[cur], wgu_ref[...], preferred_element_type=F32)
        g, u = gu[:, :tf], gu[:, tf:]
        a = (g * jax.nn.sigmoid(g) * u).astype(BF16)
        down = jnp.dot(a, wd_ref[...], preferred_element_type=F32)
        h_ref[...] = (x_buf[...] if first_step else h_ref[...]) + down
        if prepare_next:
            chunk = jnp.minimum(j - 2, n_chunks - 1)
            norm_rows(1 - cur, pl.multiple_of(chunk * NORM_ROWS, NORM_ROWS), NORM_ROWS)

    pl.when(j == 0)(functools.partial(step, True, False))
    pl.when(j == 1)(functools.partial(step, False, False))
    pl.when(j >= 2)(functools.partial(step, False, True))

    @pl.when(j == pl.num_programs(1) - 1)
    def _():
        def last(r, carry):
            rows = pl.ds(pl.multiple_of(r * ROW_CHUNK, ROW_CHUNK), ROW_CHUNK)
            n_ref[rows, :] = (_rms(h_ref[rows, :]) * g2_ref[...]).astype(BF16)
            return carry

        lax.fori_loop(0, tm // ROW_CHUNK, last, 0, unroll=ROW_UNROLL_EXIT)


def _ffn(xs, g1, w_gate_up, slot, wd, g2, *, tm=1024):
    d = xs[0].shape[1]
    dff, tf = wd.shape[0], FFN_TF
    assert all(x.shape[0] % tm == 0 and x.shape[1] == d for x in xs)
    assert w_gate_up.shape[1:] == (dff // tf, d, 2 * tf)
    assert tm % NORM_ROWS == 0 and tm // NORM_ROWS <= dff // tf - 2
    tile_starts = tuple(int(n) for n in np.cumsum([0] + [x.shape[0] // tm for x in xs]))
    t = tile_starts[-1] * tm
    return pl.pallas_call(
        functools.partial(_ffn_kernel, tile_starts=tile_starts),
        out_shape=(jax.ShapeDtypeStruct((t, d), F32), jax.ShapeDtypeStruct((t, d), BF16)),
        grid=(t // tm, dff // tf),
        in_specs=[pl.BlockSpec(memory_space=pl.ANY)] * len(xs) + [
            pl.BlockSpec((1, d), lambda i, j: (0, 0)),
            pl.BlockSpec((None, None, d, 2 * tf), lambda i, j: (slot, j, 0, 0)),
            pl.BlockSpec((tf, d), lambda i, j: (j, 0)),
            pl.BlockSpec((1, d), lambda i, j: (0, 0)),
        ],
        out_specs=(
            pl.BlockSpec((tm, d), lambda i, j: (i, 0)),
            pl.BlockSpec((tm, d), lambda i, j: (i, 0)),
        ),
        scratch_shapes=[pltpu.VMEM((tm, d), F32), pltpu.VMEM((2, tm, d), BF16), pltpu.SemaphoreType.DMA(())],
        compiler_params=_params(("arbitrary", "arbitrary"), 60),
        name="ffn",
    )(*xs, g1, w_gate_up, wd, g2)


def _permute_heads_kernel(w_ref, perm_ref, o_ref):
    o_ref[...] = jnp.dot(w_ref[...], perm_ref[...], preferred_element_type=F32).astype(BF16)


def _rotary_weight_layout(w_in):
    d = w_in.shape[0]
    half = np.concatenate([np.arange(0, HEAD_DIM, 2), np.arange(1, HEAD_DIM, 2)])
    perm = jnp.asarray(np.arange(HEAD_DIM)[:, None] == half[None, :], BF16)
    return pl.pallas_call(
        _permute_heads_kernel,
        out_shape=jax.ShapeDtypeStruct((d, ROT_WIDTH), BF16),
        grid=(ROT_WIDTH // HEAD_DIM,),
        in_specs=[pl.BlockSpec((d, HEAD_DIM), lambda h: (0, h)), _resident((HEAD_DIM, HEAD_DIM))],
        out_specs=pl.BlockSpec((d, HEAD_DIM), lambda h: (0, h)),
        compiler_params=_params(("parallel",), 16),
        name="rotary_weight_layout",
    )(w_in, perm)


def _qkv_a_kernel(u_ref, wr_ref, wv_ref, rot_ref, q_ref, k_ref, v_ref, z_buf, v_buf, *, n_tiles):
    i = pl.program_id(0)

    def project(slot):
        u = u_ref[...]
        z_buf[slot] = jnp.dot(u, wr_ref[...], preferred_element_type=F32)
        v_buf[slot] = jnp.dot(u, wv_ref[...], preferred_element_type=F32).astype(BF16)

    def finish(slot):
        for hh in range(N_HEADS_A + N_KV_A):
            zh = z_buf[slot, :, hh * HEAD_DIM:(hh + 1) * HEAD_DIM]
            tab = 0 if hh < N_HEADS_A else 2
            cos_g = rot_ref[:, tab * HEAD_DIM:(tab + 1) * HEAD_DIM]
            sin_g = rot_ref[:, (tab + 1) * HEAD_DIM:(tab + 2) * HEAD_DIM]
            r = lax.rsqrt(jnp.mean(zh * zh, axis=-1, keepdims=True) + EPS)
            out = ((zh * cos_g + pltpu.roll(zh, HEAD_DIM // 2, 1) * sin_g) * r).astype(BF16)
            if hh < N_HEADS_A:
                q_ref[:, hh * HEAD_DIM:(hh + 1) * HEAD_DIM] = out
            else:
                kk = hh - N_HEADS_A
                k_ref[:, kk * HEAD_DIM:(kk + 1) * HEAD_DIM] = out
        v_ref[...] = v_buf[slot]

    @pl.when(i == 0)
    def _():
        project(0)

    for parity in range(2):
        @pl.when((i > 0) & (i < n_tiles) & (i % 2 == parity))
        def _(parity=parity):
            project(parity)
            finish(1 - parity)

    @pl.when(i == n_tiles)
    def _():
        finish((n_tiles - 1) % 2)


def _qkv_a(u, w_rot, w_in, rot, seq, *, tm=512):
    t, d = u.shape
    assert t % tm == 0 and seq % tm == 0 and ROT_WIDTH % KV_WIDTH_A == 0
    n_tiles, seq_blocks = t // tm, seq // tm

    def lagged(i):
        return jnp.maximum(i - 1, 0)

    return pl.pallas_call(
        functools.partial(_qkv_a_kernel, n_tiles=n_tiles),
        out_shape=(
            jax.ShapeDtypeStruct((t, WIDTH_A), BF16),
            jax.ShapeDtypeStruct((t, KV_WIDTH_A), BF16),
            jax.ShapeDtypeStruct((t, KV_WIDTH_A), BF16),
        ),
        grid=(n_tiles + 1,),
        in_specs=[
            pl.BlockSpec((tm, d), lambda i: (jnp.minimum(i, n_tiles - 1), 0)),
            _resident((d, ROT_WIDTH)),
            pl.BlockSpec((d, KV_WIDTH_A), lambda i: (0, ROT_WIDTH // KV_WIDTH_A), pipeline_mode=pl.Buffered(1)),
            pl.BlockSpec((tm, 4 * HEAD_DIM), lambda i: (lagged(i) % seq_blocks, 0)),
        ],
        out_specs=(
            pl.BlockSpec((tm, WIDTH_A), lambda i: (lagged(i), 0)),
            pl.BlockSpec((tm, KV_WIDTH_A), lambda i: (lagged(i), 0)),
            pl.BlockSpec((tm, KV_WIDTH_A), lambda i: (lagged(i), 0)),
        ),
        scratch_shapes=[pltpu.VMEM((2, tm, ROT_WIDTH), F32), pltpu.VMEM((2, tm, KV_WIDTH_A), BF16)],
        compiler_params=_params(("arbitrary",), 40),
        name="qkv_a",
    )(u, w_rot, w_in, rot)


def _proj_kernel(u_ref, w_ref, scale_ref, z_ref):
    z = jnp.dot(u_ref[...], w_ref[...], preferred_element_type=F32)
    z_ref[...] = (z * scale_ref[...]).astype(BF16)


def _proj(u, w, col_scale, *, tm=2048):
    t, d = u.shape
    n_chunks, _, tn = w.shape
    assert t % tm == 0 and col_scale.shape == (1, n_chunks * tn)
    return pl.pallas_call(
        _proj_kernel,
        out_shape=jax.ShapeDtypeStruct((t, n_chunks * tn), BF16),
        grid=(t // tm, n_chunks),
        in_specs=[
            pl.BlockSpec((tm, d), lambda i, j: (i, 0)),
            pl.BlockSpec((None, d, tn), lambda i, j: (j, 0, 0)),
            pl.BlockSpec((1, tn), lambda i, j: (0, j)),
        ],
        out_specs=pl.BlockSpec((tm, tn), lambda i, j: (i, j)),
        compiler_params=_params(("parallel", "arbitrary"), 48),
        name="proj",
    )(u, w, col_scale)


def _softmax_pv(s, v):
    m = jnp.max(s, axis=-1, keepdims=True)
    p = jnp.exp2(s - m)
    l = jnp.sum(p, axis=-1, keepdims=True)
    return jnp.dot(p.astype(BF16), v, preferred_element_type=F32) / l


_NT = (((1,), (1,)), ((), ()))


def _gqa_kernel(q_ref, k_ref, v_ref, o_ref, *, bounded):
    k = k_ref[0]
    v = v_ref[0]

    def score(g):
        return lax.dot_general(q_ref[0, :, g * HEAD_DIM:(g + 1) * HEAD_DIM], k, _NT, preferred_element_type=F32)

    s_next = score(0)
    for g in range(GQA_GROUP):
        s = s_next
        if g + 1 < GQA_GROUP:
            s_next = score(g + 1)
        if bounded:
            p = jnp.exp2(s)
            l = jnp.sum(p, axis=-1, keepdims=True)
            o = jnp.dot(p.astype(BF16), v, preferred_element_type=F32) / l
        else:
            o = _softmax_pv(s, v)
        o_ref[0, :, g * HEAD_DIM:(g + 1) * HEAD_DIM] = o.astype(BF16)


def _gqa(q, k, v, *, bounded, tq=1024):
    b, s, _ = q.shape
    assert s % tq == 0
    gw = GQA_GROUP * HEAD_DIM
    return pl.pallas_call(
        functools.partial(_gqa_kernel, bounded=bounded),
        out_shape=jax.ShapeDtypeStruct((b, s, WIDTH_A), BF16),
        grid=(b, N_KV_A, s // tq),
        in_specs=[
            pl.BlockSpec((1, tq, gw), lambda bi, kh, qi: (bi, qi, kh)),
            pl.BlockSpec((1, s, HEAD_DIM), lambda bi, kh, qi: (bi, 0, kh)),
            pl.BlockSpec((1, s, HEAD_DIM), lambda bi, kh, qi: (bi, 0, kh)),
        ],
        out_specs=pl.BlockSpec((1, tq, gw), lambda bi, kh, qi: (bi, qi, kh)),
        compiler_params=_params(("parallel", "parallel", "arbitrary"), 40),
        name="gqa_bounded" if bounded else "gqa",
    )(q, k, v)


def _nbr_window_start(j, rows):
    return jnp.clip(NBR_Q_ROWS * j - WIN_ROWS // 2, 0, rows - NBR_KEY_ROWS)


def _nbr_block_plans(rows):
    plans = {}
    for j in range(rows // NBR_Q_ROWS):
        ws = int(np.clip(NBR_Q_ROWS * j - WIN_ROWS // 2, 0, rows - NBR_KEY_ROWS))
        lo = tuple(int(np.clip(NBR_Q_ROWS * j + qr - WIN_ROWS // 2, 0, rows - WIN_ROWS)) - ws
                   for qr in range(NBR_Q_ROWS))
        d0 = ws - NBR_Q_ROWS * j + (WIN_ROWS - 1) + NBR_DR_PAD
        assert min(lo) >= 0 and max(lo) + WIN_ROWS <= NBR_KEY_ROWS
        plans.setdefault((lo, d0), []).append(j)
    return plans


def _nbr_block(q_ref, k_ref, v_ref, cb_ref, o_ref, ws, lo, d0):
    pair_w = 2 * GRID_W
    pa = min(lo) // 2
    pb = (max(lo) + WIN_ROWS - 1) // 2 + 1
    n_keys = (pb - pa) * pair_w
    start = pl.multiple_of(ws * GRID_W + pa * pair_w, pair_w)
    lane = lax.broadcasted_iota(jnp.int32, (GRID_W, pair_w), 1)
    zeros = jnp.zeros((GRID_W, pair_w), BF16)
    heads = [slice(h * HEAD_DIM, (h + 1) * HEAD_DIM) for h in range(N_HEADS_B)]

    def score(cols):
        return lax.dot_general(q_ref[0, :, cols], k_ref[pl.ds(start, n_keys), cols], _NT,
                               preferred_element_type=F32)

    depth = NBR_LOOKAHEAD if len(set(lo)) > 1 else 0
    queue = [score(cols) for cols in heads[:depth]]
    for h, cols in enumerate(heads):
        s = queue.pop(0) if queue else score(cols)
        p_rows, l_rows = [], []
        for qr in range(NBR_Q_ROWS):
            first, last = lo[qr], lo[qr] + WIN_ROWS - 1
            tiles = []
            for a in range(first // 2, last // 2 + 1):
                t = (s[qr * GRID_W:(qr + 1) * GRID_W, (a - pa) * pair_w:(a - pa + 1) * pair_w]
                     + cb_ref[h, d0 + 2 * a - qr])
                if 2 * a < first:
                    t = jnp.where(lane < GRID_W, MASKED, t)
                if 2 * a + 1 > last:
                    t = jnp.where(lane >= GRID_W, MASKED, t)
                tiles.append(t)
            m = jnp.max(functools.reduce(jnp.maximum, tiles), axis=-1, keepdims=True)
            ps = [jnp.exp2(t - m) for t in tiles]
            l_rows.append(jnp.sum(functools.reduce(jnp.add, ps), axis=-1, keepdims=True))
            p_rows.append(jnp.concatenate(
                [zeros] * (first // 2 - pa) + [p.astype(BF16) for p in ps] + [zeros] * (pb - 1 - last // 2),
                axis=1))
        p = jnp.concatenate(p_rows, axis=0)
        l = jnp.concatenate(l_rows, axis=0)
        if depth and h + depth < N_HEADS_B:
            queue.append(score(heads[h + depth]))
        v = v_ref[pl.ds(start, n_keys), cols]
        o_ref[0, :, cols] = (jnp.dot(p, v, preferred_element_type=F32) / l).astype(BF16)


def _nbr_kernel(q_ref, z_hbm, cb_ref, o_ref, kv_buf, kv_sem, *, rows, kv_cols):
    b = pl.program_id(0)
    j = pl.program_id(1)
    slot = b % 2

    def kv_copies(seq, dst_slot):
        return [pltpu.make_async_copy(z_hbm.at[seq, :, pl.ds(col * WIDTH_B, WIDTH_B)],
                                      kv_buf.at[dst_slot, which], kv_sem.at[which])
                for which, col in enumerate(kv_cols)]

    @pl.when(j == 0)
    def _():
        @pl.when(b == 0)
        def _():
            for copy in kv_copies(0, 0):
                copy.start()

        for copy in kv_copies(b, slot):
            copy.wait()

        @pl.when(b + 1 < pl.num_programs(0))
        def _():
            for copy in kv_copies(b + 1, 1 - slot):
                copy.start()

    k_ref, v_ref = kv_buf.at[slot, 0], kv_buf.at[slot, 1]
    ws = _nbr_window_start(j, rows)
    for (lo, d0), members in _nbr_block_plans(rows).items():
        member = functools.reduce(jnp.logical_or, [j == m for m in members])
        pl.when(member)(functools.partial(_nbr_block, q_ref, k_ref, v_ref, cb_ref, o_ref, ws, lo, d0))


def _nbr(z, col_bias, q_col, k_col, v_col):
    b, s, _ = z.shape
    rows = s // GRID_W
    assert rows % NBR_Q_ROWS == 0 and rows >= NBR_KEY_ROWS
    return pl.pallas_call(
        functools.partial(_nbr_kernel, rows=rows, kv_cols=(k_col, v_col)),
        out_shape=jax.ShapeDtypeStruct((b, s, WIDTH_B), BF16),
        grid=(b, rows // NBR_Q_ROWS),
        in_specs=[
            pl.BlockSpec((1, NBR_Q, WIDTH_B), lambda bi, j: (bi, j, q_col)),
            pl.BlockSpec(memory_space=pl.ANY),
            _resident(col_bias.shape),
        ],
        out_specs=pl.BlockSpec((1, NBR_Q, WIDTH_B), lambda bi, j: (bi, j, 0)),
        scratch_shapes=[pltpu.VMEM((2, 2, s, WIDTH_B), BF16), pltpu.SemaphoreType.DMA((2,))],
        compiler_params=_params(("arbitrary", "arbitrary"), 48),
        name="nbr",
    )(z, z, col_bias)


def _nbr_col_bias(rpb):
    n_dr, n_dc = 2 * WIN_ROWS - 1, 2 * WIN_COLS - 1
    c = np.arange(GRID_W)[:, None]
    kc = np.arange(GRID_W)[None, :]
    cs = np.clip(c - WIN_COLS // 2, 0, GRID_W - WIN_COLS)
    col_valid = (kc >= cs) & (kc < cs + WIN_COLS)
    onehot = ((kc - c + (WIN_COLS - 1))[None] == np.arange(n_dc)[:, None, None]) & col_valid[None]
    onehot = jnp.asarray(onehot.reshape(n_dc, GRID_W * GRID_W), F32)
    tm = jnp.einsum("hrd,dn->hrn", rpb * LOG2_E, onehot, precision=lax.Precision.HIGHEST)
    tm = jnp.where(col_valid.reshape(-1), tm, MASKED).reshape(-1, n_dr, GRID_W, GRID_W)
    hi_pad = NBR_DR_SLOTS + 1 - NBR_DR_PAD - n_dr
    tm = jnp.pad(tm, ((0, 0), (NBR_DR_PAD, hi_pad), (0, 0), (0, 0)), constant_values=MASKED)
    return jnp.concatenate([tm[:, :-1], tm[:, 1:]], axis=-1)


MERGE_TN = 256


def _merge_kernel(oa_ref, ob_ref, ga_ref, gb_ref, h_ref, wa_ref, wb_ref, wo_ref, out_ref, m_ref):
    oa = oa_ref[...]
    ob = ob_ref[...]
    for c in range(D_MODEL // MERGE_TN):
        cols = slice(c * MERGE_TN, (c + 1) * MERGE_TN)
        a = jnp.dot(oa, wa_ref[:, cols], preferred_element_type=F32)
        b = jnp.dot(ob, wb_ref[:, cols], preferred_element_type=F32)
        merged = (jax.nn.sigmoid(ga_ref[:, cols].astype(F32)) * a
                  + jax.nn.sigmoid(gb_ref[:, cols].astype(F32)) * b)
        m_ref[:, cols] = merged.astype(BF16)
    out_ref[...] = h_ref[...] + jnp.dot(m_ref[...], wo_ref[...], preferred_element_type=F32)


def _merge(oa, ob, z, ga_col, gb_col, h, wa, wb, wo, *, tm=512):
    t, d = h.shape
    assert t % tm == 0 and d == D_MODEL
    return pl.pallas_call(
        _merge_kernel,
        out_shape=jax.ShapeDtypeStruct((t, d), F32),
        grid=(t // tm,),
        in_specs=[
            pl.BlockSpec((tm, WIDTH_A), lambda i: (i, 0)),
            pl.BlockSpec((tm, WIDTH_B), lambda i: (i, 0)),
            pl.BlockSpec((tm, d), lambda i: (i, ga_col)),
            pl.BlockSpec((tm, d), lambda i: (i, gb_col)),
            pl.BlockSpec((tm, d), lambda i: (i, 0)),
            _resident((WIDTH_A, d)),
            _resident((WIDTH_B, d)),
            _resident((d, d)),
        ],
        out_specs=pl.BlockSpec((tm, d), lambda i: (i, 0)),
        scratch_shapes=[pltpu.VMEM((tm, d), BF16)],
        compiler_params=_params(("parallel",), 56),
        name="merge",
    )(oa, ob, z, z, h, wa, wb, wo)


def _ple_kernel(h_ref, n_ref, p_ref, wg_ref, wp_ref, gf_ref, y_ref):
    gate = jax.nn.sigmoid(jnp.dot(n_ref[...], wg_ref[...], preferred_element_type=F32))
    emb = jnp.dot(p_ref[...].astype(BF16), wp_ref[...], preferred_element_type=F32)
    y_ref[...] = _rms(h_ref[...] + gate * emb) * gf_ref[...]


def _ple(h, n, row0, p, wg, wp, gf, *, tm=512):
    t, d = p.shape[0], h.shape[1]
    assert t % tm == 0 and row0 % tm == 0
    tile0 = row0 // tm
    return pl.pallas_call(
        _ple_kernel,
        out_shape=jax.ShapeDtypeStruct((t, d), F32),
        grid=(t // tm,),
        in_specs=[
            pl.BlockSpec((tm, d), lambda i: (i + tile0, 0)),
            pl.BlockSpec((tm, d), lambda i: (i + tile0, 0)),
            pl.BlockSpec((tm, D_PLE), lambda i: (i, 0)),
            _resident((d, d)),
            _resident((D_PLE, d)),
            _resident((1, d)),
        ],
        out_specs=pl.BlockSpec((tm, d), lambda i: (i, 0)),
        compiler_params=_params(("parallel",), 48),
        name="ple",
    )(h, n, p, wg, wp, gf)


CAST_BLOCK_BYTES = 4 * MIB


def _cast_kernel(w_ref, o_ref, *, scale):
    w = w_ref[...]
    o_ref[...] = (w if scale == 1.0 else w * scale).astype(BF16)


def _to_bf16(w, layer, scale=1.0):
    _, r, c = w.shape
    target = max(ROW_CHUNK, CAST_BLOCK_BYTES // (4 * c))
    rows = next(n for n in range(min(r, target) // ROW_CHUNK * ROW_CHUNK, 0, -ROW_CHUNK) if r % n == 0)
    return pl.pallas_call(
        functools.partial(_cast_kernel, scale=scale),
        out_shape=jax.ShapeDtypeStruct((r, c), BF16),
        grid=(r // rows,),
        in_specs=[pl.BlockSpec((None, rows, c), lambda i: (layer, i, 0))],
        out_specs=pl.BlockSpec((rows, c), lambda i: (i, 0)),
        compiler_params=_params(("parallel",), 32),
        name="to_bf16",
    )(w)


def _pack_w_in_kernel(w_ref, qkv_ref, proj_ref, *, qkv_width, chunk_starts, tn):
    qkv_ref[...] = w_ref[:, :qkv_width].astype(BF16)
    for c, start in enumerate(chunk_starts):
        proj_ref[c] = w_ref[:, start:start + tn].astype(BF16)


def _pack_w_in(w_in, layer, qkv_width, chunk_starts, tn, *, rows=64):
    _, d, n = w_in.shape
    assert d % rows == 0 and all(s % HEAD_DIM == 0 and s + tn <= n for s in chunk_starts)
    return pl.pallas_call(
        functools.partial(_pack_w_in_kernel, qkv_width=qkv_width, chunk_starts=chunk_starts, tn=tn),
        out_shape=(jax.ShapeDtypeStruct((d, qkv_width), BF16),
                   jax.ShapeDtypeStruct((len(chunk_starts), d, tn), BF16)),
        grid=(d // rows,),
        in_specs=[pl.BlockSpec((None, rows, n), lambda i: (layer, i, 0))],
        out_specs=(pl.BlockSpec((rows, qkv_width), lambda i: (i, 0)),
                   pl.BlockSpec((len(chunk_starts), rows, tn), lambda i: (0, i, 0))),
        compiler_params=_params(("parallel",), 32),
        name="pack_w_in",
    )(w_in)


def _pack_gate_up_kernel(*refs, tf):
    o_ref = refs[-1]
    for slot in range(len(refs) // 2):
        @pl.when(pl.program_id(0) == slot)
        def _(g_ref=refs[2 * slot], u_ref=refs[2 * slot + 1]):
            for j in range(o_ref.shape[0]):
                o_ref[j, :, :tf] = g_ref[:, j * tf:(j + 1) * tf].astype(BF16)
                o_ref[j, :, tf:] = u_ref[:, j * tf:(j + 1) * tf].astype(BF16)


def _pack_gate_up(gate_up_pairs, layer, *, tf, rows=64):
    _, r, c = gate_up_pairs[0][0].shape
    assert r % rows == 0 and c % tf == 0
    n_chunks, n_blocks = c // tf, r // rows

    def in_spec(slot):
        return pl.BlockSpec((None, rows, c), lambda s, i: (layer, jnp.where(s == slot, i, n_blocks - 1), 0))

    weights = [w for pair in gate_up_pairs for w in pair]
    assert all(w.shape == weights[0].shape for w in weights)
    return pl.pallas_call(
        functools.partial(_pack_gate_up_kernel, tf=tf),
        out_shape=jax.ShapeDtypeStruct((len(gate_up_pairs), n_chunks, r, 2 * tf), BF16),
        grid=(len(gate_up_pairs), n_blocks),
        in_specs=[in_spec(slot) for slot in range(len(gate_up_pairs)) for _ in range(2)],
        out_specs=pl.BlockSpec((None, n_chunks, rows, 2 * tf), lambda s, i: (s, 0, i, 0)),
        compiler_params=_params(("arbitrary", "arbitrary"), 32),
        name="pack_gate_up",
    )(*weights)


def _rotary_tables(seq, gain_q, gain_k):
    n_freq = HEAD_DIM // 4
    inv_freq = ROPE_THETA ** (-jnp.arange(n_freq, dtype=F32) / n_freq)
    t = jnp.arange(seq)
    row = (t // GRID_W).astype(F32)
    col = (t % GRID_W).astype(F32)
    ang = jnp.concatenate([row[:, None] * inv_freq[None], col[:, None] * inv_freq[None]], axis=-1)
    cos = jnp.concatenate([jnp.cos(ang), jnp.cos(ang)], axis=-1)
    sin = jnp.concatenate([-jnp.sin(ang), jnp.sin(ang)], axis=-1)
    return jnp.concatenate([cos * g if part == 0 else sin * jnp.roll(g, HEAD_DIM // 2)
                            for g in (gain_q, gain_k) for part in (0, 1)], axis=-1)


_W_IN_SEGMENTS = dict(q_b=QKV_A_WIDTH, k_b=QKV_A_WIDTH + WIDTH_B, v_b=QKV_A_WIDTH + 2 * WIDTH_B,
                      g_a=QKV_A_WIDTH + 3 * WIDTH_B, g_b=QKV_A_WIDTH + 3 * WIDTH_B + D_MODEL)
_PROJ_LAYOUT = (("g_a", D_MODEL), ("g_b", D_MODEL), ("q_b", WIDTH_B), ("k_b", WIDTH_B), ("v_b", WIDTH_B))
_PROJ_CHUNK_STARTS = tuple(_W_IN_SEGMENTS[name] + off
                           for name, width in _PROJ_LAYOUT for off in range(0, width, PROJ_TN))
assert all(width % PROJ_TN == 0 for _, width in _PROJ_LAYOUT)
_GA_COL, _GB_COL = 0, 1
_QB_COL, _KB_COL, _VB_COL = 4, 5, 6
assert 2 * D_MODEL == _QB_COL * WIDTH_B


def _prepare_weights(ffn1_norm, ffn1_w_gate, ffn1_w_up, ffn1_w_down, mix_norm, w_in, q_norm, k_norm,
                     w_branch_a, w_branch_b, w_out, ffn2_norm, ffn2_w_gate, ffn2_w_up, ffn2_w_down,
                     ple_norm, w_ple_gate, w_ple_proj, final_norm, layer):
    i = layer
    half = np.concatenate([np.arange(0, HEAD_DIM, 2), np.arange(1, HEAD_DIM, 2)])
    w_qkv_a, w_proj = _pack_w_in(w_in, i, QKV_A_WIDTH, _PROJ_CHUNK_STARTS, PROJ_TN)
    w_rot = _rotary_weight_layout(w_qkv_a)
    w_gate_up = _pack_gate_up([(ffn1_w_gate, ffn1_w_up), (ffn2_w_gate, ffn2_w_up)], i, tf=FFN_TF)
    q_scale = ATTN_SCALE * LOG2_E
    proj_scale = jnp.concatenate([jnp.full((width,), q_scale if name == "q_b" else 1.0, F32)
                                  for name, width in _PROJ_LAYOUT])[None]
    rot_gains = ((q_norm[i] * q_scale)[half], k_norm[i][half])
    gqa_score_bound = (HEAD_DIM * q_scale * BF16_NORM_MARGIN
                       * jnp.max(jnp.abs(q_norm[i])) * jnp.max(jnp.abs(k_norm[i])))
    return dict(
        ffn1=(ffn1_norm[i][None], w_gate_up, 0, _to_bf16(ffn1_w_down, i, FFN_RESIDUAL_SCALE), mix_norm[i][None]),
        w_rot=w_rot, rot_gains=rot_gains, w_qkv_a=w_qkv_a, w_proj=w_proj, proj_scale=proj_scale, gqa_score_bound=gqa_score_bound,
        merge=(_to_bf16(w_branch_a, i), _to_bf16(w_branch_b, i), _to_bf16(w_out, i)),
        ffn2=(ffn2_norm[i][None], w_gate_up, 1, _to_bf16(ffn2_w_down, i, FFN_RESIDUAL_SCALE), ple_norm[i][None]),
        ple=(_to_bf16(w_ple_gate, i), _to_bf16(w_ple_proj, i), final_norm[None]),
    )


def _encoder(xs, ps, wts, rot, col_bias):
    s, d = xs[0].shape[1:]
    assert all(x.shape[1:] == (s, d) for x in xs)
    b = sum(x.shape[0] for x in xs)
    t = b * s
    h1, u = _ffn([x.reshape(-1, d) for x in xs], *wts["ffn1"])
    q_a, k_a, v_a = _qkv_a(u, wts["w_rot"], wts["w_qkv_a"], rot, s)
    z = _proj(u, wts["w_proj"], wts["proj_scale"])
    qkv = (q_a.reshape(b, s, -1), k_a.reshape(b, s, -1), v_a.reshape(b, s, -1))
    o_a = lax.cond(wts["gqa_score_bound"] <= GQA_SAFE_LOG2_SCORE,
                   functools.partial(_gqa, bounded=True), functools.partial(_gqa, bounded=False), *qkv)
    o_b = _nbr(z.reshape(b, s, -1), col_bias, _QB_COL, _KB_COL, _VB_COL)
    h2 = _merge(o_a.reshape(t, -1), o_b.reshape(t, -1), z, _GA_COL, _GB_COL, h1, *wts["merge"])
    h3, n = _ffn([h2], *wts["ffn2"])
    row_starts = np.cumsum([0] + [x.shape[0] * s for x in xs])
    return tuple(_ple(h3, n, int(row0), p.reshape(-1, p.shape[-1]), *wts["ple"]).reshape(x.shape)
                 for row0, p, x in zip(row_starts, ps, xs))


def kernel(x_prompt, x_sample, p_prompt, p_sample, ffn1_norm, ffn1_w_gate, ffn1_w_up, ffn1_w_down, mix_norm, w_in, q_norm, k_norm, nat_rpb, w_branch_a, w_branch_b, w_out, ffn2_norm, ffn2_w_gate, ffn2_w_up, ffn2_w_down, ple_norm, w_ple_gate, w_ple_proj, final_norm):
    assert ffn1_norm.shape[0] == 1, "single-layer encoder"
    wts = _prepare_weights(ffn1_norm, ffn1_w_gate, ffn1_w_up, ffn1_w_down, mix_norm, w_in, q_norm, k_norm,
                           w_branch_a, w_branch_b, w_out, ffn2_norm, ffn2_w_gate, ffn2_w_up, ffn2_w_down,
                           ple_norm, w_ple_gate, w_ple_proj, final_norm, 0)
    col_bias = _nbr_col_bias(nat_rpb[0])
    groups = ((x_prompt, p_prompt[0]), (x_sample, p_sample[0]))
    outs = {}
    for seq in sorted({x.shape[1] for x, _ in groups}):
        members = [k for k, (x, _) in enumerate(groups) if x.shape[1] == seq]
        rot = _rotary_tables(seq, *wts["rot_gains"])
        ys = _encoder([groups[k][0] for k in members], [groups[k][1] for k in members], wts, rot, col_bias)
        outs.update(zip(members, ys))
    return tuple(outs[k] for k in range(len(groups)))
```

```python
import functools
import math

import numpy as np
import jax
import jax.numpy as jnp
from jax import lax
from jax.experimental import pallas as pl
from jax.experimental.pallas import tpu as pltpu

F32 = jnp.float32
BF16 = jnp.bfloat16

D_MODEL = 2048
HEAD_DIM = 128
N_HEADS_A = 8
N_KV_A = 2
GQA_GROUP = N_HEADS_A // N_KV_A
N_HEADS_B = 8
WIDTH_A = N_HEADS_A * HEAD_DIM
KV_WIDTH_A = N_KV_A * HEAD_DIM
WIDTH_B = N_HEADS_B * HEAD_DIM
ROT_WIDTH = WIDTH_A + KV_WIDTH_A
QKV_A_WIDTH = ROT_WIDTH + KV_WIDTH_A
D_FF = 5632
D_PLE = 256
GRID_W = 64
WIN_ROWS = 8
WIN_COLS = 16
ROPE_THETA = 10000.0
EPS = 1e-6
ATTN_SCALE = HEAD_DIM ** -0.5
LOG2_E = math.log2(math.e)
MASKED = -1e30
GQA_SAFE_LOG2_SCORE = 48.0
BF16_NORM_MARGIN = 1.02

NBR_Q_ROWS = 4
NBR_KEY_ROWS = 12
NBR_Q = NBR_Q_ROWS * GRID_W
NBR_LOOKAHEAD = 2
NBR_DR_PAD = (NBR_KEY_ROWS - NBR_Q_ROWS) - (WIN_ROWS - 1) + (NBR_Q_ROWS - 1)
NBR_DR_SLOTS = NBR_DR_PAD + (WIN_ROWS - 1) + (NBR_KEY_ROWS - 2) + 1

PROJ_TN = 1024

V7X_VMEM_BYTES = 64 * 1024 * 1024
MIB = 1024 * 1024


def _params(semantics, vmem_mib):
    assert vmem_mib * MIB < V7X_VMEM_BYTES
    return pltpu.CompilerParams(dimension_semantics=semantics, vmem_limit_bytes=vmem_mib * MIB)


def _resident(shape):
    zeros = (0,) * len(shape)
    return pl.BlockSpec(shape, lambda *_: zeros, pipeline_mode=pl.Buffered(1))


def _rms(x):
    return x * lax.rsqrt(jnp.mean(x * x, axis=-1, keepdims=True) + EPS)


FFN_RESIDUAL_SCALE = 0.5
FFN_TF = 512
ROW_CHUNK = 16
NORM_ROWS = 128
ROW_UNROLL_EXIT = 16

def _ffn_kernel(*refs, tile_starts):
    n_src = len(tile_starts) - 1
    x_srcs = refs[:n_src]
    g1_ref, wgu_ref, wd_ref, g2_ref, h_ref, n_ref, x_buf, xn_ref, x_sem = refs[n_src:]
    i = pl.program_id(0)
    j = pl.program_id(1)
    tm = x_buf.shape[0]

    def x_copy(tile, wait):
        for s, x_hbm in enumerate(x_srcs):
            @pl.when((tile >= tile_starts[s]) & (tile < tile_starts[s + 1]))
            def _(s=s, x_hbm=x_hbm):
                row = pl.multiple_of((tile - tile_starts[s]) * tm, tm)
                copy = pltpu.make_async_copy(x_hbm.at[pl.ds(row, tm), :], x_buf, x_sem)
                copy.wait() if wait else copy.start()

    cur = i % 2
    has_next = i + 1 < pl.num_programs(0)
    n_chunks = tm // NORM_ROWS

    def norm_rows(slot, row0, n_rows):
        for r in range(0, n_rows, ROW_CHUNK):
            rows = pl.ds(row0 + r, ROW_CHUNK)
            xn_ref[slot, rows, :] = (_rms(x_buf[rows, :]) * g1_ref[...]).astype(BF16)

    @pl.when((i == 0) & (j == 0))
    def _():
        x_copy(0, wait=False)
        x_copy(0, wait=True)

        def first(c, carry):
            norm_rows(0, pl.multiple_of(c * NORM_ROWS, NORM_ROWS), NORM_ROWS)
            return carry

        lax.fori_loop(0, n_chunks, first, 0)

    @pl.when((j == 1) & has_next)
    def _():
        x_copy(i + 1, wait=False)

    @pl.when((j == 2) & has_next)
    def _():
        x_copy(i + 1, wait=True)

    tf = wd_ref.shape[0]

    def step(first_step, prepare_next):
        gu = jnp.dot(xn_ref[cur], wgu_ref[...], preferred_element_type=F32)
        g, u = gu[:, :tf], gu[:, tf:]
        a = (g * jax.nn.sigmoid(g) * u).astype(BF16)
        down = jnp.dot(a, wd_ref[...], preferred_element_type=F32)
        h_ref[...] = (x_buf[...] if first_step else h_ref[...]) + down
        if prepare_next:
            chunk = jnp.minimum(j - 2, n_chunks - 1)
            norm_rows(1 - cur, pl.multiple_of(chunk * NORM_ROWS, NORM_ROWS), NORM_ROWS)

    pl.when(j == 0)(functools.partial(step, True, False))
    pl.when(j == 1)(functools.partial(step, False, False))
    pl.when(j >= 2)(functools.partial(step, False, True))

    @pl.when(j == pl.num_programs(1) - 1)
    def _():
        def last(r, carry):
            rows = pl.ds(pl.multiple_of(r * ROW_CHUNK, ROW_CHUNK), ROW_CHUNK)
            n_ref[rows, :] = (_rms(h_ref[rows, :]) * g2_ref[...]).astype(BF16)
            return carry

        lax.fori_loop(0, tm // ROW_CHUNK, last, 0, unroll=ROW_UNROLL_EXIT)


def _ffn(xs, g1, w_gate_up, slot, wd, g2, *, tm=1024):
    d = xs[0].shape[1]
    dff, tf = wd.shape[0], FFN_TF
    assert all(x.shape[0] % tm == 0 and x.shape[1] == d for x in xs)
    assert w_gate_up.shape[1:] == (dff // tf, d, 2 * tf)
    assert tm % NORM_ROWS == 0 and tm // NORM_ROWS <= dff // tf - 2
    tile_starts = tuple(int(n) for n in np.cumsum([0] + [x.shape[0] // tm for x in xs]))
    t = tile_starts[-1] * tm
    return pl.pallas_call(
        functools.partial(_ffn_kernel, tile_starts=tile_starts),
        out_shape=(jax.ShapeDtypeStruct((t, d), F32), jax.ShapeDtypeStruct((t, d), BF16)),
        grid=(t // tm, dff // tf),
        in_specs=[pl.BlockSpec(memory_space=pl.ANY)] * len(xs) + [
            pl.BlockSpec((1, d), lambda i, j: (0, 0)),
            pl.BlockSpec((None, None, d, 2 * tf), lambda i, j: (slot, j, 0, 0)),
            pl.BlockSpec((tf, d), lambda i, j: (j, 0)),
            pl.BlockSpec((1, d), lambda i, j: (0, 0)),
        ],
        out_specs=(
            pl.BlockSpec((tm, d), lambda i, j: (i, 0)),
            pl.BlockSpec((tm, d), lambda i, j: (i, 0)),
        ),
        scratch_shapes=[pltpu.VMEM((tm, d), F32), pltpu.VMEM((2, tm, d), BF16), pltpu.SemaphoreType.DMA(())],
        compiler_params=_params(("arbitrary", "arbitrary"), 60),
        name="ffn",
    )(*xs, g1, w_gate_up, wd, g2)


def _permute_heads_kernel(w_ref, perm_ref, o_ref):
    o_ref[...] = jnp.dot(w_ref[...], perm_ref[...], preferred_element_type=F32).astype(BF16)


def _rotary_weight_layout(w_in):
    d = w_in.shape[0]
    half = np.concatenate([np.arange(0, HEAD_DIM, 2), np.arange(1, HEAD_DIM, 2)])
    perm = jnp.asarray(np.arange(HEAD_DIM)[:, None] == half[None, :], BF16)
    return pl.pallas_call(
        _permute_heads_kernel,
        out_shape=jax.ShapeDtypeStruct((d, ROT_WIDTH), BF16),
        grid=(ROT_WIDTH // HEAD_DIM,),
        in_specs=[pl.BlockSpec((d, HEAD_DIM), lambda h: (0, h)), _resident((HEAD_DIM, HEAD_DIM))],
        out_specs=pl.BlockSpec((d, HEAD_DIM), lambda h: (0, h)),
        compiler_params=_params(("parallel",), 16),
        name="rotary_weight_layout",
    )(w_in, perm)


def _qkv_a_kernel(u_ref, wr_ref, wv_ref, rot_ref, q_ref, k_ref, v_ref, z_buf, v_buf, *, n_tiles):
    i = pl.program_id(0)

    def project(slot):
        u = u_ref[...]
        z_buf[slot] = jnp.dot(u, wr_ref[...], preferred_element_type=F32)
        v_buf[slot] = jnp.dot(u, wv_ref[...], preferred_element_type=F32).astype(BF16)

    def finish(slot):
        for hh in range(N_HEADS_A + N_KV_A):
            zh = z_buf[slot, :, hh * HEAD_DIM:(hh + 1) * HEAD_DIM]
            tab = 0 if hh < N_HEADS_A else 2
            cos_g = rot_ref[:, tab * HEAD_DIM:(tab + 1) * HEAD_DIM]
            sin_g = rot_ref[:, (tab + 1) * HEAD_DIM:(tab + 2) * HEAD_DIM]
            r = lax.rsqrt(jnp.mean(zh * zh, axis=-1, keepdims=True) + EPS)
            out = ((zh * cos_g + pltpu.roll(zh, HEAD_DIM // 2, 1) * sin_g) * r).astype(BF16)
            if hh < N_HEADS_A:
                q_ref[:, hh * HEAD_DIM:(hh + 1) * HEAD_DIM] = out
            else:
                kk = hh - N_HEADS_A
                k_ref[:, kk * HEAD_DIM:(kk + 1) * HEAD_DIM] = out
        v_ref[...] = v_buf[slot]

    @pl.when(i == 0)
    def _():
        project(0)

    for parity in range(2):
        @pl.when((i > 0) & (i < n_tiles) & (i % 2 == parity))
        def _(parity=parity):
            project(parity)
            finish(1 - parity)

    @pl.when(i == n_tiles)
    def _():
        finish((n_tiles - 1) % 2)


def _qkv_a(u, w_rot, w_in, rot, seq, *, tm=512):
    t, d = u.shape
    assert t % tm == 0 and seq % tm == 0 and ROT_WIDTH % KV_WIDTH_A == 0
    n_tiles, seq_blocks = t // tm, seq // tm

    def lagged(i):
        return jnp.maximum(i - 1, 0)

    return pl.pallas_call(
        functools.partial(_qkv_a_kernel, n_tiles=n_tiles),
        out_shape=(
            jax.ShapeDtypeStruct((t, WIDTH_A), BF16),
            jax.ShapeDtypeStruct((t, KV_WIDTH_A), BF16),
            jax.ShapeDtypeStruct((t, KV_WIDTH_A), BF16),
        ),
        grid=(n_tiles + 1,),
        in_specs=[
            pl.BlockSpec((tm, d), lambda i: (jnp.minimum(i, n_tiles - 1), 0)),
            _resident((d, ROT_WIDTH)),
            pl.BlockSpec((d, KV_WIDTH_A), lambda i: (0, ROT_WIDTH // KV_WIDTH_A), pipeline_mode=pl.Buffered(1)),
            pl.BlockSpec((tm, 4 * HEAD_DIM), lambda i: (lagged(i) % seq_blocks, 0)),
        ],
        out_specs=(
            pl.BlockSpec((tm, WIDTH_A), lambda i: (lagged(i), 0)),
            pl.BlockSpec((tm, KV_WIDTH_A), lambda i: (lagged(i), 0)),
            pl.BlockSpec((tm, KV_WIDTH_A), lambda i: (lagged(i), 0)),
        ),
        scratch_shapes=[pltpu.VMEM((2, tm, ROT_WIDTH), F32), pltpu.VMEM((2, tm, KV_WIDTH_A), BF16)],
        compiler_params=_params(("arbitrary",), 40),
        name="qkv_a",
    )(u, w_rot, w_in, rot)


def _proj_kernel(u_ref, w_ref, scale_ref, z_ref):
    z = jnp.dot(u_ref[...], w_ref[...], preferred_element_type=F32)
    z_ref[...] = (z * scale_ref[...]).astype(BF16)


def _proj(u, w, col_scale, *, tm=2048):
    t, d = u.shape
    n_chunks, _, tn = w.shape
    assert t % tm == 0 and col_scale.shape == (1, n_chunks * tn)
    return pl.pallas_call(
        _proj_kernel,
        out_shape=jax.ShapeDtypeStruct((t, n_chunks * tn), BF16),
        grid=(t // tm, n_chunks),
        in_specs=[
            pl.BlockSpec((tm, d), lambda i, j: (i, 0)),
            pl.BlockSpec((None, d, tn), lambda i, j: (j, 0, 0)),
            pl.BlockSpec((1, tn), lambda i, j: (0, j)),
        ],
        out_specs=pl.BlockSpec((tm, tn), lambda i, j: (i, j)),
        compiler_params=_params(("parallel", "arbitrary"), 48),
        name="proj",
    )(u, w, col_scale)


def _softmax_pv(s, v):
    m = jnp.max(s, axis=-1, keepdims=True)
    p = jnp.exp2(s - m)
    l = jnp.sum(p, axis=-1, keepdims=True)
    return jnp.dot(p.astype(BF16), v, preferred_element_type=F32) / l


_NT = (((1,), (1,)), ((), ()))


def _gqa_kernel(q_ref, k_ref, v_ref, o_ref, *, bounded):
    k = k_ref[0]
    v = v_ref[0]

    def score(g):
        return lax.dot_general(q_ref[0, :, g * HEAD_DIM:(g + 1) * HEAD_DIM], k, _NT, preferred_element_type=F32)

    s_next = score(0)
    for g in range(GQA_GROUP):
        s = s_next
        if g + 1 < GQA_GROUP:
            s_next = score(g + 1)
        if bounded:
            p = jnp.exp2(s)
            l = jnp.sum(p, axis=-1, keepdims=True)
            o = jnp.dot(p.astype(BF16), v, preferred_element_type=F32) / l
        else:
            o = _softmax_pv(s, v)
        o_ref[0, :, g * HEAD_DIM:(g + 1) * HEAD_DIM] = o.astype(BF16)


def _gqa(q, k, v, *, bounded, tq=1024):
    b, s, _ = q.shape
    assert s % tq == 0
    gw = GQA_GROUP * HEAD_DIM
    return pl.pallas_call(
        functools.partial(_gqa_kernel, bounded=bounded),
        out_shape=jax.ShapeDtypeStruct((b, s, WIDTH_A), BF16),
        grid=(b, N_KV_A, s // tq),
        in_specs=[
            pl.BlockSpec((1, tq, gw), lambda bi, kh, qi: (bi, qi, kh)),
            pl.BlockSpec((1, s, HEAD_DIM), lambda bi, kh, qi: (bi, 0, kh)),
            pl.BlockSpec((1, s, HEAD_DIM), lambda bi, kh, qi: (bi, 0, kh)),
        ],
        out_specs=pl.BlockSpec((1, tq, gw), lambda bi, kh, qi: (bi, qi, kh)),
        compiler_params=_params(("parallel", "parallel", "arbitrary"), 40),
        name="gqa_bounded" if bounded else "gqa",
    )(q, k, v)


def _nbr_window_start(j, rows):
    return jnp.clip(NBR_Q_ROWS * j - WIN_ROWS // 2, 0, rows - NBR_KEY_ROWS)


def _nbr_block_plans(rows):
    plans = {}
    for j in range(rows // NBR_Q_ROWS):
        ws = int(np.clip(NBR_Q_ROWS * j - WIN_ROWS // 2, 0, rows - NBR_KEY_ROWS))
        lo = tuple(int(np.clip(NBR_Q_ROWS * j + qr - WIN_ROWS // 2, 0, rows - WIN_ROWS)) - ws
                   for qr in range(NBR_Q_ROWS))
        d0 = ws - NBR_Q_ROWS * j + (WIN_ROWS - 1) + NBR_DR_PAD
        assert min(lo) >= 0 and max(lo) + WIN_ROWS <= NBR_KEY_ROWS
        plans.setdefault((lo, d0), []).append(j)
    return plans


def _nbr_block(q_ref, k_ref, v_ref, cb_ref, o_ref, ws, lo, d0):
    pair_w = 2 * GRID_W
    pa = min(lo) // 2
    pb = (max(lo) + WIN_ROWS - 1) // 2 + 1
    n_keys = (pb - pa) * pair_w
    start = pl.multiple_of(ws * GRID_W + pa * pair_w, pair_w)
    lane = lax.broadcasted_iota(jnp.int32, (GRID_W, pair_w), 1)
    zeros = jnp.zeros((GRID_W, pair_w), BF16)
    heads = [slice(h * HEAD_DIM, (h + 1) * HEAD_DIM) for h in range(N_HEADS_B)]

    def score(cols):
        return lax.dot_general(q_ref[0, :, cols], k_ref[pl.ds(start, n_keys), cols], _NT,
                               preferred_element_type=F32)

    depth = NBR_LOOKAHEAD if len(set(lo)) > 1 else 0
    queue = [score(cols) for cols in heads[:depth]]
    for h, cols in enumerate(heads):
        s = queue.pop(0) if queue else score(cols)
        p_rows, l_rows = [], []
        for qr in range(NBR_Q_ROWS):
            first, last = lo[qr], lo[qr] + WIN_ROWS - 1
            tiles = []
            for a in range(first // 2, last // 2 + 1):
                t = (s[qr * GRID_W:(qr + 1) * GRID_W, (a - pa) * pair_w:(a - pa + 1) * pair_w]
                     + cb_ref[h, d0 + 2 * a - qr])
                if 2 * a < first:
                    t = jnp.where(lane < GRID_W, MASKED, t)
                if 2 * a + 1 > last:
                    t = jnp.where(lane >= GRID_W, MASKED, t)
                tiles.append(t)
            m = jnp.max(functools.reduce(jnp.maximum, tiles), axis=-1, keepdims=True)
            ps = [jnp.exp2(t - m) for t in tiles]
            l_rows.append(jnp.sum(functools.reduce(jnp.add, ps), axis=-1, keepdims=True))
            p_rows.append(jnp.concatenate(
                [zeros] * (first // 2 - pa) + [p.astype(BF16) for p in ps] + [zeros] * (pb - 1 - last // 2),
                axis=1))
        p = jnp.concatenate(p_rows, axis=0)
        l = jnp.concatenate(l_rows, axis=0)
        if depth and h + depth < N_HEADS_B:
            queue.append(score(heads[h + depth]))
        v = v_ref[pl.ds(start, n_keys), cols]
        o_ref[0, :, cols] = (jnp.dot(p, v, preferred_element_type=F32) / l).astype(BF16)


def _nbr_kernel(q_ref, z_hbm, cb_ref, o_ref, kv_buf, kv_sem, *, rows, kv_cols):
    b = pl.program_id(0)
    j = pl.program_id(1)
    slot = b % 2

    def kv_copies(seq, dst_slot):
        return [pltpu.make_async_copy(z_hbm.at[seq, :, pl.ds(col * WIDTH_B, WIDTH_B)],
                                      kv_buf.at[dst_slot, which], kv_sem.at[which])
                for which, col in enumerate(kv_cols)]

    @pl.when(j == 0)
    def _():
        @pl.when(b == 0)
        def _():
            for copy in kv_copies(0, 0):
                copy.start()

        for copy in kv_copies(b, slot):
            copy.wait()

        @pl.when(b + 1 < pl.num_programs(0))
        def _():
            for copy in kv_copies(b + 1, 1 - slot):
                copy.start()

    k_ref, v_ref = kv_buf.at[slot, 0], kv_buf.at[slot, 1]
    ws = _nbr_window_start(j, rows)
    for (lo, d0), members in _nbr_block_plans(rows).items():
        member = functools.reduce(jnp.logical_or, [j == m for m in members])
        pl.when(member)(functools.partial(_nbr_block, q_ref, k_ref, v_ref, cb_ref, o_ref, ws, lo, d0))


def _nbr(z, col_bias, q_col, k_col, v_col):
    b, s, _ = z.shape
    rows = s // GRID_W
    assert rows % NBR_Q_ROWS == 0 and rows >= NBR_KEY_ROWS
    return pl.pallas_call(
        functools.partial(_nbr_kernel, rows=rows, kv_cols=(k_col, v_col)),
        out_shape=jax.ShapeDtypeStruct((b, s, WIDTH_B), BF16),
        grid=(b, rows // NBR_Q_ROWS),
        in_specs=[
            pl.BlockSpec((1, NBR_Q, WIDTH_B), lambda bi, j: (bi, j, q_col)),
            pl.BlockSpec(memory_space=pl.ANY),
            _resident(col_bias.shape),
        ],
        out_specs=pl.BlockSpec((1, NBR_Q, WIDTH_B), lambda bi, j: (bi, j, 0)),
        scratch_shapes=[pltpu.VMEM((2, 2, s, WIDTH_B), BF16), pltpu.SemaphoreType.DMA((2,))],
        compiler_params=_params(("arbitrary", "arbitrary"), 48),
        name="nbr",
    )(z, z, col_bias)


def _nbr_col_bias(rpb):
    n_dr, n_dc = 2 * WIN_ROWS - 1, 2 * WIN_COLS - 1
    c = np.arange(GRID_W)[:, None]
    kc = np.arange(GRID_W)[None, :]
    cs = np.clip(c - WIN_COLS // 2, 0, GRID_W - WIN_COLS)
    col_valid = (kc >= cs) & (kc < cs + WIN_COLS)
    onehot = ((kc - c + (WIN_COLS - 1))[None] == np.arange(n_dc)[:, None, None]) & col_valid[None]
    onehot = jnp.asarray(onehot.reshape(n_dc, GRID_W * GRID_W), F32)
    tm = jnp.einsum("hrd,dn->hrn", rpb * LOG2_E, onehot, precision=lax.Precision.HIGHEST)
    tm = jnp.where(col_valid.reshape(-1), tm, MASKED).reshape(-1, n_dr, GRID_W, GRID_W)
    hi_pad = NBR_DR_SLOTS + 1 - NBR_DR_PAD - n_dr
    tm = jnp.pad(tm, ((0, 0), (NBR_DR_PAD, hi_pad), (0, 0), (0, 0)), constant_values=MASKED)
    return jnp.concatenate([tm[:, :-1], tm[:, 1:]], axis=-1)


MERGE_TN = 256


def _merge_kernel(oa_ref, ob_ref, ga_ref, gb_ref, h_ref, wa_ref, wb_ref, wo_ref, out_ref, m_ref):
    oa = oa_ref[...]
    ob = ob_ref[...]
    for c in range(D_MODEL // MERGE_TN):
        cols = slice(c * MERGE_TN, (c + 1) * MERGE_TN)
        a = jnp.dot(oa, wa_ref[:, cols], preferred_element_type=F32)
        b = jnp.dot(ob, wb_ref[:, cols], preferred_element_type=F32)
        merged = (jax.nn.sigmoid(ga_ref[:, cols].astype(F32)) * a
                  + jax.nn.sigmoid(gb_ref[:, cols].astype(F32)) * b)
        m_ref[:, cols] = merged.astype(BF16)
    out_ref[...] = h_ref[...] + jnp.dot(m_ref[...], wo_ref[...], preferred_element_type=F32)


def _merge(oa, ob, z, ga_col, gb_col, h, wa, wb, wo, *, tm=512):
    t, d = h.shape
    assert t % tm == 0 and d == D_MODEL
    return pl.pallas_call(
        _merge_kernel,
        out_shape=jax.ShapeDtypeStruct((t, d), F32),
        grid=(t // tm,),
        in_specs=[
            pl.BlockSpec((tm, WIDTH_A), lambda i: (i, 0)),
            pl.BlockSpec((tm, WIDTH_B), lambda i: (i, 0)),
            pl.BlockSpec((tm, d), lambda i: (i, ga_col)),
            pl.BlockSpec((tm, d), lambda i: (i, gb_col)),
            pl.BlockSpec((tm, d), lambda i: (i, 0)),
            _resident((WIDTH_A, d)),
            _resident((WIDTH_B, d)),
            _resident((d, d)),
        ],
        out_specs=pl.BlockSpec((tm, d), lambda i: (i, 0)),
        scratch_shapes=[pltpu.VMEM((tm, d), BF16)],
        compiler_params=_params(("parallel",), 56),
        name="merge",
    )(oa, ob, z, z, h, wa, wb, wo)


PLE_TN = 256


def _ple_kernel(h_ref, n_ref, p_ref, wg_ref, wp_ref, gf_ref, y_ref):
    n = n_ref[...]
    p = p_ref[...].astype(BF16)
    d = y_ref.shape[1]
    sum_sq = jnp.zeros((y_ref.shape[0], 1), F32)
    for c in range(d // PLE_TN):
        cols = slice(c * PLE_TN, (c + 1) * PLE_TN)
        gate = jax.nn.sigmoid(jnp.dot(n, wg_ref[:, cols], preferred_element_type=F32))
        emb = jnp.dot(p, wp_ref[:, cols], preferred_element_type=F32)
        t = h_ref[:, cols] + gate * emb
        y_ref[:, cols] = t
        sum_sq = sum_sq + jnp.sum(t * t, axis=-1, keepdims=True)
    y_ref[...] = y_ref[...] * lax.rsqrt(sum_sq / d + EPS) * gf_ref[...]


def _ple(h, n, row0, p, wg, wp, gf, *, tm=512):
    t, d = p.shape[0], h.shape[1]
    assert t % tm == 0 and row0 % tm == 0
    tile0 = row0 // tm
    return pl.pallas_call(
        _ple_kernel,
        out_shape=jax.ShapeDtypeStruct((t, d), F32),
        grid=(t // tm,),
        in_specs=[
            pl.BlockSpec((tm, d), lambda i: (i + tile0, 0)),
            pl.BlockSpec((tm, d), lambda i: (i + tile0, 0)),
            pl.BlockSpec((tm, D_PLE), lambda i: (i, 0)),
            _resident((d, d)),
            _resident((D_PLE, d)),
            _resident((1, d)),
        ],
        out_specs=pl.BlockSpec((tm, d), lambda i: (i, 0)),
        compiler_params=_params(("parallel",), 48),
        name="ple",
    )(h, n, p, wg, wp, gf)


CAST_BLOCK_BYTES = 4 * MIB


def _cast_kernel(w_ref, o_ref, *, scale):
    w = w_ref[...]
    o_ref[...] = (w if scale == 1.0 else w * scale).astype(BF16)


def _to_bf16(w, layer, scale=1.0):
    _, r, c = w.shape
    target = max(ROW_CHUNK, CAST_BLOCK_BYTES // (4 * c))
    rows = next(n for n in range(min(r, target) // ROW_CHUNK * ROW_CHUNK, 0, -ROW_CHUNK) if r % n == 0)
    return pl.pallas_call(
        functools.partial(_cast_kernel, scale=scale),
        out_shape=jax.ShapeDtypeStruct((r, c), BF16),
        grid=(r // rows,),
        in_specs=[pl.BlockSpec((None, rows, c), lambda i: (layer, i, 0))],
        out_specs=pl.BlockSpec((rows, c), lambda i: (i, 0)),
        compiler_params=_params(("parallel",), 32),
        name="to_bf16",
    )(w)


def _pack_w_in_kernel(w_ref, qkv_ref, proj_ref, *, qkv_width, chunk_starts, tn):
    qkv_ref[...] = w_ref[:, :qkv_width].astype(BF16)
    for c, start in enumerate(chunk_starts):
        proj_ref[c] = w_ref[:, start:start + tn].astype(BF16)


def _pack_w_in(w_in, layer, qkv_width, chunk_starts, tn, *, rows=64):
    _, d, n = w_in.shape
    assert d % rows == 0 and all(s % HEAD_DIM == 0 and s + tn <= n for s in chunk_starts)
    return pl.pallas_call(
        functools.partial(_pack_w_in_kernel, qkv_width=qkv_width, chunk_starts=chunk_starts, tn=tn),
        out_shape=(jax.ShapeDtypeStruct((d, qkv_width), BF16),
                   jax.ShapeDtypeStruct((len(chunk_starts), d, tn), BF16)),
        grid=(d // rows,),
        in_specs=[pl.BlockSpec((None, rows, n), lambda i: (layer, i, 0))],
        out_specs=(pl.BlockSpec((rows, qkv_width), lambda i: (i, 0)),
                   pl.BlockSpec((len(chunk_starts), rows, tn), lambda i: (0, i, 0))),
        compiler_params=_params(("parallel",), 32),
        name="pack_w_in",
    )(w_in)


def _pack_gate_up_kernel(*refs, tf):
    o_ref = refs[-1]
    for slot in range(len(refs) // 2):
        @pl.when(pl.program_id(0) == slot)
        def _(g_ref=refs[2 * slot], u_ref=refs[2 * slot + 1]):
            for j in range(o_ref.shape[0]):
                o_ref[j, :, :tf] = g_ref[:, j * tf:(j + 1) * tf].astype(BF16)
                o_ref[j, :, tf:] = u_ref[:, j * tf:(j + 1) * tf].astype(BF16)


def _pack_gate_up(gate_up_pairs, layer, *, tf, rows=64):
    _, r, c = gate_up_pairs[0][0].shape
    assert r % rows == 0 and c % tf == 0
    n_chunks, n_blocks = c // tf, r // rows

    def in_spec(slot):
        return pl.BlockSpec((None, rows, c), lambda s, i: (layer, jnp.where(s == slot, i, n_blocks - 1), 0))

    weights = [w for pair in gate_up_pairs for w in pair]
    assert all(w.shape == weights[0].shape for w in weights)
    return pl.pallas_call(
        functools.partial(_pack_gate_up_kernel, tf=tf),
        out_shape=jax.ShapeDtypeStruct((len(gate_up_pairs), n_chunks, r, 2 * tf), BF16),
        grid=(len(gate_up_pairs), n_blocks),
        in_specs=[in_spec(slot) for slot in range(len(gate_up_pairs)) for _ in range(2)],
        out_specs=pl.BlockSpec((None, n_chunks, rows, 2 * tf), lambda s, i: (s, 0, i, 0)),
        compiler_params=_params(("arbitrary", "arbitrary"), 32),
        name="pack_gate_up",
    )(*weights)


def _rotary_tables(seq, gain_q, gain_k):
    n_freq = HEAD_DIM // 4
    inv_freq = ROPE_THETA ** (-jnp.arange(n_freq, dtype=F32) / n_freq)
    t = jnp.arange(seq)
    row = (t // GRID_W).astype(F32)
    col = (t % GRID_W).astype(F32)
    ang = jnp.concatenate([row[:, None] * inv_freq[None], col[:, None] * inv_freq[None]], axis=-1)
    cos = jnp.concatenate([jnp.cos(ang), jnp.cos(ang)], axis=-1)
    sin = jnp.concatenate([-jnp.sin(ang), jnp.sin(ang)], axis=-1)
    return jnp.concatenate([cos * g if part == 0 else sin * jnp.roll(g, HEAD_DIM // 2)
                            for g in (gain_q, gain_k) for part in (0, 1)], axis=-1)


_W_IN_SEGMENTS = dict(q_b=QKV_A_WIDTH, k_b=QKV_A_WIDTH + WIDTH_B, v_b=QKV_A_WIDTH + 2 * WIDTH_B,
                      g_a=QKV_A_WIDTH + 3 * WIDTH_B, g_b=QKV_A_WIDTH + 3 * WIDTH_B + D_MODEL)
_PROJ_LAYOUT = (("g_a", D_MODEL), ("g_b", D_MODEL), ("q_b", WIDTH_B), ("k_b", WIDTH_B), ("v_b", WIDTH_B))
_PROJ_CHUNK_STARTS = tuple(_W_IN_SEGMENTS[name] + off
                           for name, width in _PROJ_LAYOUT for off in range(0, width, PROJ_TN))
assert all(width % PROJ_TN == 0 for _, width in _PROJ_LAYOUT)
_GA_COL, _GB_COL = 0, 1
_QB_COL, _KB_COL, _VB_COL = 4, 5, 6
assert 2 * D_MODEL == _QB_COL * WIDTH_B


def _prepare_weights(ffn1_norm, ffn1_w_gate, ffn1_w_up, ffn1_w_down, mix_norm, w_in, q_norm, k_norm,
                     w_branch_a, w_branch_b, w_out, ffn2_norm, ffn2_w_gate, ffn2_w_up, ffn2_w_down,
                     ple_norm, w_ple_gate, w_ple_proj, final_norm, layer):
    i = layer
    half = np.concatenate([np.arange(0, HEAD_DIM, 2), np.arange(1, HEAD_DIM, 2)])
    w_qkv_a, w_proj = _pack_w_in(w_in, i, QKV_A_WIDTH, _PROJ_CHUNK_STARTS, PROJ_TN)
    w_rot = _rotary_weight_layout(w_qkv_a)
    w_gate_up = _pack_gate_up([(ffn1_w_gate, ffn1_w_up), (ffn2_w_gate, ffn2_w_up)], i, tf=FFN_TF)
    q_scale = ATTN_SCALE * LOG2_E
    proj_scale = jnp.concatenate([jnp.full((width,), q_scale if name == "q_b" else 1.0, F32)
                                  for name, width in _PROJ_LAYOUT])[None]
    rot_gains = ((q_norm[i] * q_scale)[half], k_norm[i][half])
    gqa_score_bound = (HEAD_DIM * q_scale * BF16_NORM_MARGIN
                       * jnp.max(jnp.abs(q_norm[i])) * jnp.max(jnp.abs(k_norm[i])))
    return dict(
        ffn1=(ffn1_norm[i][None], w_gate_up, 0, _to_bf16(ffn1_w_down, i, FFN_RESIDUAL_SCALE), mix_norm[i][None]),
        w_rot=w_rot, rot_gains=rot_gains, w_qkv_a=w_qkv_a, w_proj=w_proj, proj_scale=proj_scale, gqa_score_bound=gqa_score_bound,
        merge=(_to_bf16(w_branch_a, i), _to_bf16(w_branch_b, i), _to_bf16(w_out, i)),
        ffn2=(ffn2_norm[i][None], w_gate_up, 1, _to_bf16(ffn2_w_down, i, FFN_RESIDUAL_SCALE), ple_norm[i][None]),
        ple=(_to_bf16(w_ple_gate, i), _to_bf16(w_ple_proj, i), final_norm[None]),
    )


def _encoder(xs, ps, wts, rot, col_bias):
    s, d = xs[0].shape[1:]
    assert all(x.shape[1:] == (s, d) for x in xs)
    b = sum(x.shape[0] for x in xs)
    t = b * s
    h1, u = _ffn([x.reshape(-1, d) for x in xs], *wts["ffn1"])
    q_a, k_a, v_a = _qkv_a(u, wts["w_rot"], wts["w_qkv_a"], rot, s)
    z = _proj(u, wts["w_proj"], wts["proj_scale"])
    qkv = (q_a.reshape(b, s, -1), k_a.reshape(b, s, -1), v_a.reshape(b, s, -1))
    o_a = lax.cond(wts["gqa_score_bound"] <= GQA_SAFE_LOG2_SCORE,
                   functools.partial(_gqa, bounded=True), functools.partial(_gqa, bounded=False), *qkv)
    o_b = _nbr(z.reshape(b, s, -1), col_bias, _QB_COL, _KB_COL, _VB_COL)
    h2 = _merge(o_a.reshape(t, -1), o_b.reshape(t, -1), z, _GA_COL, _GB_COL, h1, *wts["merge"])
    h3, n = _ffn([h2], *wts["ffn2"])
    row_starts = np.cumsum([0] + [x.shape[0] * s for x in xs])
    return tuple(_ple(h3, n, int(row0), p.reshape(-1, p.shape[-1]), *wts["ple"]).reshape(x.shape)
                 for row0, p, x in zip(row_starts, ps, xs))


def kernel(x_prompt, x_sample, p_prompt, p_sample, ffn1_norm, ffn1_w_gate, ffn1_w_up, ffn1_w_down, mix_norm, w_in, q_norm, k_norm, nat_rpb, w_branch_a, w_branch_b, w_out, ffn2_norm, ffn2_w_gate, ffn2_w_up, ffn2_w_down, ple_norm, w_ple_gate, w_ple_proj, final_norm):
    assert ffn1_norm.shape[0] == 1, "single-layer encoder"
    wts = _prepare_weights(ffn1_norm, ffn1_w_gate, ffn1_w_up, ffn1_w_down, mix_norm, w_in, q_norm, k_norm,
                           w_branch_a, w_branch_b, w_out, ffn2_norm, ffn2_w_gate, ffn2_w_up, ffn2_w_down,
                           ple_norm, w_ple_gate, w_ple_proj, final_norm, 0)
    col_bias = _nbr_col_bias(nat_rpb[0])
    groups = ((x_prompt, p_prompt[0]), (x_sample, p_sample[0]))
    outs = {}
    for seq in sorted({x.shape[1] for x, _ in groups}):
        members = [k for k, (x, _) in enumerate(groups) if x.shape[1] == seq]
        rot = _rotary_tables(seq, *wts["rot_gains"])
        ys = _encoder([groups[k][0] for k in members], [groups[k][1] for k in members], wts, rot, col_bias)
        outs.update(zip(members, ys))
    return tuple(outs[k] for k in range(len(groups)))
```
